```python
import math
import jax
import jax.numpy as jnp
from jax import lax
import numpy as np

D_MODEL = 1024
BATCH = 16
SEQ = 256
DEPTH = 4
DEC_BATCH = 8
DEC_SEQ = 4096
PAST_LEN = 512

GRID_W = 64
HEAD_DIM = 64
N_MIXERS = 4
GROUP_W = D_MODEL // N_MIXERS
GROUP_HEADS = GROUP_W // HEAD_DIM
MIX_W = N_MIXERS * GROUP_W
LORA_W = 64
LORA_A = 64
LORA_G = 128
RWKV_GN_EPS = 64e-5
DIFF_DQK = HEAD_DIM // 2
GQA_KV_HEADS = 2
GQA_GROUP = GROUP_HEADS // GQA_KV_HEADS
NA_ROWS = 8
NA_COLS = 16
Q_BLOCK = 128
ROPE_THETA = 10000.0
NORM_EPS = 1e-6
N_EXPERT_GROUPS = 4
EXPERTS_PER_GROUP = 8
N_EXPERTS = N_EXPERT_GROUPS * EXPERTS_PER_GROUP
TOP_K = 2
D_EXPERT = 512
MOE_BLOCK = 128
A_IN = 3 * GROUP_W + 2 * LORA_W + 2 * LORA_A + LORA_G
B_IN = 3 * GROUP_W
C_IN = 3 * GROUP_W
D_IN = GROUP_W + 2 * GQA_KV_HEADS * HEAD_DIM
IN_W = A_IN + B_IN + C_IN + D_IN

kernel_name = 'hybrid_diffusion_prefix_step'


def rmsnorm(x, g):
    xf = x.astype(jnp.float32)
    y = xf * lax.rsqrt(jnp.mean(xf * xf, axis=-1, keepdims=True) + NORM_EPS)
    return (y * g.astype(jnp.float32)).astype(x.dtype)


def token_shift(u, mu):
    prev = jnp.pad(u[:, :-1], ((0, 0), (1, 0), (0, 0)))
    nxt = jnp.pad(u[:, 1:], ((0, 0), (0, 1), (0, 0)))
    return u + mu[0] * (prev - u) + mu[1] * (nxt - u)


def axial_rope(t_len, dim):
    q4 = dim // 4
    inv = ROPE_THETA ** (-jnp.arange(q4, dtype=jnp.float32) / q4)
    t = jnp.arange(t_len)
    row = (t // GRID_W).astype(jnp.float32)
    col = (t % GRID_W).astype(jnp.float32)
    ang = jnp.stack([row[:, None] * inv, col[:, None] * inv], axis=1)
    return jnp.cos(ang), jnp.sin(ang)


def apply_rope(x, cs):
    cos, sin = cs
    xf = x.astype(jnp.float32).reshape(*x.shape[:-1], 2, 2, x.shape[-1] // 4)
    x1 = xf[..., 0, :]
    x2 = xf[..., 1, :]
    out = jnp.stack([x1 * cos - x2 * sin, x2 * cos + x1 * sin], axis=-2)
    return out.reshape(x.shape).astype(x.dtype)


def wkv_scan(s0, r, w, k, v, kk, a, reverse):
    def step(s, inp):
        r_t, w_t, k_t, v_t, kk_t, a_t = inp
        sa = jnp.einsum('bhvk,bhk->bhv', s, -kk_t)
        s = s * w_t[:, :, None, :] + sa[..., None] * (kk_t * a_t)[:, :, None, :] + v_t[..., None] * k_t[:, :, None, :]
        return s, jnp.einsum('bhvk,bhk->bhv', s, r_t)
    xs = tuple(jnp.moveaxis(z, 1, 0) for z in (r, w, k, v, kk, a))
    s_fin, o = lax.scan(step, s0.astype(jnp.float32), xs, reverse=reverse)
    return jnp.moveaxis(o, 0, 1), s_fin


def rwkv_time_mix(ua, s0, w0, w_up, a0, a_up, g_up, k_k, k_a, r_k, gn_w, gn_b):
    b, t, _ = ua.shape
    h, n = GROUP_HEADS, HEAD_DIM
    cuts = np.cumsum([GROUP_W, GROUP_W, GROUP_W, 2 * LORA_W, 2 * LORA_A]).tolist()
    r, k, v, wd, ad, gd = jnp.split(ua.astype(jnp.float32), cuts, axis=-1)
    wd = wd.reshape(b, t, 2, LORA_W)
    ad = ad.reshape(b, t, 2, LORA_A)
    w_raw = w0 + jnp.einsum('btdr,drc->btdc', jnp.tanh(wd), w_up)
    decay = jnp.exp(-jnp.exp(-jax.nn.softplus(-w_raw) - 0.5))
    a = jax.nn.sigmoid(a0 + jnp.einsum('btdr,drc->btdc', ad, a_up))
    g = jax.nn.sigmoid(gd) @ g_up
    kk = k[:, :, None, :] * k_k
    kd = k[:, :, None, :] * (1.0 + (a - 1.0) * k_a)

    def heads(z):
        return z.reshape(b, t, 2, h, n)
    decay, a, kk, kd = heads(decay), heads(a), heads(kk), heads(kd)
    kk = kk * lax.rsqrt(jnp.sum(kk * kk, axis=-1, keepdims=True) + 1e-12)
    rh = r.reshape(b, t, h, n)
    vh = v.reshape(b, t, h, n)
    o_f, s_f = wkv_scan(s0[:, 0], rh, decay[:, :, 0], kd[:, :, 0], vh, kk[:, :, 0], a[:, :, 0], False)
    o_b, s_b = wkv_scan(s0[:, 1], rh, decay[:, :, 1], kd[:, :, 1], vh, kk[:, :, 1], a[:, :, 1], True)
    o = o_f + o_b
    mu = jnp.mean(o, axis=-1, keepdims=True)
    var = jnp.mean(jnp.square(o - mu), axis=-1, keepdims=True)
    o = ((o - mu) * lax.rsqrt(var + RWKV_GN_EPS)).reshape(b, t, GROUP_W) * gn_w + gn_b
    bonus = jnp.sum(jnp.sum(rh[:, :, None] * kd * r_k, axis=-1, keepdims=True) * vh[:, :, None], axis=2)
    out = (o + bonus.reshape(b, t, GROUP_W)) * g
    return out.astype(ua.dtype), jnp.stack([s_f, s_b], axis=1).astype(ua.dtype)


def block_attention(q, k, v):
    b, kh, g, s, d = q.shape
    nb = s // Q_BLOCK
    scale = 1.0 / math.sqrt(d)
    qb = jnp.moveaxis(q.reshape(b, kh, g, nb, Q_BLOCK, d), 3, 0)

    def one(qi):
        sc = jnp.einsum('bkgqd,bkld->bkgql', qi, k, preferred_element_type=jnp.float32) * scale
        p = jax.nn.softmax(sc, axis=-1)
        return jnp.einsum('bkgql,bkld->bkgqd', p.astype(v.dtype), v)
    o = lax.map(one, qb)
    return jnp.moveaxis(o, 0, 3).reshape(b, kh, g, s, v.shape[-1])


def diff_attention(q, k, v, lam):
    b, h, _, s, d = q.shape
    nb = s // Q_BLOCK
    scale = 1.0 / math.sqrt(d)
    qb = jnp.moveaxis(q.reshape(b, h, 2, nb, Q_BLOCK, d), 3, 0)

    def one(qi):
        sc = jnp.einsum('bhiqd,bhild->bhiql', qi, k, preferred_element_type=jnp.float32) * scale
        p = jax.nn.softmax(sc, axis=-1)
        att = p[:, :, 0] - lam * p[:, :, 1]
        return jnp.einsum('bhql,bhlv->bhqv', att.astype(v.dtype), v)
    o = lax.map(one, qb)
    return jnp.moveaxis(o, 0, 2).reshape(b, h, s, v.shape[-1])


def neighborhood_attention(q, k, v, k_ctx, v_ctx, rel_bias):
    b, h, s, d = q.shape
    rows = s // GRID_W
    kr = min(NA_ROWS, rows)
    scale = 1.0 / math.sqrt(d)
    qg = q.reshape(b, h, rows, GRID_W, d)
    kg = k.reshape(b, h, rows, GRID_W, d)
    vg = v.reshape(b, h, rows, GRID_W, d)
    col_start = np.clip(np.arange(GRID_W) - NA_COLS // 2, 0, GRID_W - NA_COLS)
    col_idx = col_start[:, None] + np.arange(NA_COLS)
    col_bias_idx = col_idx - np.arange(GRID_W)[:, None] + NA_COLS - 1

    def one_row(r):
        rs = jnp.clip(r - kr // 2, 0, rows - kr)
        k_nb = lax.dynamic_slice_in_dim(kg, rs, kr, axis=2)[:, :, :, col_idx]
        v_nb = lax.dynamic_slice_in_dim(vg, rs, kr, axis=2)[:, :, :, col_idx]
        q_r = lax.dynamic_index_in_dim(qg, r, axis=2, keepdims=False)
        s_nb = jnp.einsum('bhwd,bhiwjd->bhwij', q_r, k_nb, preferred_element_type=jnp.float32) * scale
        row_bias_idx = rs + jnp.arange(kr) - r + NA_ROWS - 1
        bias = rel_bias[:, row_bias_idx][:, :, col_bias_idx]
        s_nb = s_nb + jnp.transpose(bias, (0, 2, 1, 3))[None].astype(jnp.float32)
        s_ctx = jnp.einsum('bhwd,bhld->bhwl', q_r, k_ctx, preferred_element_type=jnp.float32) * scale
        sc = jnp.concatenate([s_nb.reshape(b, h, GRID_W, kr * NA_COLS), s_ctx], axis=-1)
        p = jax.nn.softmax(sc, axis=-1).astype(v.dtype)
        p_nb = p[..., :kr * NA_COLS].reshape(b, h, GRID_W, kr, NA_COLS)
        p_ctx = p[..., kr * NA_COLS:]
        return jnp.einsum('bhwij,bhiwjd->bhwd', p_nb, v_nb) + jnp.einsum('bhwl,bhld->bhwd', p_ctx, v_ctx)
    o = lax.map(one_row, jnp.arange(rows))
    return jnp.moveaxis(o, 0, 2).reshape(b, h, s, d)


def hier_moe(h, w_grp, b_grp, w_rt, b_rt, w1, w3, w2):
    b, t, d = h.shape
    n = b * t
    x = h.reshape(n, d)
    g_logits = (x @ w_grp + b_grp).astype(jnp.float32)
    g_idx = jnp.argmax(g_logits, axis=-1)
    g_top = jnp.max(jax.nn.softmax(g_logits, axis=-1), axis=-1)
    e_logits = (x @ w_rt + b_rt).astype(jnp.float32).reshape(n, N_EXPERT_GROUPS, EXPERTS_PER_GROUP)
    e_logits = e_logits[jnp.arange(n), g_idx]
    top_v, top_i = lax.top_k(e_logits, TOP_K)
    gate = jax.nn.softmax(top_v, axis=-1) * g_top[:, None]
    eid = (g_idx[:, None] * EXPERTS_PER_GROUP + top_i).reshape(-1).astype(jnp.int32)
    tok = jnp.repeat(jnp.arange(n, dtype=jnp.int32), TOP_K)
    gw = gate.reshape(-1)
    n_assign = n * TOP_K
    counts = jax.ops.segment_sum(jnp.ones_like(eid), eid, num_segments=N_EXPERTS)
    padded = (counts + MOE_BLOCK - 1) // MOE_BLOCK * MOE_BLOCK
    pad_end = jnp.cumsum(padded)
    pad_start = pad_end - padded
    cnt_start = jnp.cumsum(counts) - counts
    order = jnp.argsort(eid)
    s_eid = eid[order]
    dest = pad_start[s_eid] + jnp.arange(n_assign, dtype=jnp.int32) - cnt_start[s_eid]
    n_blocks = -(-(n_assign + N_EXPERTS * (MOE_BLOCK - 1)) // MOE_BLOCK)
    cap = n_blocks * MOE_BLOCK
    buf_tok = jnp.full((cap,), n, jnp.int32).at[dest].set(tok[order])
    buf_w = jnp.zeros((cap,), jnp.float32).at[dest].set(gw[order])
    block_e = jnp.minimum(jnp.searchsorted(pad_end, jnp.arange(n_blocks, dtype=jnp.int32) * MOE_BLOCK, side='right'), N_EXPERTS - 1)
    x_pad = jnp.concatenate([x, jnp.zeros((1, d), x.dtype)], axis=0)
    xb = x_pad[buf_tok].reshape(n_blocks, MOE_BLOCK, d)

    def expert_block(args):
        xi, e = args
        return (jax.nn.silu(xi @ w1[e]) * (xi @ w3[e])) @ w2[e]
    yb = lax.map(expert_block, (xb, block_e))
    y = jax.ops.segment_sum(yb.reshape(cap, d) * buf_w[:, None].astype(x.dtype), buf_tok, num_segments=n + 1)[:n]
    return y.reshape(b, t, d)


def token_mixers(h, rope, ctx, l, W):
    b, t, _ = h.shape
    nh, n = GROUP_HEADS, HEAD_DIM
    latent = ctx is not None
    u = h @ W['w_in'][l]
    ua, ub, uc, ud = jnp.split(u, [A_IN, A_IN + B_IN, A_IN + B_IN + C_IN], axis=-1)

    ua = token_shift(ua, W['rw_shift'][l])
    s0 = ctx[0] if latent else jnp.zeros((b, 2, nh, n, n), jnp.float32)
    out_a, s_fin = rwkv_time_mix(ua, s0, W['rw_w0'][l], W['rw_w_up'][l], W['rw_a0'][l], W['rw_a_up'][l],
                                 W['rw_g_up'][l], W['rw_k_k'][l], W['rw_k_a'][l], W['rw_r_k'][l],
                                 W['rw_gn_w'][l], W['rw_gn_b'][l])

    qb, kb, vb = jnp.split(ub, 3, axis=-1)
    qb = rmsnorm(qb.reshape(b, t, nh, 2, DIFF_DQK).transpose(0, 2, 3, 1, 4), W['diff_qk_norm'][l, 0])
    kb = rmsnorm(kb.reshape(b, t, nh, 2, DIFF_DQK).transpose(0, 2, 3, 1, 4), W['diff_qk_norm'][l, 1])
    vb = vb.reshape(b, t, nh, n).transpose(0, 2, 1, 3)
    if latent:
        qb = apply_rope(qb, rope[0])
        kb_all = jnp.concatenate([ctx[1], apply_rope(kb, rope[0])], axis=3)
        vb_all = jnp.concatenate([ctx[2], vb], axis=2)
    else:
        kb_all, vb_all = kb, vb
    lam_init = 0.8 - 0.6 * math.exp(-0.3 * l)
    lv = W['diff_lambda'][l].astype(jnp.float32)
    lam = jnp.exp(jnp.sum(lv[0] * lv[1])) - jnp.exp(jnp.sum(lv[2] * lv[3])) + lam_init
    ob = diff_attention(qb, kb_all, vb_all, lam)
    ob = rmsnorm(ob, W['diff_subln'][l]) * (1.0 - lam_init)
    out_b = ob.transpose(0, 2, 1, 3).reshape(b, t, GROUP_W)

    qc, kc, vc = [z.reshape(b, t, nh, n).transpose(0, 2, 1, 3) for z in jnp.split(uc, 3, axis=-1)]
    qc = rmsnorm(qc, W['na_qk_norm'][l, 0])
    kc = rmsnorm(kc, W['na_qk_norm'][l, 1])
    if latent:
        oc = neighborhood_attention(qc, kc, vc, ctx[3], ctx[4], W['na_rel_bias'][l])
    else:
        oc = block_attention(qc[:, :, None], kc, vc)[:, :, 0]
    out_c = oc.transpose(0, 2, 1, 3).reshape(b, t, GROUP_W)

    qd, kd, vd = jnp.split(ud, [GROUP_W, GROUP_W + GQA_KV_HEADS * n], axis=-1)
    qd = rmsnorm(qd.reshape(b, t, GQA_KV_HEADS, GQA_GROUP, n).transpose(0, 2, 3, 1, 4), W['gqa_qk_norm'][l, 0])
    kd = rmsnorm(kd.reshape(b, t, GQA_KV_HEADS, n).transpose(0, 2, 1, 3), W['gqa_qk_norm'][l, 1])
    vd = vd.reshape(b, t, GQA_KV_HEADS, n).transpose(0, 2, 1, 3)
    if latent:
        qd = apply_rope(qd, rope[1])
        kd_all = jnp.concatenate([ctx[5], apply_rope(kd, rope[1])], axis=2)
        vd_all = jnp.concatenate([ctx[6], vd], axis=2)
    else:
        kd_all, vd_all = kd, vd
    od = block_attention(qd, kd_all, vd_all)
    out_d = od.transpose(0, 3, 1, 2, 4).reshape(b, t, GROUP_W)

    mix = jnp.concatenate([out_a, out_b, out_c, out_d], axis=-1)
    new_ctx = None if latent else (s_fin, kb, vb, kc, vc, kd, vd)
    return mix, new_ctx


def trunk_layer(x, cond, rope, ctx, l, W):
    mod = jax.nn.silu(cond) @ W['w_mod'][l] + W['b_mod'][l]
    sh1, sc1, g1, sh2, sc2, g2 = [m[:, None, :] for m in jnp.split(mod, 6, axis=-1)]
    h = rmsnorm(x, W['norm_mix'][l]) * (1 + sc1) + sh1
    mix, new_ctx = token_mixers(h, rope, ctx, l, W)
    x = x + g1 * (mix @ W['w_out'][l])
    h = rmsnorm(x, W['norm_ffn'][l]) * (1 + sc2) + sh2
    x = x + g2 * hier_moe(h, W['moe_w_group'][l], W['moe_b_group'][l], W['moe_w_router'][l],
                          W['moe_b_router'][l], W['moe_w1'][l], W['moe_w3'][l], W['moe_w2'][l])
    return x, new_ctx


def setup_inputs(seed: int = 0) -> dict:
    key = jax.random.key(seed)
    ks = iter(jax.random.split(key, 64))

    def nrm(shape, s):
        return jax.random.normal(next(ks), shape, jnp.float32) * s

    def uni(shape, lo, hi):
        return jax.random.uniform(next(ks), shape, jnp.float32, lo, hi)
    H, N, L = GROUP_HEADS, HEAD_DIM, PAST_LEN
    return {
        'x_prompt': nrm((BATCH, SEQ, D_MODEL), 1.0),
        'x_sample': nrm((DEC_BATCH, DEC_SEQ, D_MODEL), 1.0),
        'c': nrm((DEC_BATCH, D_MODEL), 1.0),
        'state_rwkv': nrm((DEC_BATCH, DEPTH, 2, H, N, N), 0.5),
        'cache_diff_k': nrm((DEC_BATCH, DEPTH, H, 2, L, DIFF_DQK), 1.0),
        'cache_diff_v': nrm((DEC_BATCH, DEPTH, H, L, N), 1.0),
        'cache_na_k': nrm((DEC_BATCH, DEPTH, H, L, N), 1.0),
        'cache_na_v': nrm((DEC_BATCH, DEPTH, H, L, N), 1.0),
        'cache_gqa_k': nrm((DEC_BATCH, DEPTH, GQA_KV_HEADS, L, N), 1.0),
        'cache_gqa_v': nrm((DEC_BATCH, DEPTH, GQA_KV_HEADS, L, N), 1.0),
        'c_ctx': nrm((D_MODEL,), 1.0),
        'norm_mix': 1.0 + nrm((DEPTH, D_MODEL), 0.02),
        'norm_ffn': 1.0 + nrm((DEPTH, D_MODEL), 0.02),
        'w_mod': nrm((DEPTH, D_MODEL, 6 * D_MODEL), 0.5 * D_MODEL ** -0.5),
        'b_mod': nrm((DEPTH, 6 * D_MODEL), 0.02),
        'w_in': nrm((DEPTH, D_MODEL, IN_W), D_MODEL ** -0.5),
        'w_out': nrm((DEPTH, MIX_W, D_MODEL), MIX_W ** -0.5),
        'rw_shift': uni((DEPTH, 2, A_IN), 0.0, 0.5),
        'rw_w0': -0.5 + nrm((DEPTH, 2, GROUP_W), 0.5),
        'rw_w_up': nrm((DEPTH, 2, LORA_W, GROUP_W), 0.1),
        'rw_a0': nrm((DEPTH, 2, GROUP_W), 0.5),
        'rw_a_up': nrm((DEPTH, 2, LORA_A, GROUP_W), 0.5 * LORA_A ** -0.5),
        'rw_g_up': nrm((DEPTH, LORA_G, GROUP_W), LORA_G ** -0.5),
        'rw_k_k': 0.85 + nrm((DEPTH, 2, GROUP_W), 0.05),
        'rw_k_a': 1.0 + nrm((DEPTH, 2, GROUP_W), 0.05),
        'rw_r_k': nrm((DEPTH, 2, H, N), 0.1),
        'rw_gn_w': 1.0 + nrm((DEPTH, GROUP_W), 0.02),
        'rw_gn_b': nrm((DEPTH, GROUP_W), 0.02),
        'diff_qk_norm': 1.0 + nrm((DEPTH, 2, DIFF_DQK), 0.02),
        'diff_lambda': nrm((DEPTH, 4, DIFF_DQK), 0.1),
        'diff_subln': 1.0 + nrm((DEPTH, N), 0.02),
        'na_qk_norm': 1.0 + nrm((DEPTH, 2, N), 0.02),
        'na_rel_bias': nrm((DEPTH, H, 2 * NA_ROWS - 1, 2 * NA_COLS - 1), 0.1),
        'gqa_qk_norm': 1.0 + nrm((DEPTH, 2, N), 0.02),
        'moe_w_group': nrm((DEPTH, D_MODEL, N_EXPERT_GROUPS), D_MODEL ** -0.5),
        'moe_b_group': nrm((DEPTH, N_EXPERT_GROUPS), 0.01),
        'moe_w_router': nrm((DEPTH, D_MODEL, N_EXPERTS), D_MODEL ** -0.5),
        'moe_b_router': nrm((DEPTH, N_EXPERTS), 0.01),
        'moe_w1': nrm((DEPTH, N_EXPERTS, D_MODEL, D_EXPERT), D_MODEL ** -0.5),
        'moe_w3': nrm((DEPTH, N_EXPERTS, D_MODEL, D_EXPERT), D_MODEL ** -0.5),
        'moe_w2': nrm((DEPTH, N_EXPERTS, D_EXPERT, D_MODEL), D_EXPERT ** -0.5),
    }


def reference(x_prompt, x_sample, c, state_rwkv, cache_diff_k, cache_diff_v, cache_na_k, cache_na_v,
              cache_gqa_k, cache_gqa_v, c_ctx, norm_mix, norm_ffn, w_mod, b_mod, w_in, w_out,
              rw_shift, rw_w0, rw_w_up, rw_a0, rw_a_up, rw_g_up, rw_k_k, rw_k_a, rw_r_k, rw_gn_w, rw_gn_b,
              diff_qk_norm, diff_lambda, diff_subln, na_qk_norm, na_rel_bias, gqa_qk_norm,
              moe_w_group, moe_b_group, moe_w_router, moe_b_router, moe_w1, moe_w3, moe_w2):
    W = {
        'norm_mix': norm_mix, 'norm_ffn': norm_ffn, 'w_mod': w_mod, 'b_mod': b_mod,
        'w_in': w_in, 'w_out': w_out, 'rw_shift': rw_shift, 'rw_w0': rw_w0, 'rw_w_up': rw_w_up,
        'rw_a0': rw_a0, 'rw_a_up': rw_a_up, 'rw_g_up': rw_g_up, 'rw_k_k': rw_k_k, 'rw_k_a': rw_k_a,
        'rw_r_k': rw_r_k, 'rw_gn_w': rw_gn_w, 'rw_gn_b': rw_gn_b, 'diff_qk_norm': diff_qk_norm,
        'diff_lambda': diff_lambda, 'diff_subln': diff_subln, 'na_qk_norm': na_qk_norm,
        'na_rel_bias': na_rel_bias, 'gqa_qk_norm': gqa_qk_norm, 'moe_w_group': moe_w_group,
        'moe_b_group': moe_b_group, 'moe_w_router': moe_w_router, 'moe_b_router': moe_b_router,
        'moe_w1': moe_w1, 'moe_w3': moe_w3, 'moe_w2': moe_w2,
    }
    xp = x_prompt
    cond_ctx = c_ctx[None, :]
    ctx_layers = []
    for l in range(DEPTH):
        xp, new_ctx = trunk_layer(xp, cond_ctx, None, None, l, W)
        ctx_layers.append(new_ctx)
    new_state_rwkv = jnp.stack([z[0] for z in ctx_layers], axis=1)
    new_diff_k = jnp.stack([z[1] for z in ctx_layers], axis=1)
    new_diff_v = jnp.stack([z[2] for z in ctx_layers], axis=1)
    new_na_k = jnp.stack([z[3] for z in ctx_layers], axis=1)
    new_na_v = jnp.stack([z[4] for z in ctx_layers], axis=1)
    new_gqa_k = jnp.stack([z[5] for z in ctx_layers], axis=1)
    new_gqa_v = jnp.stack([z[6] for z in ctx_layers], axis=1)

    t_lat = x_sample.shape[1]
    rope = (axial_rope(t_lat, DIFF_DQK), axial_rope(t_lat, HEAD_DIM))
    xs = x_sample
    for l in range(DEPTH):
        ctx = (state_rwkv[:, l], cache_diff_k[:, l], cache_diff_v[:, l], cache_na_k[:, l],
               cache_na_v[:, l], cache_gqa_k[:, l], cache_gqa_v[:, l])
        xs, _ = trunk_layer(xs, c, rope, ctx, l, W)
    return (xp, xs, new_state_rwkv, new_diff_k, new_diff_v, new_na_k, new_na_v, new_gqa_k, new_gqa_v)
```

```python
import functools
import math

import numpy as np
import jax
import jax.numpy as jnp
from jax import lax
from jax.experimental import pallas as pl
from jax.experimental.pallas import tpu as pltpu

F32 = jnp.float32
BF16 = jnp.bfloat16

D_MODEL = 1024
GRID_W = 64
HEAD_DIM = 64
GROUP_W = 256
GROUP_HEADS = 4
LORA_W = 64
LORA_A = 64
LORA_G = 128
RWKV_GN_EPS = 64e-5
DIFF_DQK = 32
GQA_KV_HEADS = 2
GQA_GROUP = 2
NA_ROWS = 8
NA_COLS = 16
ROPE_THETA = 10000.0
NORM_EPS = 1e-6
N_EXPERT_GROUPS = 4
EXPERTS_PER_GROUP = 8
N_EXPERTS = 32
TOP_K = 2
D_EXPERT = 512
A_IN = 3 * GROUP_W + 2 * LORA_W + 2 * LORA_A + LORA_G
B_IN = 3 * GROUP_W
C_IN = 3 * GROUP_W
D_IN = GROUP_W + 2 * GQA_KV_HEADS * HEAD_DIM
IN_W = A_IN + B_IN + C_IN + D_IN

LANES = 128
SUBLANES = 8
VMEM_LIMIT = 48 * 1024 * 1024

ROW_TILE = 256
Q_TILE = 256
SCAN_STEPS = 16
MOE_TILE = 256
ROUTER_W = 128
MASK_VALUE = -1e30


def _cparams(sem):
    return pltpu.CompilerParams(dimension_semantics=sem, vmem_limit_bytes=VMEM_LIMIT)


def _mod_kernel(c_ref, w_ref, b_ref, o_ref):
    c = c_ref[...]
    a = c * jax.nn.sigmoid(c)
    o_ref[...] = jnp.dot(a, w_ref[...], preferred_element_type=F32,
                         precision=lax.Precision.HIGHEST) + b_ref[...]


def _modulation(cond, w, b):
    m, d = cond.shape
    n = w.shape[1]
    mp = -(-m // SUBLANES) * SUBLANES
    cp = jnp.pad(cond, ((0, mp - m), (0, 0)))
    tn = 768
    out = pl.pallas_call(
        _mod_kernel,
        out_shape=jax.ShapeDtypeStruct((mp, n), F32),
        grid=(n // tn,),
        in_specs=[pl.BlockSpec((mp, d), lambda j: (0, 0)),
                  pl.BlockSpec((d, tn), lambda j: (0, j)),
                  pl.BlockSpec((1, tn), lambda j: (0, j))],
        out_specs=pl.BlockSpec((mp, tn), lambda j: (0, j)),
        compiler_params=_cparams(("arbitrary",)),
        name="modulation",
    )(cp, w, b.reshape(1, n))
    return out[:m]


def _inproj_kernel(x_ref, g_ref, sc_ref, sh_ref, w_ref, oa_ref, ob_ref, oc_ref, od_ref):
    x = x_ref[...]
    ms = jnp.mean(x * x, axis=-1, keepdims=True)
    h = x * lax.rsqrt(ms + NORM_EPS) * g_ref[...]
    h = h * sc_ref[0] + sh_ref[0]
    u = jnp.dot(h.astype(BF16), w_ref[...].astype(BF16), preferred_element_type=F32)
    oa_ref[...] = u[:, :A_IN]
    ob_ref[...] = u[:, A_IN:A_IN + B_IN]
    oc_ref[...] = u[:, A_IN + B_IN:A_IN + B_IN + C_IN]
    od_ref[...] = u[:, A_IN + B_IN + C_IN:]


def _in_projection(x2, gain, scale1p, shift, w_in, t_len):
    n, d = x2.shape
    bm = scale1p.shape[0]
    tm = ROW_TILE
    per_seq = t_len // tm

    def mod_idx(i):
        return ((i // per_seq) if bm > 1 else 0, 0, 0)

    widths = (A_IN, B_IN, C_IN, D_IN)
    return pl.pallas_call(
        _inproj_kernel,
        out_shape=tuple(jax.ShapeDtypeStruct((n, w), F32) for w in widths),
        grid=(n // tm,),
        in_specs=[pl.BlockSpec((tm, d), lambda i: (i, 0)),
                  pl.BlockSpec((1, d), lambda i: (0, 0)),
                  pl.BlockSpec((1, 1, d), mod_idx),
                  pl.BlockSpec((1, 1, d), mod_idx),
                  pl.BlockSpec((d, IN_W), lambda i: (0, 0))],
        out_specs=tuple(pl.BlockSpec((tm, w), lambda i: (i, 0)) for w in widths),
        compiler_params=_cparams(("arbitrary",)),
        name="in_projection",
    )(x2, gain.reshape(1, d), scale1p.reshape(bm, 1, d), shift.reshape(bm, 1, d), w_in)


def _outproj_kernel(mix_ref, w_ref, x_ref, g1_ref, gn_ref, sc_ref, sh_ref, wr_ref, br_ref,
                    xo_ref, h_ref, lg_ref):
    y = jnp.dot(mix_ref[...], w_ref[...].astype(BF16), preferred_element_type=F32)
    xn = x_ref[...] + g1_ref[0] * y
    xo_ref[...] = xn
    ms = jnp.mean(xn * xn, axis=-1, keepdims=True)
    h = xn * lax.rsqrt(ms + NORM_EPS) * gn_ref[...]
    h = h * sc_ref[0] + sh_ref[0]
    h_ref[...] = h.astype(BF16)
    lg_ref[...] = jnp.dot(h, wr_ref[...], preferred_element_type=F32,
                          precision=lax.Precision.HIGHEST) + br_ref[...]


def _out_projection(mix, w_out, x2, gate1, gain, scale1p, shift, w_router, b_router, t_len):
    n, d = x2.shape
    bm = gate1.shape[0]
    tm = ROW_TILE
    per_seq = t_len // tm

    def mod_idx(i):
        return ((i // per_seq) if bm > 1 else 0, 0, 0)

    mod_spec = pl.BlockSpec((1, 1, d), mod_idx)
    return pl.pallas_call(
        _outproj_kernel,
        out_shape=(jax.ShapeDtypeStruct((n, d), F32),
                   jax.ShapeDtypeStruct((n, d), BF16),
                   jax.ShapeDtypeStruct((n, ROUTER_W), F32)),
        grid=(n // tm,),
        in_specs=[pl.BlockSpec((tm, d), lambda i: (i, 0)),
                  pl.BlockSpec((d, d), lambda i: (0, 0)),
                  pl.BlockSpec((tm, d), lambda i: (i, 0)),
                  mod_spec,
                  pl.BlockSpec((1, d), lambda i: (0, 0)),
                  mod_spec, mod_spec,
                  pl.BlockSpec((d, ROUTER_W), lambda i: (0, 0)),
                  pl.BlockSpec((1, ROUTER_W), lambda i: (0, 0))],
        out_specs=(pl.BlockSpec((tm, d), lambda i: (i, 0)),
                   pl.BlockSpec((tm, d), lambda i: (i, 0)),
                   pl.BlockSpec((tm, ROUTER_W), lambda i: (i, 0))),
        compiler_params=_cparams(("arbitrary",)),
        name="out_projection",
    )(mix, w_out, x2, gate1.reshape(bm, 1, d), gain.reshape(1, d), scale1p.reshape(bm, 1, d),
      shift.reshape(bm, 1, d), w_router, b_router)


def _attn_kernel(q_ref, kt_ref, v_ref, o_ref):
    s = jnp.dot(q_ref[0, 0], kt_ref[0, 0], preferred_element_type=F32)
    m = jnp.max(s, axis=-1, keepdims=True)
    p = jnp.exp(s - m)
    l = jnp.sum(p, axis=-1, keepdims=True)
    o = jnp.dot(p.astype(BF16), v_ref[0, 0], preferred_element_type=F32)
    o_ref[0, 0] = o / l


def _attention(q, kt, v, group):
    b, hq, s, d = q.shape
    l = kt.shape[-1]
    dv = v.shape[-1]
    tq = min(Q_TILE, s)
    return pl.pallas_call(
        _attn_kernel,
        out_shape=jax.ShapeDtypeStruct((b, hq, s, dv), F32),
        grid=(b, hq, s // tq),
        in_specs=[pl.BlockSpec((1, 1, tq, d), lambda bi, h, i: (bi, h, i, 0)),
                  pl.BlockSpec((1, 1, d, l), lambda bi, h, i: (bi, h // group, 0, 0)),
                  pl.BlockSpec((1, 1, l, dv), lambda bi, h, i: (bi, h // group, 0, 0))],
        out_specs=pl.BlockSpec((1, 1, tq, dv), lambda bi, h, i: (bi, h, i, 0)),
        compiler_params=_cparams(("arbitrary", "arbitrary", "arbitrary")),
        name="attention",
    )(q, kt, v)


def _diff_kernel(sc_ref, q_ref, kt_ref, v_ref, g_ref, o_ref):
    lam = sc_ref[0]
    out_scale = sc_ref[1]
    s1 = jnp.dot(q_ref[0, 0, 0], kt_ref[0, 0, 0], preferred_element_type=F32)
    s2 = jnp.dot(q_ref[0, 0, 1], kt_ref[0, 0, 1], preferred_element_type=F32)
    p1 = jnp.exp(s1 - jnp.max(s1, axis=-1, keepdims=True))
    p2 = jnp.exp(s2 - jnp.max(s2, axis=-1, keepdims=True))
    c1 = 1.0 / jnp.sum(p1, axis=-1, keepdims=True)
    c2 = lam / jnp.sum(p2, axis=-1, keepdims=True)
    att = p1 * c1 - p2 * c2
    o = jnp.dot(att.astype(BF16), v_ref[0, 0], preferred_element_type=F32)
    ms = jnp.mean(o * o, axis=-1, keepdims=True)
    o_ref[0, 0] = o * lax.rsqrt(ms + NORM_EPS) * g_ref[...] * out_scale


def _diff_attention(q, kt, v, scalars, subln):
    b, h, _, s, d = q.shape
    l = kt.shape[-1]
    dv = v.shape[-1]
    tq = min(Q_TILE, s)
    return pl.pallas_call(
        _diff_kernel,
        out_shape=jax.ShapeDtypeStruct((b, h, s, dv), F32),
        grid_spec=pltpu.PrefetchScalarGridSpec(
            num_scalar_prefetch=1,
            grid=(b, h, s // tq),
            in_specs=[pl.BlockSpec((1, 1, 2, tq, d), lambda bi, hi, i, sc: (bi, hi, 0, i, 0)),
                      pl.BlockSpec((1, 1, 2, d, l), lambda bi, hi, i, sc: (bi, hi, 0, 0, 0)),
                      pl.BlockSpec((1, 1, l, dv), lambda bi, hi, i, sc: (bi, hi, 0, 0)),
                      pl.BlockSpec((1, dv), lambda bi, hi, i, sc: (0, 0))],
            out_specs=pl.BlockSpec((1, 1, tq, dv), lambda bi, hi, i, sc: (bi, hi, i, 0))),
        compiler_params=_cparams(("arbitrary", "arbitrary", "arbitrary")),
        name="diff_attention",
    )(scalars, q, kt, v, subln.reshape(1, dv))


def _na_kernel(q_ref, k_ref, v_ref, kct_ref, vc_ref, bias_ref, o_ref, *, rows):
    r = pl.program_id(1)
    rs = jnp.clip(r - NA_ROWS // 2, 0, rows - NA_ROWS)
    start = pl.multiple_of(rs * GRID_W, GRID_W)
    win = NA_ROWS * GRID_W
    for h in range(GROUP_HEADS):
        q = q_ref[0, h]
        kw = k_ref[0, h, pl.ds(start, win), :]
        vw = v_ref[0, h, pl.ds(start, win), :]
        s_nb = lax.dot_general(q, kw, (((1,), (1,)), ((), ())), preferred_element_type=F32)
        s_nb = s_nb + bias_ref[0, h]
        s_cx = jnp.dot(q, kct_ref[0, h], preferred_element_type=F32)
        m = jnp.maximum(jnp.max(s_nb, axis=-1, keepdims=True), jnp.max(s_cx, axis=-1, keepdims=True))
        p_nb = jnp.exp(s_nb - m)
        p_cx = jnp.exp(s_cx - m)
        l = jnp.sum(p_nb, axis=-1, keepdims=True) + jnp.sum(p_cx, axis=-1, keepdims=True)
        o = (jnp.dot(p_nb.astype(BF16), vw, preferred_element_type=F32)
             + jnp.dot(p_cx.astype(BF16), vc_ref[0, h], preferred_element_type=F32))
        o_ref[0, h] = o / l


def _na_bias_table(rel_bias):
    w = np.arange(GRID_W)
    col_start = np.clip(w - NA_COLS // 2, 0, GRID_W - NA_COLS)
    c = np.arange(GRID_W)
    valid = (c[None, :] >= col_start[:, None]) & (c[None, :] < col_start[:, None] + NA_COLS)
    cidx = np.clip(c[None, :] - w[:, None] + NA_COLS - 1, 0, 2 * NA_COLS - 2)
    delta = np.arange(NA_ROWS)
    i = np.arange(NA_ROWS)
    ridx = i[None, :] - delta[:, None] + NA_ROWS - 1
    t = rel_bias[:, ridx]
    t = t[:, :, :, cidx]
    t = jnp.where(valid[None, None, None], t, MASK_VALUE)
    t = jnp.transpose(t, (1, 0, 3, 2, 4))
    return t.reshape(NA_ROWS, rel_bias.shape[0], GRID_W, NA_ROWS * GRID_W).astype(F32)


def _neighborhood_attention(q, k, v, kct, vc, bias_tab):
    b, h, s, d = q.shape
    l = kct.shape[-1]
    rows = s // GRID_W
    assert rows >= NA_ROWS
    half = NA_ROWS // 2

    def bias_idx(bi, r):
        return (r - jnp.clip(r - half, 0, rows - NA_ROWS), 0, 0, 0)

    return pl.pallas_call(
        functools.partial(_na_kernel, rows=rows),
        out_shape=jax.ShapeDtypeStruct((b, h, s, d), F32),
        grid=(b, rows),
        in_specs=[pl.BlockSpec((1, h, GRID_W, d), lambda bi, r: (bi, 0, r, 0)),
                  pl.BlockSpec((1, h, s, d), lambda bi, r: (bi, 0, 0, 0)),
                  pl.BlockSpec((1, h, s, d), lambda bi, r: (bi, 0, 0, 0)),
                  pl.BlockSpec((1, h, d, l), lambda bi, r: (bi, 0, 0, 0)),
                  pl.BlockSpec((1, h, l, d), lambda bi, r: (bi, 0, 0, 0)),
                  pl.BlockSpec((1, h, GRID_W, NA_ROWS * GRID_W), bias_idx)],
        out_specs=pl.BlockSpec((1, h, GRID_W, d), lambda bi, r: (bi, 0, r, 0)),
        compiler_params=_cparams(("arbitrary", "arbitrary")),
        name="neighborhood_attention",
    )(q, k, v, kct, vc, bias_tab)


N_KOPS = 5
V_ROWS = HEAD_DIM // 2


def _scan_kernel(kop_ref, v_ref, s0_ref, y_ref, sfin_ref, st_ref, *, steps):
    t = pl.program_id(1)

    @pl.when(t == 0)
    def _():
        st_ref[...] = s0_ref[...]

    nacc = 4

    def tree(parts):
        return (parts[0] + parts[1]) + (parts[2] + parts[3])

    def step(s, carry):
        parts = [None] * nacc
        for k in range(HEAD_DIM):
            term = st_ref[k] * kop_ref[s, 0, pl.ds(k, 1), :]
            j = k % nacc
            parts[j] = term if parts[j] is None else parts[j] + term
        sa = -tree(parts)
        vv = v_ref[s]
        yp = [None] * nacc
        for k in range(HEAD_DIM):
            w = kop_ref[s, 1, pl.ds(k, 1), :]
            ka = kop_ref[s, 2, pl.ds(k, 1), :]
            kd = kop_ref[s, 3, pl.ds(k, 1), :]
            rr = kop_ref[s, 4, pl.ds(k, 1), :]
            sn = st_ref[k] * w + sa * ka + vv * kd
            st_ref[k] = sn
            term = sn * rr
            j = k % nacc
            yp[j] = term if yp[j] is None else yp[j] + term
        y_ref[s] = tree(yp)
        return carry

    lax.fori_loop(0, steps, step, 0)

    @pl.when(t == pl.num_programs(1) - 1)
    def _():
        sfin_ref[...] = st_ref[...]


def _rwkv_scan(kops, vop, s0):
    t_len, _, kdim, ltot = kops.shape
    groups = ltot // LANES
    ts = SCAN_STEPS
    return pl.pallas_call(
        functools.partial(_scan_kernel, steps=ts),
        out_shape=(jax.ShapeDtypeStruct((t_len, V_ROWS, ltot), F32),
                   jax.ShapeDtypeStruct((kdim, V_ROWS, ltot), F32)),
        grid=(groups, t_len // ts),
        in_specs=[pl.BlockSpec((ts, N_KOPS, kdim, LANES), lambda g, t: (t, 0, 0, g)),
                  pl.BlockSpec((ts, V_ROWS, LANES), lambda g, t: (t, 0, g)),
                  pl.BlockSpec((kdim, V_ROWS, LANES), lambda g, t: (0, 0, g))],
        out_specs=(pl.BlockSpec((ts, V_ROWS, LANES), lambda g, t: (t, 0, g)),
                   pl.BlockSpec((kdim, V_ROWS, LANES), lambda g, t: (0, 0, g))),
        scratch_shapes=[pltpu.VMEM((kdim, V_ROWS, LANES), F32)],
        compiler_params=_cparams(("arbitrary", "arbitrary")),
        name="rwkv_scan",
    )(kops, vop, s0)


def _token_shift(u, mu):
    prev = jnp.pad(u[:, :-1], ((0, 0), (1, 0), (0, 0)))
    nxt = jnp.pad(u[:, 1:], ((0, 0), (0, 1), (0, 0)))
    return u + mu[0] * (prev - u) + mu[1] * (nxt - u)


def _rwkv_time_mix(ua, s0, w0, w_up, a0, a_up, g_up, k_k, k_a, r_k, gn_w, gn_b):
    b, t, _ = ua.shape
    h, n = GROUP_HEADS, HEAD_DIM
    cuts = np.cumsum([GROUP_W, GROUP_W, GROUP_W, 2 * LORA_W, 2 * LORA_A]).tolist()
    r, k, v, wd, ad, gd = jnp.split(ua, cuts, axis=-1)
    wd = wd.reshape(b, t, 2, LORA_W)
    ad = ad.reshape(b, t, 2, LORA_A)
    w_raw = w0 + jnp.einsum('btdr,drc->btdc', jnp.tanh(wd), w_up)
    decay = jnp.exp(-jnp.exp(-jax.nn.softplus(-w_raw) - 0.5))
    a = jax.nn.sigmoid(a0 + jnp.einsum('btdr,drc->btdc', ad, a_up))
    g = jax.nn.sigmoid(gd) @ g_up
    kk = k[:, :, None, :] * k_k
    kd = k[:, :, None, :] * (1.0 + (a - 1.0) * k_a)

    def heads(z):
        return z.reshape(b, t, 2, h, n)
    decay, a, kk, kd = heads(decay), heads(a), heads(kk), heads(kd)
    kk = kk * lax.rsqrt(jnp.sum(kk * kk, axis=-1, keepdims=True) + 1e-12)
    rh = r.reshape(b, t, h, n)
    vh = v.reshape(b, t, h, n)

    nchain = b * h * 2

    def time_major(z):
        z = jnp.stack([z[:, :, 0], z[:, ::-1, 1]], axis=2)
        return jnp.transpose(z, (1, 4, 0, 3, 2)).reshape(t, n, nchain)

    r2 = jnp.broadcast_to(rh[:, :, None], (b, t, 2, h, n))
    kops = jnp.stack([time_major(z) for z in (kk, decay, kk * a, kd, r2)], axis=1)
    kops = jnp.concatenate([kops, kops], axis=-1)
    v2 = jnp.broadcast_to(vh[:, :, None], (b, t, 2, h, n))
    v2 = jnp.stack([v2[:, :, 0], v2[:, ::-1, 1]], axis=2).reshape(b, t, 2, h, 2, V_ROWS)
    vop = jnp.transpose(v2, (1, 5, 4, 0, 3, 2)).reshape(t, V_ROWS, 2 * nchain)
    s0l = s0.astype(F32).reshape(b, 2, h, 2, V_ROWS, n)
    s0l = jnp.transpose(s0l, (5, 4, 3, 0, 2, 1)).reshape(n, V_ROWS, 2 * nchain)

    lane_pad = -(2 * nchain) % LANES
    if lane_pad:
        kops, vop, s0l = [jnp.pad(z, [(0, 0)] * (z.ndim - 1) + [(0, lane_pad)]) for z in (kops, vop, s0l)]
    y, s_fin = _rwkv_scan(kops, vop, s0l)
    y = y[..., :2 * nchain]
    s_fin = s_fin[..., :2 * nchain]

    y = y.reshape(t, V_ROWS, 2, b, h, 2)
    y = jnp.transpose(y, (3, 5, 0, 4, 2, 1)).reshape(b, 2, t, h, n)
    o = y[:, 0] + y[:, 1, ::-1]
    s_fin = s_fin.reshape(n, V_ROWS, 2, b, h, 2)
    s_fin = jnp.transpose(s_fin, (3, 5, 4, 2, 1, 0)).reshape(b, 2, h, n, n)

    mu = jnp.mean(o, axis=-1, keepdims=True)
    var = jnp.mean(jnp.square(o - mu), axis=-1, keepdims=True)
    o = ((o - mu) * lax.rsqrt(var + RWKV_GN_EPS)).reshape(b, t, GROUP_W) * gn_w + gn_b
    bonus = jnp.sum(jnp.sum(rh[:, :, None] * kd * r_k, axis=-1, keepdims=True) * vh[:, :, None], axis=2)
    out = (o + bonus.reshape(b, t, GROUP_W)) * g
    return out, s_fin


def _expert_kernel(be_ref, nu_ref, x_ref, w1_ref, w3_ref, w2_ref, o_ref, w1s, w3s, w2s):
    i = pl.program_id(0)
    e = be_ref[i]
    prev = be_ref[jnp.maximum(i - 1, 0)]

    @pl.when((i == 0) | (e != prev))
    def _():
        w1s[...] = w1_ref[0].astype(BF16)
        w3s[...] = w3_ref[0].astype(BF16)
        w2s[...] = w2_ref[0].astype(BF16)

    @pl.when(i < nu_ref[0])
    def _():
        x = x_ref[...]
        a = jnp.dot(x, w1s[...], preferred_element_type=F32)
        g = jnp.dot(x, w3s[...], preferred_element_type=F32)
        hmid = (a * jax.nn.sigmoid(a)) * g
        o_ref[...] = jnp.dot(hmid.astype(BF16), w2s[...], preferred_element_type=F32)

    @pl.when(i >= nu_ref[0])
    def _():
        o_ref[...] = jnp.zeros_like(o_ref)


def _expert_mlp(xb, block_e, n_used, w1, w3, w2):
    cap, d = xb.shape
    bm = MOE_TILE
    de = w1.shape[-1]
    return pl.pallas_call(
        _expert_kernel,
        out_shape=jax.ShapeDtypeStruct((cap, d), F32),
        grid_spec=pltpu.PrefetchScalarGridSpec(
            num_scalar_prefetch=2,
            grid=(cap // bm,),
            in_specs=[pl.BlockSpec((bm, d), lambda i, be, nu: (i, 0)),
                      pl.BlockSpec((1, d, de), lambda i, be, nu: (be[i], 0, 0)),
                      pl.BlockSpec((1, d, de), lambda i, be, nu: (be[i], 0, 0)),
                      pl.BlockSpec((1, de, d), lambda i, be, nu: (be[i], 0, 0))],
            out_specs=pl.BlockSpec((bm, d), lambda i, be, nu: (i, 0)),
            scratch_shapes=[pltpu.VMEM((d, de), BF16), pltpu.VMEM((d, de), BF16), pltpu.VMEM((de, d), BF16)]),
        compiler_params=_cparams(("arbitrary",)),
        name="expert_mlp",
    )(block_e, n_used, xb, w1, w3, w2)


def _hier_moe(h_bf, logits, w1, w3, w2):
    n, d = h_bf.shape
    bm = MOE_TILE
    g_logits = logits[:, :N_EXPERT_GROUPS]
    g_idx = jnp.argmax(g_logits, axis=-1)
    g_top = jnp.max(jax.nn.softmax(g_logits, axis=-1), axis=-1)
    e_logits = logits[:, N_EXPERT_GROUPS:N_EXPERT_GROUPS + N_EXPERTS].reshape(n, N_EXPERT_GROUPS, EXPERTS_PER_GROUP)
    e_logits = jnp.take_along_axis(e_logits, g_idx[:, None, None], axis=1)[:, 0]
    top_v, top_i = lax.top_k(e_logits, TOP_K)
    gate = jax.nn.softmax(top_v, axis=-1) * g_top[:, None]
    eid = (g_idx[:, None] * EXPERTS_PER_GROUP + top_i).reshape(-1).astype(jnp.int32)
    tok = jnp.repeat(jnp.arange(n, dtype=jnp.int32), TOP_K)
    n_assign = n * TOP_K
    onehot = (eid[:, None] == jnp.arange(N_EXPERTS, dtype=jnp.int32)[None, :]).astype(jnp.int32)
    csum = jnp.cumsum(onehot, axis=0)
    counts = csum[-1]
    rank = jnp.take_along_axis(csum, eid[:, None], axis=1)[:, 0] - 1
    padded = (counts + bm - 1) // bm * bm
    pad_end = jnp.cumsum(padded)
    pad_start = pad_end - padded
    dest = pad_start[eid] + rank
    n_blocks = -(-(n_assign + N_EXPERTS * (bm - 1)) // bm)
    cap = n_blocks * bm
    buf_tok = jnp.zeros((cap,), jnp.int32).at[dest].set(tok)
    block_e = jnp.minimum(jnp.searchsorted(pad_end, jnp.arange(n_blocks, dtype=jnp.int32) * bm, side='right'),
                          N_EXPERTS - 1).astype(jnp.int32)
    n_used = (pad_end[-1:] // bm).astype(jnp.int32)
    xb = h_bf[buf_tok]
    yb = _expert_mlp(xb, block_e, n_used, w1, w3, w2)
    d2 = dest.reshape(n, TOP_K)
    return yb[d2[:, 0]] * gate[:, 0:1] + yb[d2[:, 1]] * gate[:, 1:2]


def _rms(x, g):
    return x * lax.rsqrt(jnp.mean(x * x, axis=-1, keepdims=True) + NORM_EPS) * g


def _axial_rope(t_len, dim):
    q4 = dim // 4
    inv = ROPE_THETA ** (-jnp.arange(q4, dtype=F32) / q4)
    t = jnp.arange(t_len)
    row = (t // GRID_W).astype(F32)
    col = (t % GRID_W).astype(F32)
    ang = jnp.stack([row[:, None] * inv, col[:, None] * inv], axis=1)
    return jnp.cos(ang), jnp.sin(ang)


def _apply_rope(x, cs):
    cos, sin = cs
    xf = x.reshape(*x.shape[:-1], 2, 2, x.shape[-1] // 4)
    x1 = xf[..., 0, :]
    x2 = xf[..., 1, :]
    out = jnp.stack([x1 * cos - x2 * sin, x2 * cos + x1 * sin], axis=-2)
    return out.reshape(x.shape)


def _token_mixers(ua, ub, uc, ud, rope, ctx, l, W):
    b, t, _ = ua.shape
    nh, n = GROUP_HEADS, HEAD_DIM
    latent = ctx is not None

    ua = _token_shift(ua, W['rw_shift'][l])
    s0 = ctx[0] if latent else jnp.zeros((b, 2, nh, n, n), F32)
    out_a, s_fin = _rwkv_time_mix(ua, s0, W['rw_w0'][l], W['rw_w_up'][l], W['rw_a0'][l], W['rw_a_up'][l],
                                  W['rw_g_up'][l], W['rw_k_k'][l], W['rw_k_a'][l], W['rw_r_k'][l],
                                  W['rw_gn_w'][l], W['rw_gn_b'][l])

    qb, kb, vb = jnp.split(ub, 3, axis=-1)
    qb = _rms(qb.reshape(b, t, nh, 2, DIFF_DQK).transpose(0, 2, 3, 1, 4), W['diff_qk_norm'][l, 0])
    kb = _rms(kb.reshape(b, t, nh, 2, DIFF_DQK).transpose(0, 2, 3, 1, 4), W['diff_qk_norm'][l, 1])
    vb = vb.reshape(b, t, nh, n).transpose(0, 2, 1, 3)
    if latent:
        qb = _apply_rope(qb, rope[0])
        kb_all = jnp.concatenate([ctx[1], _apply_rope(kb, rope[0])], axis=3)
        vb_all = jnp.concatenate([ctx[2], vb], axis=2)
    else:
        kb_all, vb_all = kb, vb
    lam_init = 0.8 - 0.6 * math.exp(-0.3 * l)
    lv = W['diff_lambda'][l].astype(F32)
    lam = jnp.exp(jnp.sum(lv[0] * lv[1])) - jnp.exp(jnp.sum(lv[2] * lv[3])) + lam_init
    scalars = jnp.stack([lam, jnp.asarray(1.0 - lam_init, F32)]).astype(F32)
    ob = _diff_attention((qb * (1.0 / math.sqrt(DIFF_DQK))).astype(BF16),
                         jnp.swapaxes(kb_all, -1, -2).astype(BF16), vb_all.astype(BF16),
                         scalars, W['diff_subln'][l])
    out_b = ob.transpose(0, 2, 1, 3).reshape(b, t, GROUP_W)

    qc, kc, vc = [z.reshape(b, t, nh, n).transpose(0, 2, 1, 3) for z in jnp.split(uc, 3, axis=-1)]
    qc = _rms(qc, W['na_qk_norm'][l, 0])
    kc = _rms(kc, W['na_qk_norm'][l, 1])
    qcs = (qc * (1.0 / math.sqrt(n))).astype(BF16)
    if latent:
        oc = _neighborhood_attention(qcs, kc.astype(BF16), vc.astype(BF16),
                                     jnp.swapaxes(ctx[3], -1, -2).astype(BF16), ctx[4].astype(BF16),
                                     _na_bias_table(W['na_rel_bias'][l]))
    else:
        oc = _attention(qcs, jnp.swapaxes(kc, -1, -2).astype(BF16), vc.astype(BF16), 1)
    out_c = oc.transpose(0, 2, 1, 3).reshape(b, t, GROUP_W)

    qd, kd, vd = jnp.split(ud, [GROUP_W, GROUP_W + GQA_KV_HEADS * n], axis=-1)
    qd = _rms(qd.reshape(b, t, GQA_KV_HEADS, GQA_GROUP, n).transpose(0, 2, 3, 1, 4), W['gqa_qk_norm'][l, 0])
    kd = _rms(kd.reshape(b, t, GQA_KV_HEADS, n).transpose(0, 2, 1, 3), W['gqa_qk_norm'][l, 1])
    vd = vd.reshape(b, t, GQA_KV_HEADS, n).transpose(0, 2, 1, 3)
    if latent:
        qd = _apply_rope(qd, rope[1])
        kd_all = jnp.concatenate([ctx[5], _apply_rope(kd, rope[1])], axis=2)
        vd_all = jnp.concatenate([ctx[6], vd], axis=2)
    else:
        kd_all, vd_all = kd, vd
    qds = (qd * (1.0 / math.sqrt(n))).astype(BF16).reshape(b, nh, t, n)
    od = _attention(qds, jnp.swapaxes(kd_all, -1, -2).astype(BF16), vd_all.astype(BF16), GQA_GROUP)
    out_d = od.transpose(0, 2, 1, 3).reshape(b, t, GROUP_W)

    mix = jnp.concatenate([out_a, out_b, out_c, out_d], axis=-1)
    new_ctx = None if latent else (s_fin, kb, vb, kc, vc, kd, vd)
    return mix, new_ctx


def _trunk_layer(x, cond, rope, ctx, l, W):
    b, t, d = x.shape
    n = b * t
    mod = _modulation(cond, W['w_mod'][l], W['b_mod'][l])
    sh1, sc1, g1, sh2, sc2, g2 = jnp.split(mod, 6, axis=-1)
    x2 = x.reshape(n, d)
    ua, ub, uc, ud = _in_projection(x2, W['norm_mix'][l], 1.0 + sc1, sh1, W['w_in'][l], t)
    mix, new_ctx = _token_mixers(ua.reshape(b, t, -1), ub.reshape(b, t, -1), uc.reshape(b, t, -1),
                                 ud.reshape(b, t, -1), rope, ctx, l, W)
    w_router = jnp.concatenate([W['moe_w_group'][l], W['moe_w_router'][l]], axis=1)
    w_router = jnp.pad(w_router, ((0, 0), (0, ROUTER_W - w_router.shape[1])))
    b_router = jnp.concatenate([W['moe_b_group'][l], W['moe_b_router'][l]])
    b_router = jnp.pad(b_router, (0, ROUTER_W - b_router.shape[0])).reshape(1, ROUTER_W)
    x2, h2, logits = _out_projection(mix.reshape(n, d).astype(BF16), W['w_out'][l], x2, g1,
                                     W['norm_ffn'][l], 1.0 + sc2, sh2, w_router, b_router, t)
    y = _hier_moe(h2, logits, W['moe_w1'][l], W['moe_w3'][l], W['moe_w2'][l])
    g2r = jnp.broadcast_to(g2[:, None, :], (g2.shape[0], n // g2.shape[0], d)).reshape(n, d)
    x2 = x2 + g2r * y
    return x2.reshape(b, t, d), new_ctx


def kernel(x_prompt, x_sample, c, state_rwkv, cache_diff_k, cache_diff_v, cache_na_k, cache_na_v,
           cache_gqa_k, cache_gqa_v, c_ctx, norm_mix, norm_ffn, w_mod, b_mod, w_in, w_out,
           rw_shift, rw_w0, rw_w_up, rw_a0, rw_a_up, rw_g_up, rw_k_k, rw_k_a, rw_r_k, rw_gn_w, rw_gn_b,
           diff_qk_norm, diff_lambda, diff_subln, na_qk_norm, na_rel_bias, gqa_qk_norm,
           moe_w_group, moe_b_group, moe_w_router, moe_b_router, moe_w1, moe_w3, moe_w2):
    W = {
        'norm_mix': norm_mix, 'norm_ffn': norm_ffn, 'w_mod': w_mod, 'b_mod': b_mod,
        'w_in': w_in, 'w_out': w_out, 'rw_shift': rw_shift, 'rw_w0': rw_w0, 'rw_w_up': rw_w_up,
        'rw_a0': rw_a0, 'rw_a_up': rw_a_up, 'rw_g_up': rw_g_up, 'rw_k_k': rw_k_k, 'rw_k_a': rw_k_a,
        'rw_r_k': rw_r_k, 'rw_gn_w': rw_gn_w, 'rw_gn_b': rw_gn_b, 'diff_qk_norm': diff_qk_norm,
        'diff_lambda': diff_lambda, 'diff_subln': diff_subln, 'na_qk_norm': na_qk_norm,
        'na_rel_bias': na_rel_bias, 'gqa_qk_norm': gqa_qk_norm, 'moe_w_group': moe_w_group,
        'moe_b_group': moe_b_group, 'moe_w_router': moe_w_router, 'moe_b_router': moe_b_router,
        'moe_w1': moe_w1, 'moe_w3': moe_w3, 'moe_w2': moe_w2,
    }
    depth = w_in.shape[0]

    xp = x_prompt
    cond_ctx = c_ctx[None, :]
    ctx_layers = []
    for l in range(depth):
        xp, new_ctx = _trunk_layer(xp, cond_ctx, None, None, l, W)
        ctx_layers.append(new_ctx)
    new_caches = tuple(jnp.stack([z[i] for z in ctx_layers], axis=1) for i in range(7))

    t_lat = x_sample.shape[1]
    rope = (_axial_rope(t_lat, DIFF_DQK), _axial_rope(t_lat, HEAD_DIM))
    xs = x_sample
    for l in range(depth):
        ctx = (state_rwkv[:, l], cache_diff_k[:, l], cache_diff_v[:, l], cache_na_k[:, l],
               cache_na_v[:, l], cache_gqa_k[:, l], cache_gqa_v[:, l])
        xs, _ = _trunk_layer(xs, c, rope, ctx, l, W)
    return (xp, xs) + new_caches
```

```python
import functools
import math

import numpy as np
import jax
import jax.numpy as jnp
from jax import lax
from jax.experimental import pallas as pl
from jax.experimental.pallas import tpu as pltpu

F32 = jnp.float32
BF16 = jnp.bfloat16

D_MODEL = 1024
GRID_W = 64
HEAD_DIM = 64
GROUP_W = 256
GROUP_HEADS = 4
LORA_W = 64
LORA_A = 64
LORA_G = 128
RWKV_GN_EPS = 64e-5
DIFF_DQK = 32
GQA_KV_HEADS = 2
GQA_GROUP = 2
NA_ROWS = 8
NA_COLS = 16
ROPE_THETA = 10000.0
NORM_EPS = 1e-6
N_EXPERT_GROUPS = 4
EXPERTS_PER_GROUP = 8
N_EXPERTS = 32
TOP_K = 2
D_EXPERT = 512
A_IN = 3 * GROUP_W + 2 * LORA_W + 2 * LORA_A + LORA_G
B_IN = 3 * GROUP_W
C_IN = 3 * GROUP_W
D_IN = GROUP_W + 2 * GQA_KV_HEADS * HEAD_DIM
IN_W = A_IN + B_IN + C_IN + D_IN

LANES = 128
SUBLANES = 8
VMEM_LIMIT = 48 * 1024 * 1024

ROW_TILE = 256
Q_TILE = 256
SCAN_STEPS = 16
KV_CHUNK = 512
MOE_TILE = 256
ROUTER_W = 128
MASK_VALUE = -1e30
LOG2E = 1.4426950408889634


def _cparams(sem):
    return pltpu.CompilerParams(dimension_semantics=sem, vmem_limit_bytes=VMEM_LIMIT)


def _mod_kernel(c_ref, w_ref, b_ref, o_ref):
    c = c_ref[...]
    a = c * jax.nn.sigmoid(c)
    o_ref[...] = jnp.dot(a, w_ref[...], preferred_element_type=F32,
                         precision=lax.Precision.HIGHEST) + b_ref[...]


def _modulation(cond, w, b):
    m, d = cond.shape
    n = w.shape[1]
    mp = -(-m // SUBLANES) * SUBLANES
    cp = jnp.pad(cond, ((0, mp - m), (0, 0)))
    tn = 768
    out = pl.pallas_call(
        _mod_kernel,
        out_shape=jax.ShapeDtypeStruct((mp, n), F32),
        grid=(n // tn,),
        in_specs=[pl.BlockSpec((mp, d), lambda j: (0, 0)),
                  pl.BlockSpec((d, tn), lambda j: (0, j)),
                  pl.BlockSpec((1, tn), lambda j: (0, j))],
        out_specs=pl.BlockSpec((mp, tn), lambda j: (0, j)),
        compiler_params=_cparams(("arbitrary",)),
        name="modulation",
    )(cp, w, b.reshape(1, n))
    return out[:m]


def _inproj_kernel(x_ref, g_ref, sc_ref, sh_ref, w_ref, oa_ref, ob_ref, oc_ref, od_ref):
    x = x_ref[...]
    ms = jnp.mean(x * x, axis=-1, keepdims=True)
    h = x * lax.rsqrt(ms + NORM_EPS) * g_ref[...]
    h = h * sc_ref[0] + sh_ref[0]
    u = jnp.dot(h.astype(BF16), w_ref[...].astype(BF16), preferred_element_type=F32)
    oa_ref[...] = u[:, :A_IN]
    ob_ref[...] = u[:, A_IN:A_IN + B_IN]
    oc_ref[...] = u[:, A_IN + B_IN:A_IN + B_IN + C_IN]
    od_ref[...] = u[:, A_IN + B_IN + C_IN:]


def _in_projection(x2, gain, scale1p, shift, w_in, t_len):
    n, d = x2.shape
    bm = scale1p.shape[0]
    tm = ROW_TILE
    per_seq = t_len // tm

    def mod_idx(i):
        return ((i // per_seq) if bm > 1 else 0, 0, 0)

    widths = (A_IN, B_IN, C_IN, D_IN)
    return pl.pallas_call(
        _inproj_kernel,
        out_shape=tuple(jax.ShapeDtypeStruct((n, w), F32) for w in widths),
        grid=(n // tm,),
        in_specs=[pl.BlockSpec((tm, d), lambda i: (i, 0)),
                  pl.BlockSpec((1, d), lambda i: (0, 0)),
                  pl.BlockSpec((1, 1, d), mod_idx),
                  pl.BlockSpec((1, 1, d), mod_idx),
                  pl.BlockSpec((d, IN_W), lambda i: (0, 0))],
        out_specs=tuple(pl.BlockSpec((tm, w), lambda i: (i, 0)) for w in widths),
        compiler_params=_cparams(("arbitrary",)),
        name="in_projection",
    )(x2, gain.reshape(1, d), scale1p.reshape(bm, 1, d), shift.reshape(bm, 1, d), w_in)


def _outproj_kernel(mix_ref, w_ref, x_ref, g1_ref, gn_ref, sc_ref, sh_ref, wr_ref, br_ref,
                    xo_ref, h_ref, lg_ref):
    y = jnp.dot(mix_ref[...], w_ref[...].astype(BF16), preferred_element_type=F32)
    xn = x_ref[...] + g1_ref[0] * y
    xo_ref[...] = xn
    ms = jnp.mean(xn * xn, axis=-1, keepdims=True)
    h = xn * lax.rsqrt(ms + NORM_EPS) * gn_ref[...]
    h = h * sc_ref[0] + sh_ref[0]
    h_ref[...] = h.astype(BF16)
    lg_ref[...] = jnp.dot(h, wr_ref[...], preferred_element_type=F32,
                          precision=lax.Precision.HIGHEST) + br_ref[...]


def _out_projection(mix, w_out, x2, gate1, gain, scale1p, shift, w_router, b_router, t_len):
    n, d = x2.shape
    bm = gate1.shape[0]
    tm = ROW_TILE
    per_seq = t_len // tm

    def mod_idx(i):
        return ((i // per_seq) if bm > 1 else 0, 0, 0)

    mod_spec = pl.BlockSpec((1, 1, d), mod_idx)
    return pl.pallas_call(
        _outproj_kernel,
        out_shape=(jax.ShapeDtypeStruct((n, d), F32),
                   jax.ShapeDtypeStruct((n, d), BF16),
                   jax.ShapeDtypeStruct((n, ROUTER_W), F32)),
        grid=(n // tm,),
        in_specs=[pl.BlockSpec((tm, d), lambda i: (i, 0)),
                  pl.BlockSpec((d, d), lambda i: (0, 0)),
                  pl.BlockSpec((tm, d), lambda i: (i, 0)),
                  mod_spec,
                  pl.BlockSpec((1, d), lambda i: (0, 0)),
                  mod_spec, mod_spec,
                  pl.BlockSpec((d, ROUTER_W), lambda i: (0, 0)),
                  pl.BlockSpec((1, ROUTER_W), lambda i: (0, 0))],
        out_specs=(pl.BlockSpec((tm, d), lambda i: (i, 0)),
                   pl.BlockSpec((tm, d), lambda i: (i, 0)),
                   pl.BlockSpec((tm, ROUTER_W), lambda i: (i, 0))),
        compiler_params=_cparams(("arbitrary",)),
        name="out_projection",
    )(mix, w_out, x2, gate1.reshape(bm, 1, d), gain.reshape(1, d), scale1p.reshape(bm, 1, d),
      shift.reshape(bm, 1, d), w_router, b_router)


ACC_ROWS = HEAD_DIM + 16


def _flash_kernel(sc_ref, q_ref, k_ref, vt_ref, g_ref, o_ref, m_ref, acc_ref, sa_ref, sb_ref, *, heads, diff, chunk):
    tq = q_ref.shape[1]
    kdim = k_ref.shape[2]
    n_chunks = k_ref.shape[1] // chunk
    qt = q_ref[0].astype(F32).T.astype(BF16)

    def stationary(q_rows, k_row0):
        pieces = []
        if k_row0 > 0:
            pieces.append(jnp.zeros((k_row0, tq), BF16))
        pieces.append(q_rows)
        rest = kdim - k_row0 - q_rows.shape[0]
        if rest > 0:
            pieces.append(jnp.zeros((rest, tq), BF16))
        return jnp.concatenate(pieces, axis=0) if len(pieces) > 1 else pieces[0]

    streams = []
    for (maps, v_row0) in heads:
        for (qs, qn, ks) in maps:
            streams.append((stationary(qt[qs:qs + qn], ks), v_row0))
    m_ref[...] = jnp.full(m_ref.shape, MASK_VALUE, F32)
    acc_ref[...] = jnp.zeros(acc_ref.shape, F32)
    ones_rows = (lax.broadcasted_iota(jnp.int32, (ACC_ROWS - HEAD_DIM, chunk), 0) == 0).astype(BF16)

    def scores(c, s_ref):
        off = pl.multiple_of(c * chunk, chunk)
        kc = k_ref[0, pl.ds(off, chunk), :]
        for j, (w, _) in enumerate(streams):
            s_ref[j] = jnp.dot(kc, w, preferred_element_type=F32)

    def accumulate(c, s_ref):
        off = pl.multiple_of(c * chunk, chunk)
        for j, (_, v_row0) in enumerate(streams):
            s = s_ref[j]
            m = m_ref[j]
            m_new = jnp.maximum(m, jnp.max(s, axis=0, keepdims=True))
            p = jnp.exp2(s - m_new).astype(BF16)
            alpha = jnp.exp2(m - m_new)
            m_ref[j] = m_new
            vc = jnp.concatenate([vt_ref[0, pl.ds(v_row0, HEAD_DIM), pl.ds(off, chunk)], ones_rows], axis=0)
            acc_ref[j] = alpha * acc_ref[j] + jnp.dot(vc, p, preferred_element_type=F32)

    scores(0, sa_ref)

    def body(jj, carry):
        c = 2 * jj
        scores(c + 1, sb_ref)
        accumulate(c, sa_ref)
        scores(c + 2, sa_ref)
        accumulate(c + 1, sb_ref)
        return carry

    lax.fori_loop(0, (n_chunks - 1) // 2, body, 0)
    if n_chunks % 2 == 0:
        scores(n_chunks - 1, sb_ref)
        accumulate(n_chunks - 2, sa_ref)
        accumulate(n_chunks - 1, sb_ref)
    else:
        accumulate(n_chunks - 1, sa_ref)

    outs = []
    j = 0
    for (maps, v_row0) in heads:
        a1 = acc_ref[j]
        o = a1[:HEAD_DIM] / a1[HEAD_DIM:HEAD_DIM + 1]
        if diff:
            a2 = acc_ref[j + 1]
            o = o - (sc_ref[0] / a2[HEAD_DIM:HEAD_DIM + 1]) * a2[:HEAD_DIM]
            ms = jnp.mean(o * o, axis=0, keepdims=True)
            o = o * lax.rsqrt(ms + NORM_EPS) * g_ref[...] * sc_ref[1]
        j += len(maps)
        outs.append(o)
    o_ref[0] = jnp.concatenate(outs, axis=0).T


def _flash_attention(q, k, vt, heads, scalars=None, gain=None):
    b, s, c = q.shape
    l, ck = k.shape[1:]
    cv = vt.shape[1]
    tq = min(Q_TILE, s)
    chunk = min(KV_CHUNK, l)
    assert l % chunk == 0 and s % tq == 0
    diff = scalars is not None
    ns = sum(len(maps) for maps, _ in heads)
    if not diff:
        scalars = jnp.zeros((2,), F32)
        gain = jnp.ones((HEAD_DIM,), F32)
    return pl.pallas_call(
        functools.partial(_flash_kernel, heads=heads, diff=diff, chunk=chunk),
        out_shape=jax.ShapeDtypeStruct((b, s, c), F32),
        grid_spec=pltpu.PrefetchScalarGridSpec(
            num_scalar_prefetch=1,
            grid=(b, s // tq),
            in_specs=[pl.BlockSpec((1, tq, c), lambda bi, i, sc: (bi, i, 0)),
                      pl.BlockSpec((1, l, ck), lambda bi, i, sc: (bi, 0, 0)),
                      pl.BlockSpec((1, cv, l), lambda bi, i, sc: (bi, 0, 0)),
                      pl.BlockSpec((HEAD_DIM, 1), lambda bi, i, sc: (0, 0))],
            out_specs=pl.BlockSpec((1, tq, c), lambda bi, i, sc: (bi, i, 0)),
            scratch_shapes=[pltpu.VMEM((ns, 1, tq), F32), pltpu.VMEM((ns, ACC_ROWS, tq), F32),
                            pltpu.VMEM((ns, chunk, tq), F32), pltpu.VMEM((ns, chunk, tq), F32)]),
        compiler_params=_cparams(("arbitrary", "arbitrary")),
        name="flash_attention",
    )(scalars, q, k, vt, gain.reshape(HEAD_DIM, 1))


DIFF_HEADS = tuple((((h * HEAD_DIM, HEAD_DIM // 2, h * HEAD_DIM),
                     (h * HEAD_DIM + HEAD_DIM // 2, HEAD_DIM // 2, h * HEAD_DIM + HEAD_DIM // 2)), h * HEAD_DIM)
                   for h in range(GROUP_HEADS))
DENSE_HEADS = tuple((((h * HEAD_DIM, HEAD_DIM, h * HEAD_DIM),), h * HEAD_DIM) for h in range(GROUP_HEADS))
GQA_HEADS = tuple((((h * HEAD_DIM, HEAD_DIM, (h // 2) * HEAD_DIM),), (h // 2) * HEAD_DIM) for h in range(GROUP_HEADS))


def _na_kernel(q_ref, k_ref, v_ref, kct_ref, vc_ref, bias_ref, o_ref, *, rows):
    r = pl.program_id(1)
    rs = jnp.clip(r - NA_ROWS // 2, 0, rows - NA_ROWS)
    start = pl.multiple_of(rs * GRID_W, GRID_W)
    win = NA_ROWS * GRID_W
    for h in range(GROUP_HEADS):
        q = q_ref[0, h]
        kw = k_ref[0, h, pl.ds(start, win), :]
        vw = v_ref[0, h, pl.ds(start, win), :]
        s_nb = lax.dot_general(q, kw, (((1,), (1,)), ((), ())), preferred_element_type=F32)
        s_nb = s_nb + bias_ref[0, h]
        s_cx = jnp.dot(q, kct_ref[0, h], preferred_element_type=F32)
        m = jnp.maximum(jnp.max(s_nb, axis=-1, keepdims=True), jnp.max(s_cx, axis=-1, keepdims=True))
        p_nb = jnp.exp(s_nb - m)
        p_cx = jnp.exp(s_cx - m)
        l = jnp.sum(p_nb, axis=-1, keepdims=True) + jnp.sum(p_cx, axis=-1, keepdims=True)
        o = (jnp.dot(p_nb.astype(BF16), vw, preferred_element_type=F32)
             + jnp.dot(p_cx.astype(BF16), vc_ref[0, h], preferred_element_type=F32))
        o_ref[0, h] = o / l


def _na_bias_table(rel_bias):
    w = np.arange(GRID_W)
    col_start = np.clip(w - NA_COLS // 2, 0, GRID_W - NA_COLS)
    c = np.arange(GRID_W)
    valid = (c[None, :] >= col_start[:, None]) & (c[None, :] < col_start[:, None] + NA_COLS)
    cidx = np.clip(c[None, :] - w[:, None] + NA_COLS - 1, 0, 2 * NA_COLS - 2)
    delta = np.arange(NA_ROWS)
    i = np.arange(NA_ROWS)
    ridx = i[None, :] - delta[:, None] + NA_ROWS - 1
    t = rel_bias[:, ridx]
    t = t[:, :, :, cidx]
    t = jnp.where(valid[None, None, None], t, MASK_VALUE)
    t = jnp.transpose(t, (1, 0, 3, 2, 4))
    return t.reshape(NA_ROWS, rel_bias.shape[0], GRID_W, NA_ROWS * GRID_W).astype(F32)


def _neighborhood_attention(q, k, v, kct, vc, bias_tab):
    b, h, s, d = q.shape
    l = kct.shape[-1]
    rows = s // GRID_W
    assert rows >= NA_ROWS
    half = NA_ROWS // 2

    def bias_idx(bi, r):
        return (r - jnp.clip(r - half, 0, rows - NA_ROWS), 0, 0, 0)

    return pl.pallas_call(
        functools.partial(_na_kernel, rows=rows),
        out_shape=jax.ShapeDtypeStruct((b, h, s, d), F32),
        grid=(b, rows),
        in_specs=[pl.BlockSpec((1, h, GRID_W, d), lambda bi, r: (bi, 0, r, 0)),
                  pl.BlockSpec((1, h, s, d), lambda bi, r: (bi, 0, 0, 0)),
                  pl.BlockSpec((1, h, s, d), lambda bi, r: (bi, 0, 0, 0)),
                  pl.BlockSpec((1, h, d, l), lambda bi, r: (bi, 0, 0, 0)),
                  pl.BlockSpec((1, h, l, d), lambda bi, r: (bi, 0, 0, 0)),
                  pl.BlockSpec((1, h, GRID_W, NA_ROWS * GRID_W), bias_idx)],
        out_specs=pl.BlockSpec((1, h, GRID_W, d), lambda bi, r: (bi, 0, r, 0)),
        compiler_params=_cparams(("arbitrary", "arbitrary")),
        name="neighborhood_attention",
    )(q, k, v, kct, vc, bias_tab)


SCAN_BATCH = SUBLANES
N_KQ = 5
FWD_W = (N_KQ + 1) * GROUP_W
BWD_W = (N_KQ - 1) * GROUP_W
V_SPLIT = 4
V_TILES = HEAD_DIM // (V_SPLIT * SUBLANES)
KT_BUFFERS = 4


def _scan_kernel(zf_ref, zb_ref, zrv_ref, s0_ref, yf_ref, yb_ref, sfin_ref, st_ref, sa_ref, *kt_refs, steps):
    i = pl.program_id(1)

    @pl.when(i == 0)
    def _():
        st_ref[...] = s0_ref[0]

    lane = lax.broadcasted_iota(jnp.int32, (SUBLANES, LANES), 1)
    lane_vq = lane // (LANES // V_SPLIT)
    chan_vq = (lane // (V_TILES * SUBLANES)) % V_SPLIT

    def operand_tile(q, s):
        sb = steps - 1 - s
        pieces = []
        for hp in range(2):
            pieces.append(zf_ref[s, :, pl.ds(q * GROUP_W + hp * LANES, LANES)])
        for hp in range(2):
            if q < N_KQ - 1:
                pieces.append(zb_ref[sb, :, pl.ds(q * GROUP_W + hp * LANES, LANES)])
            else:
                pieces.append(zrv_ref[sb, :, pl.ds((q - (N_KQ - 1)) * GROUP_W + hp * LANES, LANES)])
        tile = jnp.concatenate(pieces * V_SPLIT, axis=0)
        return tile.T

    def prepare(kt_ref, s):
        for q in range(N_KQ + 1):
            kt_ref[q] = operand_tile(q, s)

    nacc = 4

    def tree(parts):
        return (parts[0] + parts[1]) + (parts[2] + parts[3])

    prepare(kt_refs[0], 0)
    prepare(kt_refs[1], 1)
    for hpar in range(2):
        for vb in range(V_TILES):
            parts = [None] * nacc
            for k in range(HEAD_DIM):
                term = st_ref[hpar, k, vb] * kt_refs[0][0, pl.ds(hpar * HEAD_DIM + k, 1), :]
                parts[k % nacc] = term if parts[k % nacc] is None else parts[k % nacc] + term
            sa_ref[hpar, vb] = -tree(parts)

    def step(s, kt_ref, nxt_ref, far_ref):
        prepare(far_ref, jnp.minimum(s + 2, steps - 1))
        vt = kt_ref[N_KQ]
        ys = []
        for hpar in range(2):
            base = hpar * HEAD_DIM
            vops = []
            for vb in range(V_TILES):
                acc = None
                for vq in range(V_SPLIT):
                    r0 = base + vq * V_TILES * SUBLANES + vb * SUBLANES
                    piece = vt[r0:r0 + SUBLANES, :]
                    acc = piece if acc is None else jnp.where(lane_vq == vq, piece, acc)
                vops.append(acc)
            sas = [sa_ref[hpar, vb] for vb in range(V_TILES)]
            yp = [[None] * nacc for _ in range(V_TILES)]
            sp = [[None] * nacc for _ in range(V_TILES)]
            for k in range(HEAD_DIM):
                row = pl.ds(base + k, 1)
                w = kt_ref[1, row, :]
                ka = kt_ref[2, row, :]
                kd = kt_ref[3, row, :]
                rr = kt_ref[4, row, :]
                kn = nxt_ref[0, row, :]
                j = k % nacc
                for vb in range(V_TILES):
                    sn = st_ref[hpar, k, vb] * w + sas[vb] * ka + vops[vb] * kd
                    st_ref[hpar, k, vb] = sn
                    ty = sn * rr
                    yp[vb][j] = ty if yp[vb][j] is None else yp[vb][j] + ty
                    tn = sn * kn
                    sp[vb][j] = tn if sp[vb][j] is None else sp[vb][j] + tn
            for vb in range(V_TILES):
                sa_ref[hpar, vb] = -tree(sp[vb])
            ys.append([tree(yp[vb]) for vb in range(V_TILES)])
        ytile = jnp.concatenate([ys[hpar][vb] for hpar in range(2) for _ in range(V_SPLIT) for vb in range(V_TILES)],
                                axis=0)
        yt = ytile.T
        sb = steps - 1 - s
        for d in range(2):
            for hp in range(2):
                acc = None
                for vq in range(V_SPLIT):
                    r0 = vq * (LANES // V_SPLIT) + d * 2 * SUBLANES + hp * SUBLANES
                    piece = yt[r0:r0 + SUBLANES, :]
                    acc = piece if acc is None else jnp.where(chan_vq == vq, piece, acc)
                if d == 0:
                    yf_ref[s, :, pl.ds(hp * LANES, LANES)] = acc
                else:
                    yb_ref[sb, :, pl.ds(hp * LANES, LANES)] = acc

    nbuf = len(kt_refs)

    def body(j, carry):
        for u in range(nbuf):
            step(nbuf * j + u, kt_refs[u], kt_refs[(u + 1) % nbuf], kt_refs[(u + 2) % nbuf])
        return carry

    lax.fori_loop(0, steps // nbuf, body, 0)

    @pl.when(i == pl.num_programs(1) - 1)
    def _():
        sfin_ref[0] = st_ref[...]


def _rwkv_scan(zf, zb, s0):
    t_len, b, _ = zf.shape
    groups = b // SCAN_BATCH
    ts = SCAN_STEPS
    nblk = t_len // ts
    st_shape = (2, HEAD_DIM, V_TILES, SUBLANES, LANES)
    return pl.pallas_call(
        functools.partial(_scan_kernel, steps=ts),
        out_shape=(jax.ShapeDtypeStruct((t_len, b, GROUP_W), F32),
                   jax.ShapeDtypeStruct((t_len, b, GROUP_W), F32),
                   jax.ShapeDtypeStruct((groups,) + st_shape, F32)),
        grid=(groups, nblk),
        in_specs=[pl.BlockSpec((ts, SCAN_BATCH, FWD_W), lambda g, i: (i, g, 0)),
                  pl.BlockSpec((ts, SCAN_BATCH, BWD_W), lambda g, i: (nblk - 1 - i, g, 0)),
                  pl.BlockSpec((ts, SCAN_BATCH, 2 * GROUP_W), lambda g, i: (nblk - 1 - i, g, (N_KQ - 1) // 2)),
                  pl.BlockSpec((1,) + st_shape, lambda g, i: (g, 0, 0, 0, 0, 0))],
        out_specs=(pl.BlockSpec((ts, SCAN_BATCH, GROUP_W), lambda g, i: (i, g, 0)),
                   pl.BlockSpec((ts, SCAN_BATCH, GROUP_W), lambda g, i: (nblk - 1 - i, g, 0)),
                   pl.BlockSpec((1,) + st_shape, lambda g, i: (g, 0, 0, 0, 0, 0))),
        scratch_shapes=[pltpu.VMEM(st_shape, F32), pltpu.VMEM((2, V_TILES, SUBLANES, LANES), F32)]
        + [pltpu.VMEM((N_KQ + 1, LANES, LANES), F32)] * KT_BUFFERS,
        compiler_params=_cparams(("arbitrary", "arbitrary")),
        name="rwkv_scan",
    )(zf, zb, zf, s0)


def _rwkv_time_mix(ua, s0, mu, w0, w_up, a0, a_up, g_up, k_k, k_a, r_k, gn_w, gn_b):
    b, t, _ = ua.shape
    h, n = GROUP_HEADS, HEAD_DIM
    u = jnp.swapaxes(ua, 0, 1)
    prev = jnp.pad(u[:-1], ((1, 0), (0, 0), (0, 0)))
    nxt = jnp.pad(u[1:], ((0, 1), (0, 0), (0, 0)))
    u = u + mu[0] * (prev - u) + mu[1] * (nxt - u)
    cuts = np.cumsum([GROUP_W, GROUP_W, GROUP_W, 2 * LORA_W, 2 * LORA_A]).tolist()
    r, k, v, wd, ad, gd = jnp.split(u, cuts, axis=-1)
    wd = wd.reshape(t, b, 2, LORA_W)
    ad = ad.reshape(t, b, 2, LORA_A)
    w_raw = w0 + jnp.einsum('tbdr,drc->tbdc', jnp.tanh(wd), w_up)
    decay = jnp.exp(-jnp.exp(-jax.nn.softplus(-w_raw) - 0.5))
    a = jax.nn.sigmoid(a0 + jnp.einsum('tbdr,drc->tbdc', ad, a_up))
    g = jax.nn.sigmoid(gd) @ g_up
    kk = k[:, :, None, :] * k_k
    kd = k[:, :, None, :] * (1.0 + (a - 1.0) * k_a)
    kkh = kk.reshape(t, b, 2, h, n)
    kkh = kkh * lax.rsqrt(jnp.sum(kkh * kkh, axis=-1, keepdims=True) + 1e-12)
    kk = kkh.reshape(t, b, 2, GROUP_W)
    ka = kk * a

    zf = jnp.concatenate([kk[:, :, 0], decay[:, :, 0], ka[:, :, 0], kd[:, :, 0], r, v], axis=-1)
    zb = jnp.concatenate([kk[:, :, 1], decay[:, :, 1], ka[:, :, 1], kd[:, :, 1]], axis=-1)
    bp = -(-b // SCAN_BATCH) * SCAN_BATCH
    groups = bp // SCAN_BATCH
    if s0 is None:
        s0l = jnp.zeros((groups, 2, n, V_TILES, SUBLANES, LANES), F32)
    else:
        s0 = jnp.pad(s0.astype(F32), ((0, bp - b),) + ((0, 0),) * 4)
        s0l = s0.reshape(groups, SCAN_BATCH, 2, 2, 2, V_SPLIT, V_TILES, SUBLANES, n)
        s0l = jnp.transpose(s0l, (0, 4, 8, 6, 7, 5, 2, 3, 1)).reshape(groups, 2, n, V_TILES, SUBLANES, LANES)
    if bp != b:
        zf = jnp.pad(zf, ((0, 0), (0, bp - b), (0, 0)))
        zb = jnp.pad(zb, ((0, 0), (0, bp - b), (0, 0)))

    yf, yb, s_fin = _rwkv_scan(zf, zb, s0l)

    o = (yf[:, :b] + yb[:, :b]).reshape(t, b, h, n)
    s_fin = s_fin.reshape(groups, 2, n, V_TILES, SUBLANES, V_SPLIT, 2, 2, SCAN_BATCH)
    s_fin = jnp.transpose(s_fin, (0, 8, 6, 7, 1, 5, 3, 4, 2)).reshape(bp, 2, h, n, n)[:b]

    m = jnp.mean(o, axis=-1, keepdims=True)
    var = jnp.mean(jnp.square(o - m), axis=-1, keepdims=True)
    o = ((o - m) * lax.rsqrt(var + RWKV_GN_EPS)).reshape(t, b, GROUP_W) * gn_w + gn_b
    rh = r.reshape(t, b, h, n)
    vh = v.reshape(t, b, h, n)
    kdh = kd.reshape(t, b, 2, h, n)
    bonus = jnp.sum(jnp.sum(rh[:, :, None] * kdh * r_k, axis=-1, keepdims=True) * vh[:, :, None], axis=2)
    out = (o + bonus.reshape(t, b, GROUP_W)) * g
    return jnp.swapaxes(out, 0, 1), s_fin


def _expert_kernel(be_ref, nu_ref, x_ref, w1_ref, w3_ref, w2_ref, o_ref, w1s, w3s, w2s):
    i = pl.program_id(0)
    e = be_ref[i]
    prev = be_ref[jnp.maximum(i - 1, 0)]

    @pl.when((i == 0) | (e != prev))
    def _():
        w1s[...] = w1_ref[0].astype(BF16)
        w3s[...] = w3_ref[0].astype(BF16)
        w2s[...] = w2_ref[0].astype(BF16)

    @pl.when(i < nu_ref[0])
    def _():
        x = x_ref[...]
        a = jnp.dot(x, w1s[...], preferred_element_type=F32)
        g = jnp.dot(x, w3s[...], preferred_element_type=F32)
        hmid = (a * jax.nn.sigmoid(a)) * g
        o_ref[...] = jnp.dot(hmid.astype(BF16), w2s[...], preferred_element_type=F32)

    @pl.when(i >= nu_ref[0])
    def _():
        o_ref[...] = jnp.zeros_like(o_ref)


def _expert_mlp(xb, block_e, n_used, w1, w3, w2):
    cap, d = xb.shape
    bm = MOE_TILE
    de = w1.shape[-1]
    return pl.pallas_call(
        _expert_kernel,
        out_shape=jax.ShapeDtypeStruct((cap, d), F32),
        grid_spec=pltpu.PrefetchScalarGridSpec(
            num_scalar_prefetch=2,
            grid=(cap // bm,),
            in_specs=[pl.BlockSpec((bm, d), lambda i, be, nu: (i, 0)),
                      pl.BlockSpec((1, d, de), lambda i, be, nu: (be[i], 0, 0)),
                      pl.BlockSpec((1, d, de), lambda i, be, nu: (be[i], 0, 0)),
                      pl.BlockSpec((1, de, d), lambda i, be, nu: (be[i], 0, 0))],
            out_specs=pl.BlockSpec((bm, d), lambda i, be, nu: (i, 0)),
            scratch_shapes=[pltpu.VMEM((d, de), BF16), pltpu.VMEM((d, de), BF16), pltpu.VMEM((de, d), BF16)]),
        compiler_params=_cparams(("arbitrary",)),
        name="expert_mlp",
    )(block_e, n_used, xb, w1, w3, w2)


def _hier_moe(h_bf, logits, w1, w3, w2):
    n, d = h_bf.shape
    bm = MOE_TILE
    g_logits = logits[:, :N_EXPERT_GROUPS]
    g_idx = jnp.argmax(g_logits, axis=-1)
    g_top = jnp.max(jax.nn.softmax(g_logits, axis=-1), axis=-1)
    e_logits = logits[:, N_EXPERT_GROUPS:N_EXPERT_GROUPS + N_EXPERTS].reshape(n, N_EXPERT_GROUPS, EXPERTS_PER_GROUP)
    e_logits = jnp.take_along_axis(e_logits, g_idx[:, None, None], axis=1)[:, 0]
    top_v, top_i = lax.top_k(e_logits, TOP_K)
    gate = jax.nn.softmax(top_v, axis=-1) * g_top[:, None]
    eid = (g_idx[:, None] * EXPERTS_PER_GROUP + top_i).reshape(-1).astype(jnp.int32)
    tok = jnp.repeat(jnp.arange(n, dtype=jnp.int32), TOP_K)
    n_assign = n * TOP_K
    onehot = (eid[:, None] == jnp.arange(N_EXPERTS, dtype=jnp.int32)[None, :]).astype(jnp.int32)
    csum = jnp.cumsum(onehot, axis=0)
    counts = csum[-1]
    rank = jnp.take_along_axis(csum, eid[:, None], axis=1)[:, 0] - 1
    padded = (counts + bm - 1) // bm * bm
    pad_end = jnp.cumsum(padded)
    pad_start = pad_end - padded
    dest = pad_start[eid] + rank
    n_blocks = -(-(n_assign + N_EXPERTS * (bm - 1)) // bm)
    cap = n_blocks * bm
    buf_tok = jnp.zeros((cap,), jnp.int32).at[dest].set(tok)
    block_e = jnp.minimum(jnp.searchsorted(pad_end, jnp.arange(n_blocks, dtype=jnp.int32) * bm, side='right'),
                          N_EXPERTS - 1).astype(jnp.int32)
    n_used = (pad_end[-1:] // bm).astype(jnp.int32)
    xb = h_bf[buf_tok]
    yb = _expert_mlp(xb, block_e, n_used, w1, w3, w2)
    d2 = dest.reshape(n, TOP_K)
    return yb[d2[:, 0]] * gate[:, 0:1] + yb[d2[:, 1]] * gate[:, 1:2]


def _rms(x, g):
    return x * lax.rsqrt(jnp.mean(x * x, axis=-1, keepdims=True) + NORM_EPS) * g


def _axial_rope(t_len, dim):
    q4 = dim // 4
    inv = ROPE_THETA ** (-jnp.arange(q4, dtype=F32) / q4)
    t = jnp.arange(t_len)
    row = (t // GRID_W).astype(F32)
    col = (t % GRID_W).astype(F32)
    ang = jnp.stack([row[:, None] * inv, col[:, None] * inv], axis=1)
    return jnp.cos(ang), jnp.sin(ang)


def _rope_tokens(x, cs):
    cos, sin = cs
    lead = x.ndim - 3
    shape = (1, cos.shape[0]) + (1,) * lead + cos.shape[1:]
    cos = cos.reshape(shape)
    sin = sin.reshape(shape)
    xf = x.reshape(*x.shape[:-1], 2, 2, x.shape[-1] // 4)
    x1 = xf[..., 0, :]
    x2 = xf[..., 1, :]
    out = jnp.stack([x1 * cos - x2 * sin, x2 * cos + x1 * sin], axis=-2)
    return out.reshape(x.shape)


def _token_mixers(ua, ub, uc, ud, rope, ctx, l, W):
    b, t, _ = ua.shape
    nh, n = GROUP_HEADS, HEAD_DIM
    latent = ctx is not None

    out_a, s_fin = _rwkv_time_mix(ua, ctx[0] if latent else None, W['rw_shift'][l], W['rw_w0'][l], W['rw_w_up'][l],
                                  W['rw_a0'][l], W['rw_a_up'][l], W['rw_g_up'][l], W['rw_k_k'][l], W['rw_k_a'][l],
                                  W['rw_r_k'][l], W['rw_gn_w'][l], W['rw_gn_b'][l])

    qb, kb, vb = jnp.split(ub, 3, axis=-1)
    qb = _rms(qb.reshape(b, t, nh, 2, DIFF_DQK), W['diff_qk_norm'][l, 0])
    kb = _rms(kb.reshape(b, t, nh, 2, DIFF_DQK), W['diff_qk_norm'][l, 1])
    if latent:
        qb = _rope_tokens(qb, rope[0])
        k_ctx = jnp.transpose(ctx[1], (0, 3, 1, 2, 4)).reshape(b, -1, GROUP_W)
        v_ctx = jnp.transpose(ctx[2], (0, 2, 1, 3)).reshape(b, -1, GROUP_W)
        kb_all = jnp.concatenate([k_ctx, _rope_tokens(kb, rope[0]).reshape(b, t, GROUP_W)], axis=1)
        vb_all = jnp.concatenate([v_ctx, vb], axis=1)
    else:
        kb_all, vb_all = kb.reshape(b, t, GROUP_W), vb
    lam_init = 0.8 - 0.6 * math.exp(-0.3 * l)
    lv = W['diff_lambda'][l].astype(F32)
    lam = jnp.exp(jnp.sum(lv[0] * lv[1])) - jnp.exp(jnp.sum(lv[2] * lv[3])) + lam_init
    scalars = jnp.stack([lam, jnp.asarray(1.0 - lam_init, F32)]).astype(F32)
    out_b = _flash_attention((qb * (LOG2E / math.sqrt(DIFF_DQK))).reshape(b, t, GROUP_W).astype(BF16),
                             kb_all.astype(BF16), jnp.swapaxes(vb_all, 1, 2).astype(BF16),
                             DIFF_HEADS, scalars, W['diff_subln'][l])

    qc, kc, vc = jnp.split(uc, 3, axis=-1)
    qc = _rms(qc.reshape(b, t, nh, n), W['na_qk_norm'][l, 0])
    kc = _rms(kc.reshape(b, t, nh, n), W['na_qk_norm'][l, 1])
    if latent:
        vch = vc.reshape(b, t, nh, n).transpose(0, 2, 1, 3)
        oc = _neighborhood_attention((qc * (1.0 / math.sqrt(n))).transpose(0, 2, 1, 3).astype(BF16),
                                     kc.transpose(0, 2, 1, 3).astype(BF16), vch.astype(BF16),
                                     jnp.swapaxes(ctx[3], -1, -2).astype(BF16), ctx[4].astype(BF16),
                                     _na_bias_table(W['na_rel_bias'][l]))
        out_c = oc.transpose(0, 2, 1, 3).reshape(b, t, GROUP_W)
    else:
        out_c = _flash_attention((qc * (LOG2E / math.sqrt(n))).reshape(b, t, GROUP_W).astype(BF16),
                                 kc.reshape(b, t, GROUP_W).astype(BF16), jnp.swapaxes(vc, 1, 2).astype(BF16),
                                 DENSE_HEADS)

    kvw = GQA_KV_HEADS * n
    qd, kd, vd = jnp.split(ud, [GROUP_W, GROUP_W + kvw], axis=-1)
    qd = _rms(qd.reshape(b, t, nh, n), W['gqa_qk_norm'][l, 0])
    kd = _rms(kd.reshape(b, t, GQA_KV_HEADS, n), W['gqa_qk_norm'][l, 1])
    if latent:
        qd = _rope_tokens(qd, rope[1])
        k_ctx = jnp.transpose(ctx[5], (0, 2, 1, 3)).reshape(b, -1, kvw)
        v_ctx = jnp.transpose(ctx[6], (0, 2, 1, 3)).reshape(b, -1, kvw)
        kd_all = jnp.concatenate([k_ctx, _rope_tokens(kd, rope[1]).reshape(b, t, kvw)], axis=1)
        vd_all = jnp.concatenate([v_ctx, vd], axis=1)
    else:
        kd_all, vd_all = kd.reshape(b, t, kvw), vd
    out_d = _flash_attention((qd * (LOG2E / math.sqrt(n))).reshape(b, t, GROUP_W).astype(BF16),
                             kd_all.astype(BF16), jnp.swapaxes(vd_all, 1, 2).astype(BF16), GQA_HEADS)

    mix = jnp.concatenate([out_a, out_b, out_c, out_d], axis=-1)
    if latent:
        return mix, None
    new_ctx = (s_fin, jnp.transpose(kb, (0, 2, 3, 1, 4)), vb.reshape(b, t, nh, n).transpose(0, 2, 1, 3),
               kc.transpose(0, 2, 1, 3), vc.reshape(b, t, nh, n).transpose(0, 2, 1, 3),
               kd.transpose(0, 2, 1, 3), vd.reshape(b, t, GQA_KV_HEADS, n).transpose(0, 2, 1, 3))
    return mix, new_ctx


def _trunk_layer(x, cond, rope, ctx, l, W):
    b, t, d = x.shape
    n = b * t
    mod = _modulation(cond, W['w_mod'][l], W['b_mod'][l])
    sh1, sc1, g1, sh2, sc2, g2 = jnp.split(mod, 6, axis=-1)
    x2 = x.reshape(n, d)
    ua, ub, uc, ud = _in_projection(x2, W['norm_mix'][l], 1.0 + sc1, sh1, W['w_in'][l], t)
    mix, new_ctx = _token_mixers(ua.reshape(b, t, -1), ub.reshape(b, t, -1), uc.reshape(b, t, -1),
                                 ud.reshape(b, t, -1), rope, ctx, l, W)
    w_router = jnp.concatenate([W['moe_w_group'][l], W['moe_w_router'][l]], axis=1)
    w_router = jnp.pad(w_router, ((0, 0), (0, ROUTER_W - w_router.shape[1])))
    b_router = jnp.concatenate([W['moe_b_group'][l], W['moe_b_router'][l]])
    b_router = jnp.pad(b_router, (0, ROUTER_W - b_router.shape[0])).reshape(1, ROUTER_W)
    x2, h2, logits = _out_projection(mix.reshape(n, d).astype(BF16), W['w_out'][l], x2, g1,
                                     W['norm_ffn'][l], 1.0 + sc2, sh2, w_router, b_router, t)
    y = _hier_moe(h2, logits, W['moe_w1'][l], W['moe_w3'][l], W['moe_w2'][l])
    g2r = jnp.broadcast_to(g2[:, None, :], (g2.shape[0], n // g2.shape[0], d)).reshape(n, d)
    x2 = x2 + g2r * y
    return x2.reshape(b, t, d), new_ctx


def kernel(x_prompt, x_sample, c, state_rwkv, cache_diff_k, cache_diff_v, cache_na_k, cache_na_v,
           cache_gqa_k, cache_gqa_v, c_ctx, norm_mix, norm_ffn, w_mod, b_mod, w_in, w_out,
           rw_shift, rw_w0, rw_w_up, rw_a0, rw_a_up, rw_g_up, rw_k_k, rw_k_a, rw_r_k, rw_gn_w, rw_gn_b,
           diff_qk_norm, diff_lambda, diff_subln, na_qk_norm, na_rel_bias, gqa_qk_norm,
           moe_w_group, moe_b_group, moe_w_router, moe_b_router, moe_w1, moe_w3, moe_w2):
    W = {
        'norm_mix': norm_mix, 'norm_ffn': norm_ffn, 'w_mod': w_mod, 'b_mod': b_mod,
        'w_in': w_in, 'w_out': w_out, 'rw_shift': rw_shift, 'rw_w0': rw_w0, 'rw_w_up': rw_w_up,
        'rw_a0': rw_a0, 'rw_a_up': rw_a_up, 'rw_g_up': rw_g_up, 'rw_k_k': rw_k_k, 'rw_k_a': rw_k_a,
        'rw_r_k': rw_r_k, 'rw_gn_w': rw_gn_w, 'rw_gn_b': rw_gn_b, 'diff_qk_norm': diff_qk_norm,
        'diff_lambda': diff_lambda, 'diff_subln': diff_subln, 'na_qk_norm': na_qk_norm,
        'na_rel_bias': na_rel_bias, 'gqa_qk_norm': gqa_qk_norm, 'moe_w_group': moe_w_group,
        'moe_b_group': moe_b_group, 'moe_w_router': moe_w_router, 'moe_b_router': moe_b_router,
        'moe_w1': moe_w1, 'moe_w3': moe_w3, 'moe_w2': moe_w2,
    }
    depth = w_in.shape[0]

    xp = x_prompt
    cond_ctx = c_ctx[None, :]
    ctx_layers = []
    for l in range(depth):
        xp, new_ctx = _trunk_layer(xp, cond_ctx, None, None, l, W)
        ctx_layers.append(new_ctx)
    new_caches = tuple(jnp.stack([z[i] for z in ctx_layers], axis=1) for i in range(7))

    t_lat = x_sample.shape[1]
    rope = (_axial_rope(t_lat, DIFF_DQK), _axial_rope(t_lat, HEAD_DIM))
    xs = x_sample
    for l in range(depth):
        ctx = (state_rwkv[:, l], cache_diff_k[:, l], cache_diff_v[:, l], cache_na_k[:, l],
               cache_na_v[:, l], cache_gqa_k[:, l], cache_gqa_v[:, l])
        xs, _ = _trunk_layer(xs, c, rope, ctx, l, W)
    return (xp, xs) + new_caches
```

```python
import functools
import math

import numpy as np
import jax
import jax.numpy as jnp
from jax import lax
from jax.experimental import pallas as pl
from jax.experimental.pallas import tpu as pltpu

F32 = jnp.float32
BF16 = jnp.bfloat16

D_MODEL = 1024
GRID_W = 64
HEAD_DIM = 64
GROUP_W = 256
GROUP_HEADS = 4
LORA_W = 64
LORA_A = 64
LORA_G = 128
RWKV_GN_EPS = 64e-5
DIFF_DQK = 32
GQA_KV_HEADS = 2
GQA_GROUP = 2
NA_ROWS = 8
NA_COLS = 16
ROPE_THETA = 10000.0
NORM_EPS = 1e-6
N_EXPERT_GROUPS = 4
EXPERTS_PER_GROUP = 8
N_EXPERTS = 32
TOP_K = 2
D_EXPERT = 512
A_IN = 3 * GROUP_W + 2 * LORA_W + 2 * LORA_A + LORA_G
B_IN = 3 * GROUP_W
C_IN = 3 * GROUP_W
D_IN = GROUP_W + 2 * GQA_KV_HEADS * HEAD_DIM
IN_W = A_IN + B_IN + C_IN + D_IN

LANES = 128
SUBLANES = 8
VMEM_LIMIT = 48 * 1024 * 1024

ROW_TILE = 256
Q_TILE = 256
SCAN_STEPS = 16
KV_CHUNK = 512
MOE_TILE = 256
ROUTER_W = 128
MASK_VALUE = -1e30
LOG2E = 1.4426950408889634


def _cparams(sem):
    return pltpu.CompilerParams(dimension_semantics=sem, vmem_limit_bytes=VMEM_LIMIT)


def _mod_kernel(c_ref, w_ref, b_ref, o_ref):
    c = c_ref[...]
    a = c * jax.nn.sigmoid(c)
    o_ref[...] = jnp.dot(a, w_ref[...], preferred_element_type=F32,
                         precision=lax.Precision.HIGHEST) + b_ref[...]


def _modulation(cond, w, b):
    m, d = cond.shape
    n = w.shape[1]
    mp = -(-m // SUBLANES) * SUBLANES
    cp = jnp.pad(cond, ((0, mp - m), (0, 0)))
    tn = 768
    out = pl.pallas_call(
        _mod_kernel,
        out_shape=jax.ShapeDtypeStruct((mp, n), F32),
        grid=(n // tn,),
        in_specs=[pl.BlockSpec((mp, d), lambda j: (0, 0)),
                  pl.BlockSpec((d, tn), lambda j: (0, j)),
                  pl.BlockSpec((1, tn), lambda j: (0, j))],
        out_specs=pl.BlockSpec((mp, tn), lambda j: (0, j)),
        compiler_params=_cparams(("arbitrary",)),
        name="modulation",
    )(cp, w, b.reshape(1, n))
    return out[:m]


def _inproj_kernel(x_ref, g_ref, sc_ref, sh_ref, w_ref, oa_ref, ob_ref, oc_ref, od_ref):
    x = x_ref[...]
    ms = jnp.mean(x * x, axis=-1, keepdims=True)
    h = x * lax.rsqrt(ms + NORM_EPS) * g_ref[...]
    h = h * sc_ref[0] + sh_ref[0]
    u = jnp.dot(h.astype(BF16), w_ref[...].astype(BF16), preferred_element_type=F32)
    oa_ref[...] = u[:, :A_IN]
    ob_ref[...] = u[:, A_IN:A_IN + B_IN]
    oc_ref[...] = u[:, A_IN + B_IN:A_IN + B_IN + C_IN]
    od_ref[...] = u[:, A_IN + B_IN + C_IN:]


def _in_projection(x2, gain, scale1p, shift, w_in, t_len):
    n, d = x2.shape
    bm = scale1p.shape[0]
    tm = ROW_TILE
    per_seq = t_len // tm
    b = n // t_len

    def mod_idx(i):
        return ((i // per_seq) if bm > 1 else 0, 0, 0)

    widths = (B_IN, C_IN, D_IN)
    ua, ub, uc, ud = pl.pallas_call(
        _inproj_kernel,
        out_shape=(jax.ShapeDtypeStruct((t_len, b * A_IN), F32),)
        + tuple(jax.ShapeDtypeStruct((n, w), F32) for w in widths),
        grid=(n // tm,),
        in_specs=[pl.BlockSpec((tm, d), lambda i: (i, 0)),
                  pl.BlockSpec((1, d), lambda i: (0, 0)),
                  pl.BlockSpec((1, 1, d), mod_idx),
                  pl.BlockSpec((1, 1, d), mod_idx),
                  pl.BlockSpec((d, IN_W), lambda i: (0, 0))],
        out_specs=(pl.BlockSpec((tm, A_IN), lambda i: (i % per_seq, i // per_seq)),)
        + tuple(pl.BlockSpec((tm, w), lambda i: (i, 0)) for w in widths),
        compiler_params=_cparams(("arbitrary",)),
        name="in_projection",
    )(x2, gain.reshape(1, d), scale1p.reshape(bm, 1, d), shift.reshape(bm, 1, d), w_in)
    return ua.reshape(t_len, b, A_IN), ub, uc, ud


def _head_sums(x, seg):
    hi = x.astype(BF16)
    lo = (x - hi.astype(F32)).astype(BF16)
    return (jnp.dot(hi, seg, preferred_element_type=F32) + jnp.dot(lo, seg, preferred_element_type=F32))


def _head_segments():
    idx = np.arange(GROUP_W) // HEAD_DIM
    return jnp.asarray(idx[:, None] == idx[None, :], BF16)


def _outproj_kernel(yf_ref, yb_ref, gb_ref, ob_ref, oc_ref, od_ref, gnw_ref, gnb_ref, seg_ref, w_ref, x_ref, g1_ref,
                    gn_ref, sc_ref, sh_ref, wr_ref, br_ref, xo_ref, h_ref, lg_ref):
    gw = GROUP_W
    o = yf_ref[...] + yb_ref[...]
    seg = seg_ref[...]
    mean = _head_sums(o, seg) * (1.0 / HEAD_DIM)
    cen = o - mean
    var = _head_sums(cen * cen, seg) * (1.0 / HEAD_DIM)
    oa = cen * lax.rsqrt(var + RWKV_GN_EPS) * gnw_ref[...] + gnb_ref[...]
    oa = (oa + gb_ref[:, gw:2 * gw]) * gb_ref[:, 0:gw]
    y = jnp.dot(oa.astype(BF16), w_ref[0:gw, :].astype(BF16), preferred_element_type=F32)
    for j, m_ref in enumerate((ob_ref, oc_ref, od_ref)):
        y = y + jnp.dot(m_ref[...].astype(BF16), w_ref[(j + 1) * gw:(j + 2) * gw, :].astype(BF16),
                        preferred_element_type=F32)
    xn = x_ref[...] + g1_ref[0] * y
    xo_ref[...] = xn
    ms = jnp.mean(xn * xn, axis=-1, keepdims=True)
    h = xn * lax.rsqrt(ms + NORM_EPS) * gn_ref[...]
    h = (h * sc_ref[0] + sh_ref[0]).astype(BF16)
    h_ref[...] = h
    lg_ref[...] = jnp.dot(h, wr_ref[...].astype(BF16), preferred_element_type=F32) + br_ref[...]


def _out_projection(yf, yb, gb, ob, oc, od, gn_w, gn_b, w_out, x2, gate1, gain, scale1p, shift, w_router, b_router,
                    t_len):
    n, d = x2.shape
    bm = gate1.shape[0]
    tm = ROW_TILE
    per_seq = t_len // tm
    gw = GROUP_W

    def mod_idx(i):
        return ((i // per_seq) if bm > 1 else 0, 0, 0)

    def tm_idx(i):
        return (i % per_seq, i // per_seq)

    def rows(w):
        return pl.BlockSpec((tm, w), lambda i: (i, 0))

    def full(shape):
        return pl.BlockSpec(shape, lambda i: (0,) * len(shape))

    mod_spec = pl.BlockSpec((1, 1, d), mod_idx)
    return pl.pallas_call(
        _outproj_kernel,
        out_shape=(jax.ShapeDtypeStruct((n, d), F32),
                   jax.ShapeDtypeStruct((n, d), BF16),
                   jax.ShapeDtypeStruct((n, ROUTER_W), F32)),
        grid=(n // tm,),
        in_specs=[pl.BlockSpec((tm, gw), tm_idx), pl.BlockSpec((tm, gw), tm_idx), pl.BlockSpec((tm, 2 * gw), tm_idx),
                  rows(gw), rows(gw), rows(gw), full((1, gw)), full((1, gw)), full((gw, gw)), full((d, d)), rows(d),
                  mod_spec, full((1, d)), mod_spec, mod_spec, full((d, ROUTER_W)), full((1, ROUTER_W))],
        out_specs=(rows(d), rows(d), rows(ROUTER_W)),
        compiler_params=_cparams(("arbitrary",)),
        name="out_projection",
    )(yf.reshape(t_len, -1), yb.reshape(t_len, -1), gb.reshape(t_len, -1), ob, oc, od,
      gn_w.reshape(1, gw), gn_b.reshape(1, gw), _head_segments(), w_out, x2, gate1.reshape(bm, 1, d),
      gain.reshape(1, d), scale1p.reshape(bm, 1, d), shift.reshape(bm, 1, d), w_router, b_router)


ACC_ROWS = HEAD_DIM + 16


def _flash_kernel(sc_ref, q_ref, k_ref, vt_ref, g_ref, o_ref, m_ref, acc_ref, sa_ref, sb_ref, *, heads, diff, chunk):
    tq = q_ref.shape[1]
    kdim = k_ref.shape[2]
    n_chunks = k_ref.shape[1] // chunk
    qt = q_ref[0].astype(F32).T.astype(BF16)

    def stationary(q_rows, k_row0):
        pieces = []
        if k_row0 > 0:
            pieces.append(jnp.zeros((k_row0, tq), BF16))
        pieces.append(q_rows)
        rest = kdim - k_row0 - q_rows.shape[0]
        if rest > 0:
            pieces.append(jnp.zeros((rest, tq), BF16))
        return jnp.concatenate(pieces, axis=0) if len(pieces) > 1 else pieces[0]

    streams = []
    for (maps, v_row0) in heads:
        for (qs, qn, ks) in maps:
            streams.append((stationary(qt[qs:qs + qn], ks), v_row0))
    m_ref[...] = jnp.full(m_ref.shape, MASK_VALUE, F32)
    acc_ref[...] = jnp.zeros(acc_ref.shape, F32)
    ones_rows = (lax.broadcasted_iota(jnp.int32, (ACC_ROWS - HEAD_DIM, chunk), 0) == 0).astype(BF16)

    def scores(c, s_ref):
        off = pl.multiple_of(c * chunk, chunk)
        kc = k_ref[0, pl.ds(off, chunk), :]
        for j, (w, _) in enumerate(streams):
            s_ref[j] = jnp.dot(kc, w, preferred_element_type=F32)

    def accumulate(c, s_ref):
        off = pl.multiple_of(c * chunk, chunk)
        for j, (_, v_row0) in enumerate(streams):
            s = s_ref[j]
            m = m_ref[j]
            m_new = jnp.maximum(m, jnp.max(s, axis=0, keepdims=True))
            p = jnp.exp2(s - m_new).astype(BF16)
            alpha = jnp.exp2(m - m_new)
            m_ref[j] = m_new
            vc = jnp.concatenate([vt_ref[0, pl.ds(v_row0, HEAD_DIM), pl.ds(off, chunk)], ones_rows], axis=0)
            acc_ref[j] = alpha * acc_ref[j] + jnp.dot(vc, p, preferred_element_type=F32)

    scores(0, sa_ref)

    def body(jj, carry):
        c = 2 * jj
        scores(c + 1, sb_ref)
        accumulate(c, sa_ref)
        scores(c + 2, sa_ref)
        accumulate(c + 1, sb_ref)
        return carry

    lax.fori_loop(0, (n_chunks - 1) // 2, body, 0)
    if n_chunks % 2 == 0:
        scores(n_chunks - 1, sb_ref)
        accumulate(n_chunks - 2, sa_ref)
        accumulate(n_chunks - 1, sb_ref)
    else:
        accumulate(n_chunks - 1, sa_ref)

    outs = []
    j = 0
    for (maps, v_row0) in heads:
        a1 = acc_ref[j]
        o = a1[:HEAD_DIM] / a1[HEAD_DIM:HEAD_DIM + 1]
        if diff:
            a2 = acc_ref[j + 1]
            o = o - (sc_ref[0] / a2[HEAD_DIM:HEAD_DIM + 1]) * a2[:HEAD_DIM]
            ms = jnp.mean(o * o, axis=0, keepdims=True)
            o = o * lax.rsqrt(ms + NORM_EPS) * g_ref[...] * sc_ref[1]
        j += len(maps)
        outs.append(o)
    o_ref[0] = jnp.concatenate(outs, axis=0).T


def _flash_attention(q, k, vt, heads, scalars=None, gain=None):
    b, s, c = q.shape
    l, ck = k.shape[1:]
    cv = vt.shape[1]
    tq = min(Q_TILE, s)
    chunk = min(KV_CHUNK, l)
    assert l % chunk == 0 and s % tq == 0
    diff = scalars is not None
    ns = sum(len(maps) for maps, _ in heads)
    if not diff:
        scalars = jnp.zeros((2,), F32)
        gain = jnp.ones((HEAD_DIM,), F32)
    return pl.pallas_call(
        functools.partial(_flash_kernel, heads=heads, diff=diff, chunk=chunk),
        out_shape=jax.ShapeDtypeStruct((b, s, c), F32),
        grid_spec=pltpu.PrefetchScalarGridSpec(
            num_scalar_prefetch=1,
            grid=(b, s // tq),
            in_specs=[pl.BlockSpec((1, tq, c), lambda bi, i, sc: (bi, i, 0)),
                      pl.BlockSpec((1, l, ck), lambda bi, i, sc: (bi, 0, 0)),
                      pl.BlockSpec((1, cv, l), lambda bi, i, sc: (bi, 0, 0)),
                      pl.BlockSpec((HEAD_DIM, 1), lambda bi, i, sc: (0, 0))],
            out_specs=pl.BlockSpec((1, tq, c), lambda bi, i, sc: (bi, i, 0)),
            scratch_shapes=[pltpu.VMEM((ns, 1, tq), F32), pltpu.VMEM((ns, ACC_ROWS, tq), F32),
                            pltpu.VMEM((ns, chunk, tq), F32), pltpu.VMEM((ns, chunk, tq), F32)]),
        compiler_params=_cparams(("arbitrary", "arbitrary")),
        name="flash_attention",
    )(scalars, q, k, vt, gain.reshape(HEAD_DIM, 1))


DIFF_HEADS = tuple((((h * HEAD_DIM, HEAD_DIM // 2, h * HEAD_DIM),
                     (h * HEAD_DIM + HEAD_DIM // 2, HEAD_DIM // 2, h * HEAD_DIM + HEAD_DIM // 2)), h * HEAD_DIM)
                   for h in range(GROUP_HEADS))
DENSE_HEADS = tuple((((h * HEAD_DIM, HEAD_DIM, h * HEAD_DIM),), h * HEAD_DIM) for h in range(GROUP_HEADS))
GQA_HEADS = tuple((((h * HEAD_DIM, HEAD_DIM, (h // 2) * HEAD_DIM),), (h // 2) * HEAD_DIM) for h in range(GROUP_HEADS))


def _na_kernel(q_ref, k_ref, v_ref, kct_ref, vc_ref, bias_ref, o_ref, *, rows):
    r = pl.program_id(1)
    rs = jnp.clip(r - NA_ROWS // 2, 0, rows - NA_ROWS)
    start = pl.multiple_of(rs * GRID_W, GRID_W)
    win = NA_ROWS * GRID_W
    for h in range(GROUP_HEADS):
        q = q_ref[0, h]
        kw = k_ref[0, h, pl.ds(start, win), :]
        vw = v_ref[0, h, pl.ds(start, win), :]
        s_nb = lax.dot_general(q, kw, (((1,), (1,)), ((), ())), preferred_element_type=F32)
        s_nb = s_nb + bias_ref[0, h]
        s_cx = jnp.dot(q, kct_ref[0, h], preferred_element_type=F32)
        m = jnp.maximum(jnp.max(s_nb, axis=-1, keepdims=True), jnp.max(s_cx, axis=-1, keepdims=True))
        p_nb = jnp.exp(s_nb - m)
        p_cx = jnp.exp(s_cx - m)
        l = jnp.sum(p_nb, axis=-1, keepdims=True) + jnp.sum(p_cx, axis=-1, keepdims=True)
        o = (jnp.dot(p_nb.astype(BF16), vw, preferred_element_type=F32)
             + jnp.dot(p_cx.astype(BF16), vc_ref[0, h], preferred_element_type=F32))
        o_ref[0, h] = o / l


def _na_bias_table(rel_bias):
    w = np.arange(GRID_W)
    col_start = np.clip(w - NA_COLS // 2, 0, GRID_W - NA_COLS)
    c = np.arange(GRID_W)
    valid = (c[None, :] >= col_start[:, None]) & (c[None, :] < col_start[:, None] + NA_COLS)
    cidx = np.clip(c[None, :] - w[:, None] + NA_COLS - 1, 0, 2 * NA_COLS - 2)
    delta = np.arange(NA_ROWS)
    i = np.arange(NA_ROWS)
    ridx = i[None, :] - delta[:, None] + NA_ROWS - 1
    t = rel_bias[:, ridx]
    t = t[:, :, :, cidx]
    t = jnp.where(valid[None, None, None], t, MASK_VALUE)
    t = jnp.transpose(t, (1, 0, 3, 2, 4))
    return t.reshape(NA_ROWS, rel_bias.shape[0], GRID_W, NA_ROWS * GRID_W).astype(F32)


def _neighborhood_attention(q, k, v, kct, vc, bias_tab):
    b, h, s, d = q.shape
    l = kct.shape[-1]
    rows = s // GRID_W
    assert rows >= NA_ROWS
    half = NA_ROWS // 2

    def bias_idx(bi, r):
        return (r - jnp.clip(r - half, 0, rows - NA_ROWS), 0, 0, 0)

    return pl.pallas_call(
        functools.partial(_na_kernel, rows=rows),
        out_shape=jax.ShapeDtypeStruct((b, h, s, d), F32),
        grid=(b, rows),
        in_specs=[pl.BlockSpec((1, h, GRID_W, d), lambda bi, r: (bi, 0, r, 0)),
                  pl.BlockSpec((1, h, s, d), lambda bi, r: (bi, 0, 0, 0)),
                  pl.BlockSpec((1, h, s, d), lambda bi, r: (bi, 0, 0, 0)),
                  pl.BlockSpec((1, h, d, l), lambda bi, r: (bi, 0, 0, 0)),
                  pl.BlockSpec((1, h, l, d), lambda bi, r: (bi, 0, 0, 0)),
                  pl.BlockSpec((1, h, GRID_W, NA_ROWS * GRID_W), bias_idx)],
        out_specs=pl.BlockSpec((1, h, GRID_W, d), lambda bi, r: (bi, 0, r, 0)),
        compiler_params=_cparams(("arbitrary", "arbitrary")),
        name="neighborhood_attention",
    )(q, k, v, kct, vc, bias_tab)


SCAN_BATCH = SUBLANES
N_KQ = 5
FWD_W = (N_KQ + 1) * GROUP_W
BWD_W = (N_KQ - 1) * GROUP_W
V_SPLIT = 4
V_TILES = HEAD_DIM // (V_SPLIT * SUBLANES)
KT_BUFFERS = 4


def _scan_kernel(zf_ref, zb_ref, zrv_ref, s0_ref, yf_ref, yb_ref, sfin_ref, st_ref, sa_ref, *kt_refs, steps):
    i = pl.program_id(1)

    @pl.when(i == 0)
    def _():
        st_ref[...] = s0_ref[0]

    lane = lax.broadcasted_iota(jnp.int32, (SUBLANES, LANES), 1)
    lane_vq = lane // (LANES // V_SPLIT)
    chan_vq = (lane // (V_TILES * SUBLANES)) % V_SPLIT

    def operand_tile(q, s):
        sb = steps - 1 - s
        pieces = []
        for hp in range(2):
            pieces.append(zf_ref[s, :, pl.ds(q * GROUP_W + hp * LANES, LANES)])
        for hp in range(2):
            if q < N_KQ - 1:
                pieces.append(zb_ref[sb, :, pl.ds(q * GROUP_W + hp * LANES, LANES)])
            else:
                pieces.append(zrv_ref[sb, :, pl.ds((q - (N_KQ - 1)) * GROUP_W + hp * LANES, LANES)])
        tile = jnp.concatenate(pieces * V_SPLIT, axis=0)
        return tile.T

    def prepare(kt_ref, s):
        for q in range(N_KQ + 1):
            kt_ref[q] = operand_tile(q, s)

    nacc = 4

    def tree(parts):
        return (parts[0] + parts[1]) + (parts[2] + parts[3])

    prepare(kt_refs[0], 0)
    prepare(kt_refs[1], 1)
    for hpar in range(2):
        for vb in range(V_TILES):
            parts = [None] * nacc
            for k in range(HEAD_DIM):
                term = st_ref[hpar, k, vb] * kt_refs[0][0, pl.ds(hpar * HEAD_DIM + k, 1), :]
                parts[k % nacc] = term if parts[k % nacc] is None else parts[k % nacc] + term
            sa_ref[hpar, vb] = -tree(parts)

    def step(s, kt_ref, nxt_ref, far_ref):
        prepare(far_ref, jnp.minimum(s + 2, steps - 1))
        vt = kt_ref[N_KQ]
        ys = []
        for hpar in range(2):
            base = hpar * HEAD_DIM
            vops = []
            for vb in range(V_TILES):
                acc = None
                for vq in range(V_SPLIT):
                    r0 = base + vq * V_TILES * SUBLANES + vb * SUBLANES
                    piece = vt[r0:r0 + SUBLANES, :]
                    acc = piece if acc is None else jnp.where(lane_vq == vq, piece, acc)
                vops.append(acc)
            sas = [sa_ref[hpar, vb] for vb in range(V_TILES)]
            yp = [[None] * nacc for _ in range(V_TILES)]
            sp = [[None] * nacc for _ in range(V_TILES)]
            for k in range(HEAD_DIM):
                row = pl.ds(base + k, 1)
                w = kt_ref[1, row, :]
                ka = kt_ref[2, row, :]
                kd = kt_ref[3, row, :]
                rr = kt_ref[4, row, :]
                kn = nxt_ref[0, row, :]
                j = k % nacc
                for vb in range(V_TILES):
                    sn = st_ref[hpar, k, vb] * w + sas[vb] * ka + vops[vb] * kd
                    st_ref[hpar, k, vb] = sn
                    ty = sn * rr
                    yp[vb][j] = ty if yp[vb][j] is None else yp[vb][j] + ty
                    tn = sn * kn
                    sp[vb][j] = tn if sp[vb][j] is None else sp[vb][j] + tn
            for vb in range(V_TILES):
                sa_ref[hpar, vb] = -tree(sp[vb])
            ys.append([tree(yp[vb]) for vb in range(V_TILES)])
        ytile = jnp.concatenate([ys[hpar][vb] for hpar in range(2) for _ in range(V_SPLIT) for vb in range(V_TILES)],
                                axis=0)
        yt = ytile.T
        sb = steps - 1 - s
        for d in range(2):
            for hp in range(2):
                acc = None
                for vq in range(V_SPLIT):
                    r0 = vq * (LANES // V_SPLIT) + d * 2 * SUBLANES + hp * SUBLANES
                    piece = yt[r0:r0 + SUBLANES, :]
                    acc = piece if acc is None else jnp.where(chan_vq == vq, piece, acc)
                if d == 0:
                    yf_ref[s, :, pl.ds(hp * LANES, LANES)] = acc
                else:
                    yb_ref[sb, :, pl.ds(hp * LANES, LANES)] = acc

    nbuf = len(kt_refs)

    def body(j, carry):
        for u in range(nbuf):
            step(nbuf * j + u, kt_refs[u], kt_refs[(u + 1) % nbuf], kt_refs[(u + 2) % nbuf])
        return carry

    lax.fori_loop(0, steps // nbuf, body, 0)

    @pl.when(i == pl.num_programs(1) - 1)
    def _():
        sfin_ref[0] = st_ref[...]


def _rwkv_scan(zf, zb, s0):
    t_len, b, _ = zf.shape
    groups = b // SCAN_BATCH
    ts = SCAN_STEPS
    nblk = t_len // ts
    st_shape = (2, HEAD_DIM, V_TILES, SUBLANES, LANES)
    return pl.pallas_call(
        functools.partial(_scan_kernel, steps=ts),
        out_shape=(jax.ShapeDtypeStruct((t_len, b, GROUP_W), F32),
                   jax.ShapeDtypeStruct((t_len, b, GROUP_W), F32),
                   jax.ShapeDtypeStruct((groups,) + st_shape, F32)),
        grid=(groups, nblk),
        in_specs=[pl.BlockSpec((ts, SCAN_BATCH, FWD_W), lambda g, i: (i, g, 0)),
                  pl.BlockSpec((ts, SCAN_BATCH, BWD_W), lambda g, i: (nblk - 1 - i, g, 0)),
                  pl.BlockSpec((ts, SCAN_BATCH, 2 * GROUP_W), lambda g, i: (nblk - 1 - i, g, (N_KQ - 1) // 2)),
                  pl.BlockSpec((1,) + st_shape, lambda g, i: (g, 0, 0, 0, 0, 0))],
        out_specs=(pl.BlockSpec((ts, SCAN_BATCH, GROUP_W), lambda g, i: (i, g, 0)),
                   pl.BlockSpec((ts, SCAN_BATCH, GROUP_W), lambda g, i: (nblk - 1 - i, g, 0)),
                   pl.BlockSpec((1,) + st_shape, lambda g, i: (g, 0, 0, 0, 0, 0))),
        scratch_shapes=[pltpu.VMEM(st_shape, F32), pltpu.VMEM((2, V_TILES, SUBLANES, LANES), F32)]
        + [pltpu.VMEM((N_KQ + 1, LANES, LANES), F32)] * KT_BUFFERS,
        compiler_params=_cparams(("arbitrary", "arbitrary")),
        name="rwkv_scan",
    )(zf, zb, zf, s0)


PREP_STEPS = 32
DECAY_RATE = math.exp(-0.5)


def _prep_kernel(u_ref, up_ref, un_ref, mu_ref, w0_ref, wup_ref, a0_ref, aup_ref, gup_ref, kk_ref, ka_ref, rk_ref,
                 seg_ref, zf_ref, zb_ref, gb_ref):
    i = pl.program_id(1)
    last = pl.num_programs(1) - 1
    tb = u_ref.shape[0]
    u = u_ref[...]
    before = jnp.where(i > 0, up_ref[...], 0.0)
    after = jnp.where(i < last, un_ref[...], 0.0)
    prev = jnp.concatenate([before, u[:-1]], axis=0)
    nxt = jnp.concatenate([u[1:], after], axis=0)
    mu = mu_ref[...]
    x = (u + mu[0:1] * (prev - u) + mu[1:2] * (nxt - u)).reshape(tb * SCAN_BATCH, A_IN)
    gw = GROUP_W
    r = x[:, 0:gw]
    k = x[:, gw:2 * gw]
    v = x[:, 2 * gw:3 * gw]
    wd = x[:, 3 * gw:3 * gw + 2 * LORA_W]
    ad = x[:, 3 * gw + 2 * LORA_W:3 * gw + 2 * LORA_W + 2 * LORA_A]
    gd = x[:, 3 * gw + 2 * LORA_W + 2 * LORA_A:]
    seg = seg_ref[...]
    w_raw = w0_ref[...] + jnp.dot(jnp.tanh(wd).astype(BF16), wup_ref[...].astype(BF16), preferred_element_type=F32)
    decay = jnp.exp(-DECAY_RATE * jax.nn.sigmoid(w_raw))
    a = jax.nn.sigmoid(a0_ref[...] + jnp.dot(ad.astype(BF16), aup_ref[...].astype(BF16),
                                             preferred_element_type=F32))
    g = jnp.dot(jax.nn.sigmoid(gd).astype(BF16), gup_ref[...].astype(BF16), preferred_element_type=F32)
    bonus = jnp.zeros_like(r)
    for d, z_ref in enumerate((zf_ref, zb_ref)):
        sl = slice(d * gw, (d + 1) * gw)
        kk = k * kk_ref[:, sl]
        a_d = a[:, sl]
        kd = k * (1.0 + (a_d - 1.0) * ka_ref[:, sl])
        kk = kk * lax.rsqrt(_head_sums(kk * kk, seg) + 1e-12)
        bonus = bonus + _head_sums(r * kd * rk_ref[:, sl], seg) * v
        for q, val in enumerate((kk, decay[:, sl], kk * a_d, kd)):
            z_ref[:, :, q * gw:(q + 1) * gw] = val.reshape(tb, SCAN_BATCH, gw)
    zf_ref[:, :, 4 * gw:5 * gw] = r.reshape(tb, SCAN_BATCH, gw)
    zf_ref[:, :, 5 * gw:6 * gw] = v.reshape(tb, SCAN_BATCH, gw)
    gb_ref[:, :, 0:gw] = g.reshape(tb, SCAN_BATCH, gw)
    gb_ref[:, :, gw:2 * gw] = bonus.reshape(tb, SCAN_BATCH, gw)


def _block_diag(w):
    z = jnp.zeros_like(w[0])
    return jnp.concatenate([jnp.concatenate([w[0], z], axis=1), jnp.concatenate([z, w[1]], axis=1)], axis=0)


def _rwkv_prepare(ua_tm, mu, w0, w_up, a0, a_up, g_up, k_k, k_a, r_k):
    t, b, _ = ua_tm.shape
    tb = min(PREP_STEPS, t)
    nblk = t // tb
    gw2 = 2 * GROUP_W

    def full(shape):
        return pl.BlockSpec(shape, lambda g, i: (0,) * len(shape))

    return pl.pallas_call(
        _prep_kernel,
        out_shape=(jax.ShapeDtypeStruct((t, b, FWD_W), F32), jax.ShapeDtypeStruct((t, b, BWD_W), F32),
                   jax.ShapeDtypeStruct((t, b, gw2), F32)),
        grid=(b // SCAN_BATCH, nblk),
        in_specs=[pl.BlockSpec((tb, SCAN_BATCH, A_IN), lambda g, i: (i, g, 0)),
                  pl.BlockSpec((1, SCAN_BATCH, A_IN), lambda g, i: (jnp.maximum(i * tb - 1, 0), g, 0)),
                  pl.BlockSpec((1, SCAN_BATCH, A_IN), lambda g, i: (jnp.minimum((i + 1) * tb, t - 1), g, 0)),
                  full((2, A_IN)), full((1, gw2)), full((2 * LORA_W, gw2)), full((1, gw2)), full((2 * LORA_A, gw2)),
                  full((LORA_G, GROUP_W)), full((1, gw2)), full((1, gw2)), full((1, gw2)),
                  full((GROUP_W, GROUP_W))],
        out_specs=(pl.BlockSpec((tb, SCAN_BATCH, FWD_W), lambda g, i: (i, g, 0)),
                   pl.BlockSpec((tb, SCAN_BATCH, BWD_W), lambda g, i: (i, g, 0)),
                   pl.BlockSpec((tb, SCAN_BATCH, gw2), lambda g, i: (i, g, 0))),
        compiler_params=_cparams(("arbitrary", "arbitrary")),
        name="rwkv_prepare",
    )(ua_tm, ua_tm, ua_tm, mu, w0.reshape(1, gw2), _block_diag(w_up), a0.reshape(1, gw2), _block_diag(a_up), g_up,
      k_k.reshape(1, gw2), k_a.reshape(1, gw2), r_k.reshape(1, gw2), _head_segments())


def _rwkv_time_mix(ua_tm, s0, mu, w0, w_up, a0, a_up, g_up, k_k, k_a, r_k):
    t, b, _ = ua_tm.shape
    h, n = GROUP_HEADS, HEAD_DIM
    bp = -(-b // SCAN_BATCH) * SCAN_BATCH
    groups = bp // SCAN_BATCH
    if bp != b:
        ua_tm = jnp.pad(ua_tm, ((0, 0), (0, bp - b), (0, 0)))
    zf, zb, gb = _rwkv_prepare(ua_tm, mu, w0, w_up, a0, a_up, g_up, k_k, k_a, r_k)
    if s0 is None:
        s0l = jnp.zeros((groups, 2, n, V_TILES, SUBLANES, LANES), F32)
    else:
        s0 = jnp.pad(s0.astype(F32), ((0, bp - b),) + ((0, 0),) * 4)
        s0l = s0.reshape(groups, SCAN_BATCH, 2, 2, 2, V_SPLIT, V_TILES, SUBLANES, n)
        s0l = jnp.transpose(s0l, (0, 4, 8, 6, 7, 5, 2, 3, 1)).reshape(groups, 2, n, V_TILES, SUBLANES, LANES)
    yf, yb, s_fin = _rwkv_scan(zf, zb, s0l)
    s_fin = s_fin.reshape(groups, 2, n, V_TILES, SUBLANES, V_SPLIT, 2, 2, SCAN_BATCH)
    s_fin = jnp.transpose(s_fin, (0, 8, 6, 7, 1, 5, 3, 4, 2)).reshape(bp, 2, h, n, n)[:b]
    return yf, yb, gb, s_fin


def _expert_kernel(be_ref, nu_ref, x_ref, w1_ref, w3_ref, w2_ref, o_ref, w1s, w3s, w2s):
    i = pl.program_id(0)
    e = be_ref[i]
    prev = be_ref[jnp.maximum(i - 1, 0)]

    @pl.when((i == 0) | (e != prev))
    def _():
        w1s[...] = w1_ref[0].astype(BF16)
        w3s[...] = w3_ref[0].astype(BF16)
        w2s[...] = w2_ref[0].astype(BF16)

    @pl.when(i < nu_ref[0])
    def _():
        x = x_ref[...]
        a = jnp.dot(x, w1s[...], preferred_element_type=F32)
        g = jnp.dot(x, w3s[...], preferred_element_type=F32)
        hmid = (a * jax.nn.sigmoid(a)) * g
        o_ref[...] = jnp.dot(hmid.astype(BF16), w2s[...], preferred_element_type=F32)

    @pl.when(i >= nu_ref[0])
    def _():
        o_ref[...] = jnp.zeros_like(o_ref)


def _expert_mlp(xb, block_e, n_used, w1, w3, w2):
    cap, d = xb.shape
    bm = MOE_TILE
    de = w1.shape[-1]
    return pl.pallas_call(
        _expert_kernel,
        out_shape=jax.ShapeDtypeStruct((cap, d), F32),
        grid_spec=pltpu.PrefetchScalarGridSpec(
            num_scalar_prefetch=2,
            grid=(cap // bm,),
            in_specs=[pl.BlockSpec((bm, d), lambda i, be, nu: (i, 0)),
                      pl.BlockSpec((1, d, de), lambda i, be, nu: (be[i], 0, 0)),
                      pl.BlockSpec((1, d, de), lambda i, be, nu: (be[i], 0, 0)),
                      pl.BlockSpec((1, de, d), lambda i, be, nu: (be[i], 0, 0))],
            out_specs=pl.BlockSpec((bm, d), lambda i, be, nu: (i, 0)),
            scratch_shapes=[pltpu.VMEM((d, de), BF16), pltpu.VMEM((d, de), BF16), pltpu.VMEM((de, d), BF16)]),
        compiler_params=_cparams(("arbitrary",)),
        name="expert_mlp",
    )(block_e, n_used, xb, w1, w3, w2)


def _hier_moe(h_bf, logits, w1, w3, w2):
    n, d = h_bf.shape
    bm = MOE_TILE
    g_logits = logits[:, :N_EXPERT_GROUPS]
    g_idx = jnp.argmax(g_logits, axis=-1)
    g_top = jnp.max(jax.nn.softmax(g_logits, axis=-1), axis=-1)
    e_logits = logits[:, N_EXPERT_GROUPS:N_EXPERT_GROUPS + N_EXPERTS].reshape(n, N_EXPERT_GROUPS, EXPERTS_PER_GROUP)
    e_logits = jnp.take_along_axis(e_logits, g_idx[:, None, None], axis=1)[:, 0]
    top_v, top_i = lax.top_k(e_logits, TOP_K)
    gate = jax.nn.softmax(top_v, axis=-1) * g_top[:, None]
    eid = (g_idx[:, None] * EXPERTS_PER_GROUP + top_i).reshape(-1).astype(jnp.int32)
    tok = jnp.repeat(jnp.arange(n, dtype=jnp.int32), TOP_K)
    n_assign = n * TOP_K
    onehot = (eid[:, None] == jnp.arange(N_EXPERTS, dtype=jnp.int32)[None, :]).astype(jnp.int32)
    csum = jnp.cumsum(onehot, axis=0)
    counts = csum[-1]
    rank = jnp.take_along_axis(csum, eid[:, None], axis=1)[:, 0] - 1
    padded = (counts + bm - 1) // bm * bm
    pad_end = jnp.cumsum(padded)
    pad_start = pad_end - padded
    dest = pad_start[eid] + rank
    n_blocks = -(-(n_assign + N_EXPERTS * (bm - 1)) // bm)
    cap = n_blocks * bm
    buf_tok = jnp.zeros((cap,), jnp.int32).at[dest].set(tok)
    block_e = jnp.minimum(jnp.searchsorted(pad_end, jnp.arange(n_blocks, dtype=jnp.int32) * bm, side='right'),
                          N_EXPERTS - 1).astype(jnp.int32)
    n_used = (pad_end[-1:] // bm).astype(jnp.int32)
    xb = h_bf[buf_tok]
    yb = _expert_mlp(xb, block_e, n_used, w1, w3, w2)
    d2 = dest.reshape(n, TOP_K)
    return yb[d2[:, 0]] * gate[:, 0:1] + yb[d2[:, 1]] * gate[:, 1:2]


def _rms(x, g):
    return x * lax.rsqrt(jnp.mean(x * x, axis=-1, keepdims=True) + NORM_EPS) * g


def _axial_rope(t_len, dim):
    q4 = dim // 4
    inv = ROPE_THETA ** (-jnp.arange(q4, dtype=F32) / q4)
    t = jnp.arange(t_len)
    row = (t // GRID_W).astype(F32)
    col = (t % GRID_W).astype(F32)
    ang = jnp.stack([row[:, None] * inv, col[:, None] * inv], axis=1)
    return jnp.cos(ang), jnp.sin(ang)


def _rope_tokens(x, cs):
    cos, sin = cs
    lead = x.ndim - 3
    shape = (1, cos.shape[0]) + (1,) * lead + cos.shape[1:]
    cos = cos.reshape(shape)
    sin = sin.reshape(shape)
    xf = x.reshape(*x.shape[:-1], 2, 2, x.shape[-1] // 4)
    x1 = xf[..., 0, :]
    x2 = xf[..., 1, :]
    out = jnp.stack([x1 * cos - x2 * sin, x2 * cos + x1 * sin], axis=-2)
    return out.reshape(x.shape)


def _token_mixers(ua_tm, ub, uc, ud, rope, ctx, l, W):
    b, t, _ = ub.shape
    nh, n = GROUP_HEADS, HEAD_DIM
    latent = ctx is not None

    yf, yb, gb, s_fin = _rwkv_time_mix(ua_tm, ctx[0] if latent else None, W['rw_shift'][l], W['rw_w0'][l],
                                       W['rw_w_up'][l], W['rw_a0'][l], W['rw_a_up'][l], W['rw_g_up'][l],
                                       W['rw_k_k'][l], W['rw_k_a'][l], W['rw_r_k'][l])

    qb, kb, vb = jnp.split(ub, 3, axis=-1)
    qb = _rms(qb.reshape(b, t, nh, 2, DIFF_DQK), W['diff_qk_norm'][l, 0])
    kb = _rms(kb.reshape(b, t, nh, 2, DIFF_DQK), W['diff_qk_norm'][l, 1])
    if latent:
        qb = _rope_tokens(qb, rope[0])
        k_ctx = jnp.transpose(ctx[1], (0, 3, 1, 2, 4)).reshape(b, -1, GROUP_W)
        v_ctx = jnp.transpose(ctx[2], (0, 2, 1, 3)).reshape(b, -1, GROUP_W)
        kb_all = jnp.concatenate([k_ctx, _rope_tokens(kb, rope[0]).reshape(b, t, GROUP_W)], axis=1)
        vb_all = jnp.concatenate([v_ctx, vb], axis=1)
    else:
        kb_all, vb_all = kb.reshape(b, t, GROUP_W), vb
    lam_init = 0.8 - 0.6 * math.exp(-0.3 * l)
    lv = W['diff_lambda'][l].astype(F32)
    lam = jnp.exp(jnp.sum(lv[0] * lv[1])) - jnp.exp(jnp.sum(lv[2] * lv[3])) + lam_init
    scalars = jnp.stack([lam, jnp.asarray(1.0 - lam_init, F32)]).astype(F32)
    out_b = _flash_attention((qb * (LOG2E / math.sqrt(DIFF_DQK))).reshape(b, t, GROUP_W).astype(BF16),
                             kb_all.astype(BF16), jnp.swapaxes(vb_all, 1, 2).astype(BF16),
                             DIFF_HEADS, scalars, W['diff_subln'][l])

    qc, kc, vc = jnp.split(uc, 3, axis=-1)
    qc = _rms(qc.reshape(b, t, nh, n), W['na_qk_norm'][l, 0])
    kc = _rms(kc.reshape(b, t, nh, n), W['na_qk_norm'][l, 1])
    if latent:
        vch = vc.reshape(b, t, nh, n).transpose(0, 2, 1, 3)
        oc = _neighborhood_attention((qc * (1.0 / math.sqrt(n))).transpose(0, 2, 1, 3).astype(BF16),
                                     kc.transpose(0, 2, 1, 3).astype(BF16), vch.astype(BF16),
                                     jnp.swapaxes(ctx[3], -1, -2).astype(BF16), ctx[4].astype(BF16),
                                     _na_bias_table(W['na_rel_bias'][l]))
        out_c = oc.transpose(0, 2, 1, 3).reshape(b, t, GROUP_W)
    else:
        out_c = _flash_attention((qc * (LOG2E / math.sqrt(n))).reshape(b, t, GROUP_W).astype(BF16),
                                 kc.reshape(b, t, GROUP_W).astype(BF16), jnp.swapaxes(vc, 1, 2).astype(BF16),
                                 DENSE_HEADS)

    kvw = GQA_KV_HEADS * n
    qd, kd, vd = jnp.split(ud, [GROUP_W, GROUP_W + kvw], axis=-1)
    qd = _rms(qd.reshape(b, t, nh, n), W['gqa_qk_norm'][l, 0])
    kd = _rms(kd.reshape(b, t, GQA_KV_HEADS, n), W['gqa_qk_norm'][l, 1])
    if latent:
        qd = _rope_tokens(qd, rope[1])
        k_ctx = jnp.transpose(ctx[5], (0, 2, 1, 3)).reshape(b, -1, kvw)
        v_ctx = jnp.transpose(ctx[6], (0, 2, 1, 3)).reshape(b, -1, kvw)
        kd_all = jnp.concatenate([k_ctx, _rope_tokens(kd, rope[1]).reshape(b, t, kvw)], axis=1)
        vd_all = jnp.concatenate([v_ctx, vd], axis=1)
    else:
        kd_all, vd_all = kd.reshape(b, t, kvw), vd
    out_d = _flash_attention((qd * (LOG2E / math.sqrt(n))).reshape(b, t, GROUP_W).astype(BF16),
                             kd_all.astype(BF16), jnp.swapaxes(vd_all, 1, 2).astype(BF16), GQA_HEADS)

    mix = (yf, yb, gb) + tuple(z.reshape(b * t, GROUP_W) for z in (out_b, out_c, out_d))
    if latent:
        return mix, None
    new_ctx = (s_fin, jnp.transpose(kb, (0, 2, 3, 1, 4)), vb.reshape(b, t, nh, n).transpose(0, 2, 1, 3),
               kc.transpose(0, 2, 1, 3), vc.reshape(b, t, nh, n).transpose(0, 2, 1, 3),
               kd.transpose(0, 2, 1, 3), vd.reshape(b, t, GQA_KV_HEADS, n).transpose(0, 2, 1, 3))
    return mix, new_ctx


def _trunk_layer(x, cond, rope, ctx, l, W):
    b, t, d = x.shape
    n = b * t
    mod = _modulation(cond, W['w_mod'][l], W['b_mod'][l])
    sh1, sc1, g1, sh2, sc2, g2 = jnp.split(mod, 6, axis=-1)
    x2 = x.reshape(n, d)
    ua, ub, uc, ud = _in_projection(x2, W['norm_mix'][l], 1.0 + sc1, sh1, W['w_in'][l], t)
    mix, new_ctx = _token_mixers(ua, ub.reshape(b, t, -1), uc.reshape(b, t, -1), ud.reshape(b, t, -1), rope, ctx, l, W)
    w_router = jnp.concatenate([W['moe_w_group'][l], W['moe_w_router'][l]], axis=1)
    w_router = jnp.pad(w_router, ((0, 0), (0, ROUTER_W - w_router.shape[1])))
    b_router = jnp.concatenate([W['moe_b_group'][l], W['moe_b_router'][l]])
    b_router = jnp.pad(b_router, (0, ROUTER_W - b_router.shape[0])).reshape(1, ROUTER_W)
    x2, h2, logits = _out_projection(*mix, W['rw_gn_w'][l], W['rw_gn_b'][l], W['w_out'][l], x2, g1,
                                     W['norm_ffn'][l], 1.0 + sc2, sh2, w_router, b_router, t)
    y = _hier_moe(h2, logits, W['moe_w1'][l], W['moe_w3'][l], W['moe_w2'][l])
    g2r = jnp.broadcast_to(g2[:, None, :], (g2.shape[0], n // g2.shape[0], d)).reshape(n, d)
    x2 = x2 + g2r * y
    return x2.reshape(b, t, d), new_ctx


def kernel(x_prompt, x_sample, c, state_rwkv, cache_diff_k, cache_diff_v, cache_na_k, cache_na_v,
           cache_gqa_k, cache_gqa_v, c_ctx, norm_mix, norm_ffn, w_mod, b_mod, w_in, w_out,
           rw_shift, rw_w0, rw_w_up, rw_a0, rw_a_up, rw_g_up, rw_k_k, rw_k_a, rw_r_k, rw_gn_w, rw_gn_b,
           diff_qk_norm, diff_lambda, diff_subln, na_qk_norm, na_rel_bias, gqa_qk_norm,
           moe_w_group, moe_b_group, moe_w_router, moe_b_router, moe_w1, moe_w3, moe_w2):
    W = {
        'norm_mix': norm_mix, 'norm_ffn': norm_ffn, 'w_mod': w_mod, 'b_mod': b_mod,
        'w_in': w_in, 'w_out': w_out, 'rw_shift': rw_shift, 'rw_w0': rw_w0, 'rw_w_up': rw_w_up,
        'rw_a0': rw_a0, 'rw_a_up': rw_a_up, 'rw_g_up': rw_g_up, 'rw_k_k': rw_k_k, 'rw_k_a': rw_k_a,
        'rw_r_k': rw_r_k, 'rw_gn_w': rw_gn_w, 'rw_gn_b': rw_gn_b, 'diff_qk_norm': diff_qk_norm,
        'diff_lambda': diff_lambda, 'diff_subln': diff_subln, 'na_qk_norm': na_qk_norm,
        'na_rel_bias': na_rel_bias, 'gqa_qk_norm': gqa_qk_norm, 'moe_w_group': moe_w_group,
        'moe_b_group': moe_b_group, 'moe_w_router': moe_w_router, 'moe_b_router': moe_b_router,
        'moe_w1': moe_w1, 'moe_w3': moe_w3, 'moe_w2': moe_w2,
    }
    depth = w_in.shape[0]

    xp = x_prompt
    cond_ctx = c_ctx[None, :]
    ctx_layers = []
    for l in range(depth):
        xp, new_ctx = _trunk_layer(xp, cond_ctx, None, None, l, W)
        ctx_layers.append(new_ctx)
    new_caches = tuple(jnp.stack([z[i] for z in ctx_layers], axis=1) for i in range(7))

    t_lat = x_sample.shape[1]
    rope = (_axial_rope(t_lat, DIFF_DQK), _axial_rope(t_lat, HEAD_DIM))
    xs = x_sample
    for l in range(depth):
        ctx = (state_rwkv[:, l], cache_diff_k[:, l], cache_diff_v[:, l], cache_na_k[:, l],
               cache_na_v[:, l], cache_gqa_k[:, l], cache_gqa_v[:, l])
        xs, _ = _trunk_layer(xs, c, rope, ctx, l, W)
    return (xp, xs) + new_caches
```

```python
import functools
import math

import numpy as np
import jax
import jax.numpy as jnp
from jax import lax
from jax.experimental import pallas as pl
from jax.experimental.pallas import tpu as pltpu

F32 = jnp.float32
BF16 = jnp.bfloat16

D_MODEL = 1024
GRID_W = 64
HEAD_DIM = 64
GROUP_W = 256
GROUP_HEADS = 4
LORA_W = 64
LORA_A = 64
LORA_G = 128
RWKV_GN_EPS = 64e-5
DIFF_DQK = 32
GQA_KV_HEADS = 2
GQA_GROUP = 2
NA_ROWS = 8
NA_COLS = 16
ROPE_THETA = 10000.0
NORM_EPS = 1e-6
N_EXPERT_GROUPS = 4
EXPERTS_PER_GROUP = 8
N_EXPERTS = 32
TOP_K = 2
D_EXPERT = 512
A_IN = 3 * GROUP_W + 2 * LORA_W + 2 * LORA_A + LORA_G
B_IN = 3 * GROUP_W
C_IN = 3 * GROUP_W
D_IN = GROUP_W + 2 * GQA_KV_HEADS * HEAD_DIM
IN_W = A_IN + B_IN + C_IN + D_IN

LANES = 128
SUBLANES = 8
VMEM_LIMIT = 48 * 1024 * 1024

ROW_TILE = 256
Q_TILE = 256
SCAN_STEPS = 16
KV_CHUNK = 512
MOE_TILE = 256
ROUTER_W = 128
MASK_VALUE = -1e30
LOG2E = 1.4426950408889634


def _cparams(sem):
    return pltpu.CompilerParams(dimension_semantics=sem, vmem_limit_bytes=VMEM_LIMIT)


def _mod_kernel(c_ref, w_ref, b_ref, o_ref):
    c = c_ref[...]
    a = c * jax.nn.sigmoid(c)
    o_ref[...] = jnp.dot(a, w_ref[...], preferred_element_type=F32,
                         precision=lax.Precision.HIGHEST) + b_ref[...]


def _modulation(cond, w, b):
    m, d = cond.shape
    n = w.shape[1]
    mp = -(-m // SUBLANES) * SUBLANES
    cp = jnp.pad(cond, ((0, mp - m), (0, 0)))
    tn = 768
    out = pl.pallas_call(
        _mod_kernel,
        out_shape=jax.ShapeDtypeStruct((mp, n), F32),
        grid=(n // tn,),
        in_specs=[pl.BlockSpec((mp, d), lambda j: (0, 0)),
                  pl.BlockSpec((d, tn), lambda j: (0, j)),
                  pl.BlockSpec((1, tn), lambda j: (0, j))],
        out_specs=pl.BlockSpec((mp, tn), lambda j: (0, j)),
        compiler_params=_cparams(("arbitrary",)),
        name="modulation",
    )(cp, w, b.reshape(1, n))
    return out[:m]


def _inproj_kernel(x_ref, g_ref, sc_ref, sh_ref, w_ref, oa_ref, ob_ref, oc_ref, od_ref):
    x = x_ref[...]
    ms = jnp.mean(x * x, axis=-1, keepdims=True)
    h = x * lax.rsqrt(ms + NORM_EPS) * g_ref[...]
    h = h * sc_ref[0] + sh_ref[0]
    u = jnp.dot(h.astype(BF16), w_ref[...].astype(BF16), preferred_element_type=F32)
    oa_ref[...] = u[:, :A_IN]
    ob_ref[...] = u[:, A_IN:A_IN + B_IN]
    oc_ref[...] = u[:, A_IN + B_IN:A_IN + B_IN + C_IN]
    od_ref[...] = u[:, A_IN + B_IN + C_IN:]


def _in_projection(x2, gain, scale1p, shift, w_in, t_len):
    n, d = x2.shape
    bm = scale1p.shape[0]
    tm = ROW_TILE
    per_seq = t_len // tm
    b = n // t_len

    def mod_idx(i):
        return ((i // per_seq) if bm > 1 else 0, 0, 0)

    widths = (B_IN, C_IN, D_IN)
    ua, ub, uc, ud = pl.pallas_call(
        _inproj_kernel,
        out_shape=(jax.ShapeDtypeStruct((t_len, b * A_IN), F32),)
        + tuple(jax.ShapeDtypeStruct((n, w), F32) for w in widths),
        grid=(n // tm,),
        in_specs=[pl.BlockSpec((tm, d), lambda i: (i, 0)),
                  pl.BlockSpec((1, d), lambda i: (0, 0)),
                  pl.BlockSpec((1, 1, d), mod_idx),
                  pl.BlockSpec((1, 1, d), mod_idx),
                  pl.BlockSpec((d, IN_W), lambda i: (0, 0))],
        out_specs=(pl.BlockSpec((tm, A_IN), lambda i: (i % per_seq, i // per_seq)),)
        + tuple(pl.BlockSpec((tm, w), lambda i: (i, 0)) for w in widths),
        compiler_params=_cparams(("arbitrary",)),
        name="in_projection",
    )(x2, gain.reshape(1, d), scale1p.reshape(bm, 1, d), shift.reshape(bm, 1, d), w_in)
    return ua.reshape(t_len, b, A_IN), ub, uc, ud


def _head_sums(x, seg):
    hi = x.astype(BF16)
    lo = (x - hi.astype(F32)).astype(BF16)
    return (jnp.dot(hi, seg, preferred_element_type=F32) + jnp.dot(lo, seg, preferred_element_type=F32))


def _head_segments():
    idx = np.arange(GROUP_W) // HEAD_DIM
    return jnp.asarray(idx[:, None] == idx[None, :], BF16)


def _outproj_kernel(yf_ref, yb_ref, gb_ref, ob_ref, oc_ref, od_ref, gnw_ref, gnb_ref, seg_ref, w_ref, x_ref, g1_ref,
                    gn_ref, sc_ref, sh_ref, wr_ref, br_ref, xo_ref, h_ref, lg_ref):
    gw = GROUP_W
    o = yf_ref[...] + yb_ref[...]
    seg = seg_ref[...]
    mean = _head_sums(o, seg) * (1.0 / HEAD_DIM)
    cen = o - mean
    var = _head_sums(cen * cen, seg) * (1.0 / HEAD_DIM)
    oa = cen * lax.rsqrt(var + RWKV_GN_EPS) * gnw_ref[...] + gnb_ref[...]
    oa = (oa + gb_ref[:, gw:2 * gw]) * gb_ref[:, 0:gw]
    y = jnp.dot(oa.astype(BF16), w_ref[0:gw, :].astype(BF16), preferred_element_type=F32)
    for j, m_ref in enumerate((ob_ref, oc_ref, od_ref)):
        y = y + jnp.dot(m_ref[...].astype(BF16), w_ref[(j + 1) * gw:(j + 2) * gw, :].astype(BF16),
                        preferred_element_type=F32)
    xn = x_ref[...] + g1_ref[0] * y
    xo_ref[...] = xn
    ms = jnp.mean(xn * xn, axis=-1, keepdims=True)
    h = xn * lax.rsqrt(ms + NORM_EPS) * gn_ref[...]
    h = (h * sc_ref[0] + sh_ref[0]).astype(BF16)
    h_ref[...] = h
    lg_ref[...] = jnp.dot(h, wr_ref[...].astype(BF16), preferred_element_type=F32) + br_ref[...]


def _out_projection(yf, yb, gb, ob, oc, od, gn_w, gn_b, w_out, x2, gate1, gain, scale1p, shift, w_router, b_router,
                    t_len):
    n, d = x2.shape
    bm = gate1.shape[0]
    tm = ROW_TILE
    per_seq = t_len // tm
    gw = GROUP_W

    def mod_idx(i):
        return ((i // per_seq) if bm > 1 else 0, 0, 0)

    def tm_idx(i):
        return (i % per_seq, i // per_seq)

    def rows(w):
        return pl.BlockSpec((tm, w), lambda i: (i, 0))

    def full(shape):
        return pl.BlockSpec(shape, lambda i: (0,) * len(shape))

    mod_spec = pl.BlockSpec((1, 1, d), mod_idx)
    return pl.pallas_call(
        _outproj_kernel,
        out_shape=(jax.ShapeDtypeStruct((n, d), F32),
                   jax.ShapeDtypeStruct((n, d), BF16),
                   jax.ShapeDtypeStruct((n, ROUTER_W), F32)),
        grid=(n // tm,),
        in_specs=[pl.BlockSpec((tm, gw), tm_idx), pl.BlockSpec((tm, gw), tm_idx), pl.BlockSpec((tm, 2 * gw), tm_idx),
                  rows(gw), rows(gw), rows(gw), full((1, gw)), full((1, gw)), full((gw, gw)), full((d, d)), rows(d),
                  mod_spec, full((1, d)), mod_spec, mod_spec, full((d, ROUTER_W)), full((1, ROUTER_W))],
        out_specs=(rows(d), rows(d), rows(ROUTER_W)),
        compiler_params=_cparams(("arbitrary",)),
        name="out_projection",
    )(yf.reshape(t_len, -1), yb.reshape(t_len, -1), gb.reshape(t_len, -1), ob, oc, od,
      gn_w.reshape(1, gw), gn_b.reshape(1, gw), _head_segments(), w_out, x2, gate1.reshape(bm, 1, d),
      gain.reshape(1, d), scale1p.reshape(bm, 1, d), shift.reshape(bm, 1, d), w_router, b_router)


ACC_ROWS = HEAD_DIM + 16


def _flash_kernel(sc_ref, q_ref, k_ref, vt_ref, g_ref, o_ref, m_ref, acc_ref, sa_ref, sb_ref, *, heads, diff, chunk):
    tq = q_ref.shape[1]
    kdim = k_ref.shape[2]
    n_chunks = k_ref.shape[1] // chunk
    qt = q_ref[0].astype(F32).T.astype(BF16)

    def stationary(q_rows, k_row0):
        pieces = []
        if k_row0 > 0:
            pieces.append(jnp.zeros((k_row0, tq), BF16))
        pieces.append(q_rows)
        rest = kdim - k_row0 - q_rows.shape[0]
        if rest > 0:
            pieces.append(jnp.zeros((rest, tq), BF16))
        return jnp.concatenate(pieces, axis=0) if len(pieces) > 1 else pieces[0]

    streams = []
    for (maps, v_row0) in heads:
        for (qs, qn, ks) in maps:
            streams.append((stationary(qt[qs:qs + qn], ks), v_row0))
    m_ref[...] = jnp.full(m_ref.shape, MASK_VALUE, F32)
    acc_ref[...] = jnp.zeros(acc_ref.shape, F32)
    ones_rows = (lax.broadcasted_iota(jnp.int32, (ACC_ROWS - HEAD_DIM, chunk), 0) == 0).astype(BF16)

    def score(c, s_ref, j):
        off = pl.multiple_of(c * chunk, chunk)
        s_ref[j] = jnp.dot(k_ref[0, pl.ds(off, chunk), :], streams[j][0], preferred_element_type=F32)

    def accumulate(c, s_ref, j):
        off = pl.multiple_of(c * chunk, chunk)
        v_row0 = streams[j][1]
        s = s_ref[j]
        m = m_ref[j]
        m_new = jnp.maximum(m, jnp.max(s, axis=0, keepdims=True))
        p = jnp.exp2(s - m_new).astype(BF16)
        alpha = jnp.exp2(m - m_new)
        m_ref[j] = m_new
        vc = jnp.concatenate([vt_ref[0, pl.ds(v_row0, HEAD_DIM), pl.ds(off, chunk)], ones_rows], axis=0)
        acc_ref[j] = alpha * acc_ref[j] + jnp.dot(vc, p, preferred_element_type=F32)

    ns = len(streams)

    def stage(c_next, next_ref, c_cur, cur_ref):
        for j in range(ns):
            if c_next is not None:
                score(c_next, next_ref, j)
            if c_cur is not None:
                accumulate(c_cur, cur_ref, j)

    stage(0, sa_ref, None, None)

    def body(jj, carry):
        c = 2 * jj
        stage(c + 1, sb_ref, c, sa_ref)
        stage(c + 2, sa_ref, c + 1, sb_ref)
        return carry

    lax.fori_loop(0, (n_chunks - 1) // 2, body, 0)
    if n_chunks % 2 == 0:
        stage(n_chunks - 1, sb_ref, n_chunks - 2, sa_ref)
        stage(None, None, n_chunks - 1, sb_ref)
    else:
        stage(None, None, n_chunks - 1, sa_ref)

    outs = []
    j = 0
    for (maps, v_row0) in heads:
        a1 = acc_ref[j]
        o = a1[:HEAD_DIM] / a1[HEAD_DIM:HEAD_DIM + 1]
        if diff:
            a2 = acc_ref[j + 1]
            o = o - (sc_ref[0] / a2[HEAD_DIM:HEAD_DIM + 1]) * a2[:HEAD_DIM]
            ms = jnp.mean(o * o, axis=0, keepdims=True)
            o = o * lax.rsqrt(ms + NORM_EPS) * g_ref[...] * sc_ref[1]
        j += len(maps)
        outs.append(o)
    o_ref[0] = jnp.concatenate(outs, axis=0).T


def _flash_attention(q, k, vt, heads, scalars=None, gain=None):
    b, s, c = q.shape
    l, ck = k.shape[1:]
    cv = vt.shape[1]
    tq = min(Q_TILE, s)
    chunk = min(KV_CHUNK, l)
    assert l % chunk == 0 and s % tq == 0
    diff = scalars is not None
    ns = sum(len(maps) for maps, _ in heads)
    if not diff:
        scalars = jnp.zeros((2,), F32)
        gain = jnp.ones((HEAD_DIM,), F32)
    return pl.pallas_call(
        functools.partial(_flash_kernel, heads=heads, diff=diff, chunk=chunk),
        out_shape=jax.ShapeDtypeStruct((b, s, c), F32),
        grid_spec=pltpu.PrefetchScalarGridSpec(
            num_scalar_prefetch=1,
            grid=(b, s // tq),
            in_specs=[pl.BlockSpec((1, tq, c), lambda bi, i, sc: (bi, i, 0)),
                      pl.BlockSpec((1, l, ck), lambda bi, i, sc: (bi, 0, 0)),
                      pl.BlockSpec((1, cv, l), lambda bi, i, sc: (bi, 0, 0)),
                      pl.BlockSpec((HEAD_DIM, 1), lambda bi, i, sc: (0, 0))],
            out_specs=pl.BlockSpec((1, tq, c), lambda bi, i, sc: (bi, i, 0)),
            scratch_shapes=[pltpu.VMEM((ns, 1, tq), F32), pltpu.VMEM((ns, ACC_ROWS, tq), F32),
                            pltpu.VMEM((ns, chunk, tq), F32), pltpu.VMEM((ns, chunk, tq), F32)]),
        compiler_params=_cparams(("arbitrary", "arbitrary")),
        name="flash_attention",
    )(scalars, q, k, vt, gain.reshape(HEAD_DIM, 1))


DIFF_HEADS = tuple((((h * HEAD_DIM, HEAD_DIM // 2, h * HEAD_DIM),
                     (h * HEAD_DIM + HEAD_DIM // 2, HEAD_DIM // 2, h * HEAD_DIM + HEAD_DIM // 2)), h * HEAD_DIM)
                   for h in range(GROUP_HEADS))
DENSE_HEADS = tuple((((h * HEAD_DIM, HEAD_DIM, h * HEAD_DIM),), h * HEAD_DIM) for h in range(GROUP_HEADS))
GQA_HEADS = tuple((((h * HEAD_DIM, HEAD_DIM, (h // 2) * HEAD_DIM),), (h // 2) * HEAD_DIM) for h in range(GROUP_HEADS))


def _na_kernel(q_ref, k_ref, v_ref, kct_ref, vc_ref, bias_ref, o_ref, *, rows):
    r = pl.program_id(1)
    rs = jnp.clip(r - NA_ROWS // 2, 0, rows - NA_ROWS)
    start = pl.multiple_of(rs * GRID_W, GRID_W)
    win = NA_ROWS * GRID_W
    for h in range(GROUP_HEADS):
        q = q_ref[0, h]
        kw = k_ref[0, h, pl.ds(start, win), :]
        vw = v_ref[0, h, pl.ds(start, win), :]
        s_nb = lax.dot_general(q, kw, (((1,), (1,)), ((), ())), preferred_element_type=F32)
        s_nb = s_nb + bias_ref[0, h]
        s_cx = jnp.dot(q, kct_ref[0, h], preferred_element_type=F32)
        m = jnp.maximum(jnp.max(s_nb, axis=-1, keepdims=True), jnp.max(s_cx, axis=-1, keepdims=True))
        p_nb = jnp.exp(s_nb - m)
        p_cx = jnp.exp(s_cx - m)
        l = jnp.sum(p_nb, axis=-1, keepdims=True) + jnp.sum(p_cx, axis=-1, keepdims=True)
        o = (jnp.dot(p_nb.astype(BF16), vw, preferred_element_type=F32)
             + jnp.dot(p_cx.astype(BF16), vc_ref[0, h], preferred_element_type=F32))
        o_ref[0, h] = o / l


def _na_bias_table(rel_bias):
    w = np.arange(GRID_W)
    col_start = np.clip(w - NA_COLS // 2, 0, GRID_W - NA_COLS)
    c = np.arange(GRID_W)
    valid = (c[None, :] >= col_start[:, None]) & (c[None, :] < col_start[:, None] + NA_COLS)
    cidx = np.clip(c[None, :] - w[:, None] + NA_COLS - 1, 0, 2 * NA_COLS - 2)
    delta = np.arange(NA_ROWS)
    i = np.arange(NA_ROWS)
    ridx = i[None, :] - delta[:, None] + NA_ROWS - 1
    t = rel_bias[:, ridx]
    t = t[:, :, :, cidx]
    t = jnp.where(valid[None, None, None], t, MASK_VALUE)
    t = jnp.transpose(t, (1, 0, 3, 2, 4))
    return t.reshape(NA_ROWS, rel_bias.shape[0], GRID_W, NA_ROWS * GRID_W).astype(F32)


def _neighborhood_attention(q, k, v, kct, vc, bias_tab):
    b, h, s, d = q.shape
    l = kct.shape[-1]
    rows = s // GRID_W
    assert rows >= NA_ROWS
    half = NA_ROWS // 2

    def bias_idx(bi, r):
        return (r - jnp.clip(r - half, 0, rows - NA_ROWS), 0, 0, 0)

    return pl.pallas_call(
        functools.partial(_na_kernel, rows=rows),
        out_shape=jax.ShapeDtypeStruct((b, h, s, d), F32),
        grid=(b, rows),
        in_specs=[pl.BlockSpec((1, h, GRID_W, d), lambda bi, r: (bi, 0, r, 0)),
                  pl.BlockSpec((1, h, s, d), lambda bi, r: (bi, 0, 0, 0)),
                  pl.BlockSpec((1, h, s, d), lambda bi, r: (bi, 0, 0, 0)),
                  pl.BlockSpec((1, h, d, l), lambda bi, r: (bi, 0, 0, 0)),
                  pl.BlockSpec((1, h, l, d), lambda bi, r: (bi, 0, 0, 0)),
                  pl.BlockSpec((1, h, GRID_W, NA_ROWS * GRID_W), bias_idx)],
        out_specs=pl.BlockSpec((1, h, GRID_W, d), lambda bi, r: (bi, 0, r, 0)),
        compiler_params=_cparams(("arbitrary", "arbitrary")),
        name="neighborhood_attention",
    )(q, k, v, kct, vc, bias_tab)


SCAN_BATCH = SUBLANES
N_KQ = 5
FWD_W = (N_KQ + 1) * GROUP_W
BWD_W = (N_KQ - 1) * GROUP_W
V_SPLIT = 4
V_TILES = HEAD_DIM // (V_SPLIT * SUBLANES)
KT_BUFFERS = 4


def _scan_kernel(zf_ref, zb_ref, zrv_ref, s0_ref, yf_ref, yb_ref, sfin_ref, st_ref, sa_ref, *kt_refs, steps):
    i = pl.program_id(1)

    @pl.when(i == 0)
    def _():
        st_ref[...] = s0_ref[0]

    lane = lax.broadcasted_iota(jnp.int32, (SUBLANES, LANES), 1)
    lane_vq = lane // (LANES // V_SPLIT)
    chan_vq = (lane // (V_TILES * SUBLANES)) % V_SPLIT

    def operand_tile(q, s):
        sb = steps - 1 - s
        pieces = []
        for hp in range(2):
            pieces.append(zf_ref[s, :, pl.ds(q * GROUP_W + hp * LANES, LANES)])
        for hp in range(2):
            if q < N_KQ - 1:
                pieces.append(zb_ref[sb, :, pl.ds(q * GROUP_W + hp * LANES, LANES)])
            else:
                pieces.append(zrv_ref[sb, :, pl.ds((q - (N_KQ - 1)) * GROUP_W + hp * LANES, LANES)])
        tile = jnp.concatenate(pieces * V_SPLIT, axis=0)
        return tile.T

    def prepare(kt_ref, s):
        for q in range(N_KQ + 1):
            kt_ref[q] = operand_tile(q, s)

    nacc = 4

    def tree(parts):
        return (parts[0] + parts[1]) + (parts[2] + parts[3])

    prepare(kt_refs[0], 0)
    prepare(kt_refs[1], 1)
    for hpar in range(2):
        for vb in range(V_TILES):
            parts = [None] * nacc
            for k in range(HEAD_DIM):
                term = st_ref[hpar, k, vb] * kt_refs[0][0, pl.ds(hpar * HEAD_DIM + k, 1), :]
                parts[k % nacc] = term if parts[k % nacc] is None else parts[k % nacc] + term
            sa_ref[hpar, vb] = -tree(parts)

    def step(s, kt_ref, nxt_ref, far_ref):
        prepare(far_ref, jnp.minimum(s + 2, steps - 1))
        vt = kt_ref[N_KQ]
        ys = []
        for hpar in range(2):
            base = hpar * HEAD_DIM
            vops = []
            for vb in range(V_TILES):
                acc = None
                for vq in range(V_SPLIT):
                    r0 = base + vq * V_TILES * SUBLANES + vb * SUBLANES
                    piece = vt[r0:r0 + SUBLANES, :]
                    acc = piece if acc is None else jnp.where(lane_vq == vq, piece, acc)
                vops.append(acc)
            sas = [sa_ref[hpar, vb] for vb in range(V_TILES)]
            yp = [[None] * nacc for _ in range(V_TILES)]
            sp = [[None] * nacc for _ in range(V_TILES)]
            for k in range(HEAD_DIM):
                row = pl.ds(base + k, 1)
                w = kt_ref[1, row, :]
                ka = kt_ref[2, row, :]
                kd = kt_ref[3, row, :]
                rr = kt_ref[4, row, :]
                kn = nxt_ref[0, row, :]
                j = k % nacc
                for vb in range(V_TILES):
                    sn = st_ref[hpar, k, vb] * w + sas[vb] * ka + vops[vb] * kd
                    st_ref[hpar, k, vb] = sn
                    ty = sn * rr
                    yp[vb][j] = ty if yp[vb][j] is None else yp[vb][j] + ty
                    tn = sn * kn
                    sp[vb][j] = tn if sp[vb][j] is None else sp[vb][j] + tn
            for vb in range(V_TILES):
                sa_ref[hpar, vb] = -tree(sp[vb])
            ys.append([tree(yp[vb]) for vb in range(V_TILES)])
        ytile = jnp.concatenate([ys[hpar][vb] for hpar in range(2) for _ in range(V_SPLIT) for vb in range(V_TILES)],
                                axis=0)
        yt = ytile.T
        sb = steps - 1 - s
        for d in range(2):
            for hp in range(2):
                acc = None
                for vq in range(V_SPLIT):
                    r0 = vq * (LANES // V_SPLIT) + d * 2 * SUBLANES + hp * SUBLANES
                    piece = yt[r0:r0 + SUBLANES, :]
                    acc = piece if acc is None else jnp.where(chan_vq == vq, piece, acc)
                if d == 0:
                    yf_ref[s, :, pl.ds(hp * LANES, LANES)] = acc
                else:
                    yb_ref[sb, :, pl.ds(hp * LANES, LANES)] = acc

    nbuf = len(kt_refs)

    def body(j, carry):
        for u in range(nbuf):
            step(nbuf * j + u, kt_refs[u], kt_refs[(u + 1) % nbuf], kt_refs[(u + 2) % nbuf])
        return carry

    lax.fori_loop(0, steps // nbuf, body, 0)

    @pl.when(i == pl.num_programs(1) - 1)
    def _():
        sfin_ref[0] = st_ref[...]


def _rwkv_scan(zf, zb, s0):
    t_len, b, _ = zf.shape
    groups = b // SCAN_BATCH
    ts = SCAN_STEPS
    nblk = t_len // ts
    st_shape = (2, HEAD_DIM, V_TILES, SUBLANES, LANES)
    return pl.pallas_call(
        functools.partial(_scan_kernel, steps=ts),
        out_shape=(jax.ShapeDtypeStruct((t_len, b, GROUP_W), F32),
                   jax.ShapeDtypeStruct((t_len, b, GROUP_W), F32),
                   jax.ShapeDtypeStruct((groups,) + st_shape, F32)),
        grid=(groups, nblk),
        in_specs=[pl.BlockSpec((ts, SCAN_BATCH, FWD_W), lambda g, i: (i, g, 0)),
                  pl.BlockSpec((ts, SCAN_BATCH, BWD_W), lambda g, i: (nblk - 1 - i, g, 0)),
                  pl.BlockSpec((ts, SCAN_BATCH, 2 * GROUP_W), lambda g, i: (nblk - 1 - i, g, (N_KQ - 1) // 2)),
                  pl.BlockSpec((1,) + st_shape, lambda g, i: (g, 0, 0, 0, 0, 0))],
        out_specs=(pl.BlockSpec((ts, SCAN_BATCH, GROUP_W), lambda g, i: (i, g, 0)),
                   pl.BlockSpec((ts, SCAN_BATCH, GROUP_W), lambda g, i: (nblk - 1 - i, g, 0)),
                   pl.BlockSpec((1,) + st_shape, lambda g, i: (g, 0, 0, 0, 0, 0))),
        scratch_shapes=[pltpu.VMEM(st_shape, F32), pltpu.VMEM((2, V_TILES, SUBLANES, LANES), F32)]
        + [pltpu.VMEM((N_KQ + 1, LANES, LANES), F32)] * KT_BUFFERS,
        compiler_params=_cparams(("arbitrary", "arbitrary")),
        name="rwkv_scan",
    )(zf, zb, zf, s0)


PREP_STEPS = 32
DECAY_RATE = math.exp(-0.5)


def _prep_kernel(u_ref, up_ref, un_ref, mu_ref, w0_ref, wup_ref, a0_ref, aup_ref, gup_ref, kk_ref, ka_ref, rk_ref,
                 seg_ref, zf_ref, zb_ref, gb_ref):
    i = pl.program_id(1)
    last = pl.num_programs(1) - 1
    tb = u_ref.shape[0]
    u = u_ref[...]
    before = jnp.where(i > 0, up_ref[...], 0.0)
    after = jnp.where(i < last, un_ref[...], 0.0)
    prev = jnp.concatenate([before, u[:-1]], axis=0)
    nxt = jnp.concatenate([u[1:], after], axis=0)
    mu = mu_ref[...]
    x = (u + mu[0:1] * (prev - u) + mu[1:2] * (nxt - u)).reshape(tb * SCAN_BATCH, A_IN)
    gw = GROUP_W
    r = x[:, 0:gw]
    k = x[:, gw:2 * gw]
    v = x[:, 2 * gw:3 * gw]
    wd = x[:, 3 * gw:3 * gw + 2 * LORA_W]
    ad = x[:, 3 * gw + 2 * LORA_W:3 * gw + 2 * LORA_W + 2 * LORA_A]
    gd = x[:, 3 * gw + 2 * LORA_W + 2 * LORA_A:]
    seg = seg_ref[...]
    w_raw = w0_ref[...] + jnp.dot(jnp.tanh(wd).astype(BF16), wup_ref[...].astype(BF16), preferred_element_type=F32)
    decay = jnp.exp(-DECAY_RATE * jax.nn.sigmoid(w_raw))
    a = jax.nn.sigmoid(a0_ref[...] + jnp.dot(ad.astype(BF16), aup_ref[...].astype(BF16),
                                             preferred_element_type=F32))
    g = jnp.dot(jax.nn.sigmoid(gd).astype(BF16), gup_ref[...].astype(BF16), preferred_element_type=F32)
    bonus = jnp.zeros_like(r)
    for d, z_ref in enumerate((zf_ref, zb_ref)):
        sl = slice(d * gw, (d + 1) * gw)
        kk = k * kk_ref[:, sl]
        a_d = a[:, sl]
        kd = k * (1.0 + (a_d - 1.0) * ka_ref[:, sl])
        kk = kk * lax.rsqrt(_head_sums(kk * kk, seg) + 1e-12)
        bonus = bonus + _head_sums(r * kd * rk_ref[:, sl], seg) * v
        for q, val in enumerate((kk, decay[:, sl], kk * a_d, kd)):
            z_ref[:, :, q * gw:(q + 1) * gw] = val.reshape(tb, SCAN_BATCH, gw)
    zf_ref[:, :, 4 * gw:5 * gw] = r.reshape(tb, SCAN_BATCH, gw)
    zf_ref[:, :, 5 * gw:6 * gw] = v.reshape(tb, SCAN_BATCH, gw)
    gb_ref[:, :, 0:gw] = g.reshape(tb, SCAN_BATCH, gw)
    gb_ref[:, :, gw:2 * gw] = bonus.reshape(tb, SCAN_BATCH, gw)


def _block_diag(w):
    z = jnp.zeros_like(w[0])
    return jnp.concatenate([jnp.concatenate([w[0], z], axis=1), jnp.concatenate([z, w[1]], axis=1)], axis=0)


def _rwkv_prepare(ua_tm, mu, w0, w_up, a0, a_up, g_up, k_k, k_a, r_k):
    t, b, _ = ua_tm.shape
    tb = min(PREP_STEPS, t)
    nblk = t // tb
    gw2 = 2 * GROUP_W

    def full(shape):
        return pl.BlockSpec(shape, lambda g, i: (0,) * len(shape))

    return pl.pallas_call(
        _prep_kernel,
        out_shape=(jax.ShapeDtypeStruct((t, b, FWD_W), F32), jax.ShapeDtypeStruct((t, b, BWD_W), F32),
                   jax.ShapeDtypeStruct((t, b, gw2), F32)),
        grid=(b // SCAN_BATCH, nblk),
        in_specs=[pl.BlockSpec((tb, SCAN_BATCH, A_IN), lambda g, i: (i, g, 0)),
                  pl.BlockSpec((1, SCAN_BATCH, A_IN), lambda g, i: (jnp.maximum(i * tb - 1, 0), g, 0)),
                  pl.BlockSpec((1, SCAN_BATCH, A_IN), lambda g, i: (jnp.minimum((i + 1) * tb, t - 1), g, 0)),
                  full((2, A_IN)), full((1, gw2)), full((2 * LORA_W, gw2)), full((1, gw2)), full((2 * LORA_A, gw2)),
                  full((LORA_G, GROUP_W)), full((1, gw2)), full((1, gw2)), full((1, gw2)),
                  full((GROUP_W, GROUP_W))],
        out_specs=(pl.BlockSpec((tb, SCAN_BATCH, FWD_W), lambda g, i: (i, g, 0)),
                   pl.BlockSpec((tb, SCAN_BATCH, BWD_W), lambda g, i: (i, g, 0)),
                   pl.BlockSpec((tb, SCAN_BATCH, gw2), lambda g, i: (i, g, 0))),
        compiler_params=_cparams(("arbitrary", "arbitrary")),
        name="rwkv_prepare",
    )(ua_tm, ua_tm, ua_tm, mu, w0.reshape(1, gw2), _block_diag(w_up), a0.reshape(1, gw2), _block_diag(a_up), g_up,
      k_k.reshape(1, gw2), k_a.reshape(1, gw2), r_k.reshape(1, gw2), _head_segments())


def _rwkv_time_mix(ua_tm, s0, mu, w0, w_up, a0, a_up, g_up, k_k, k_a, r_k):
    t, b, _ = ua_tm.shape
    h, n = GROUP_HEADS, HEAD_DIM
    bp = -(-b // SCAN_BATCH) * SCAN_BATCH
    groups = bp // SCAN_BATCH
    if bp != b:
        ua_tm = jnp.pad(ua_tm, ((0, 0), (0, bp - b), (0, 0)))
    zf, zb, gb = _rwkv_prepare(ua_tm, mu, w0, w_up, a0, a_up, g_up, k_k, k_a, r_k)
    if s0 is None:
        s0l = jnp.zeros((groups, 2, n, V_TILES, SUBLANES, LANES), F32)
    else:
        s0 = jnp.pad(s0.astype(F32), ((0, bp - b),) + ((0, 0),) * 4)
        s0l = s0.reshape(groups, SCAN_BATCH, 2, 2, 2, V_SPLIT, V_TILES, SUBLANES, n)
        s0l = jnp.transpose(s0l, (0, 4, 8, 6, 7, 5, 2, 3, 1)).reshape(groups, 2, n, V_TILES, SUBLANES, LANES)
    yf, yb, s_fin = _rwkv_scan(zf, zb, s0l)
    s_fin = s_fin.reshape(groups, 2, n, V_TILES, SUBLANES, V_SPLIT, 2, 2, SCAN_BATCH)
    s_fin = jnp.transpose(s_fin, (0, 8, 6, 7, 1, 5, 3, 4, 2)).reshape(bp, 2, h, n, n)[:b]
    return yf, yb, gb, s_fin


def _expert_kernel(be_ref, nu_ref, x_ref, w1_ref, w3_ref, w2_ref, o_ref, w1s, w3s, w2s):
    i = pl.program_id(0)
    e = be_ref[i]
    prev = be_ref[jnp.maximum(i - 1, 0)]

    @pl.when((i == 0) | (e != prev))
    def _():
        w1s[...] = w1_ref[0].astype(BF16)
        w3s[...] = w3_ref[0].astype(BF16)
        w2s[...] = w2_ref[0].astype(BF16)

    @pl.when(i < nu_ref[0])
    def _():
        x = x_ref[...]
        a = jnp.dot(x, w1s[...], preferred_element_type=F32)
        g = jnp.dot(x, w3s[...], preferred_element_type=F32)
        hmid = (a * jax.nn.sigmoid(a)) * g
        o_ref[...] = jnp.dot(hmid.astype(BF16), w2s[...], preferred_element_type=F32)

    @pl.when(i >= nu_ref[0])
    def _():
        o_ref[...] = jnp.zeros_like(o_ref)


def _expert_mlp(xb, block_e, n_used, w1, w3, w2):
    cap, d = xb.shape
    bm = MOE_TILE
    de = w1.shape[-1]
    return pl.pallas_call(
        _expert_kernel,
        out_shape=jax.ShapeDtypeStruct((cap, d), F32),
        grid_spec=pltpu.PrefetchScalarGridSpec(
            num_scalar_prefetch=2,
            grid=(cap // bm,),
            in_specs=[pl.BlockSpec((bm, d), lambda i, be, nu: (i, 0)),
                      pl.BlockSpec((1, d, de), lambda i, be, nu: (be[i], 0, 0)),
                      pl.BlockSpec((1, d, de), lambda i, be, nu: (be[i], 0, 0)),
                      pl.BlockSpec((1, de, d), lambda i, be, nu: (be[i], 0, 0))],
            out_specs=pl.BlockSpec((bm, d), lambda i, be, nu: (i, 0)),
            scratch_shapes=[pltpu.VMEM((d, de), BF16), pltpu.VMEM((d, de), BF16), pltpu.VMEM((de, d), BF16)]),
        compiler_params=_cparams(("arbitrary",)),
        name="expert_mlp",
    )(block_e, n_used, xb, w1, w3, w2)


def _hier_moe(h_bf, logits, w1, w3, w2):
    n, d = h_bf.shape
    bm = MOE_TILE
    g_logits = logits[:, :N_EXPERT_GROUPS]
    g_idx = jnp.argmax(g_logits, axis=-1)
    g_top = jnp.max(jax.nn.softmax(g_logits, axis=-1), axis=-1)
    e_logits = logits[:, N_EXPERT_GROUPS:N_EXPERT_GROUPS + N_EXPERTS].reshape(n, N_EXPERT_GROUPS, EXPERTS_PER_GROUP)
    e_logits = jnp.take_along_axis(e_logits, g_idx[:, None, None], axis=1)[:, 0]
    top_v, top_i = lax.top_k(e_logits, TOP_K)
    gate = jax.nn.softmax(top_v, axis=-1) * g_top[:, None]
    eid = (g_idx[:, None] * EXPERTS_PER_GROUP + top_i).reshape(-1).astype(jnp.int32)
    tok = jnp.repeat(jnp.arange(n, dtype=jnp.int32), TOP_K)
    n_assign = n * TOP_K
    onehot = (eid[:, None] == jnp.arange(N_EXPERTS, dtype=jnp.int32)[None, :]).astype(jnp.int32)
    csum = jnp.cumsum(onehot, axis=0)
    counts = csum[-1]
    rank = jnp.take_along_axis(csum, eid[:, None], axis=1)[:, 0] - 1
    padded = (counts + bm - 1) // bm * bm
    pad_end = jnp.cumsum(padded)
    pad_start = pad_end - padded
    dest = pad_start[eid] + rank
    n_blocks = -(-(n_assign + N_EXPERTS * (bm - 1)) // bm)
    cap = n_blocks * bm
    buf_tok = jnp.zeros((cap,), jnp.int32).at[dest].set(tok)
    block_start = jnp.arange(n_blocks, dtype=jnp.int32) * bm
    block_e = jnp.minimum(jnp.sum((pad_end[None, :] <= block_start[:, None]).astype(jnp.int32), axis=1),
                          N_EXPERTS - 1).astype(jnp.int32)
    n_used = (pad_end[-1:] // bm).astype(jnp.int32)
    xb = h_bf[buf_tok]
    yb = _expert_mlp(xb, block_e, n_used, w1, w3, w2)
    d2 = dest.reshape(n, TOP_K)
    return yb[d2[:, 0]] * gate[:, 0:1] + yb[d2[:, 1]] * gate[:, 1:2]


def _rms(x, g):
    return x * lax.rsqrt(jnp.mean(x * x, axis=-1, keepdims=True) + NORM_EPS) * g


def _axial_rope(t_len, dim):
    q4 = dim // 4
    inv = ROPE_THETA ** (-jnp.arange(q4, dtype=F32) / q4)
    t = jnp.arange(t_len)
    row = (t // GRID_W).astype(F32)
    col = (t % GRID_W).astype(F32)
    ang = jnp.stack([row[:, None] * inv, col[:, None] * inv], axis=1)
    return jnp.cos(ang), jnp.sin(ang)


def _rope_tokens(x, cs):
    cos, sin = cs
    lead = x.ndim - 3
    shape = (1, cos.shape[0]) + (1,) * lead + cos.shape[1:]
    cos = cos.reshape(shape)
    sin = sin.reshape(shape)
    xf = x.reshape(*x.shape[:-1], 2, 2, x.shape[-1] // 4)
    x1 = xf[..., 0, :]
    x2 = xf[..., 1, :]
    out = jnp.stack([x1 * cos - x2 * sin, x2 * cos + x1 * sin], axis=-2)
    return out.reshape(x.shape)


def _token_mixers(ua_tm, ub, uc, ud, rope, ctx, l, W):
    b, t, _ = ub.shape
    nh, n = GROUP_HEADS, HEAD_DIM
    latent = ctx is not None

    yf, yb, gb, s_fin = _rwkv_time_mix(ua_tm, ctx[0] if latent else None, W['rw_shift'][l], W['rw_w0'][l],
                                       W['rw_w_up'][l], W['rw_a0'][l], W['rw_a_up'][l], W['rw_g_up'][l],
                                       W['rw_k_k'][l], W['rw_k_a'][l], W['rw_r_k'][l])

    qb, kb, vb = jnp.split(ub, 3, axis=-1)
    qb = _rms(qb.reshape(b, t, nh, 2, DIFF_DQK), W['diff_qk_norm'][l, 0])
    kb = _rms(kb.reshape(b, t, nh, 2, DIFF_DQK), W['diff_qk_norm'][l, 1])
    if latent:
        qb = _rope_tokens(qb, rope[0])
        k_ctx = jnp.transpose(ctx[1], (0, 3, 1, 2, 4)).reshape(b, -1, GROUP_W)
        v_ctx = jnp.transpose(ctx[2], (0, 2, 1, 3)).reshape(b, -1, GROUP_W)
        kb_all = jnp.concatenate([k_ctx, _rope_tokens(kb, rope[0]).reshape(b, t, GROUP_W)], axis=1)
        vb_all = jnp.concatenate([v_ctx, vb], axis=1)
    else:
        kb_all, vb_all = kb.reshape(b, t, GROUP_W), vb
    lam_init = 0.8 - 0.6 * math.exp(-0.3 * l)
    lv = W['diff_lambda'][l].astype(F32)
    lam = jnp.exp(jnp.sum(lv[0] * lv[1])) - jnp.exp(jnp.sum(lv[2] * lv[3])) + lam_init
    scalars = jnp.stack([lam, jnp.asarray(1.0 - lam_init, F32)]).astype(F32)
    out_b = _flash_attention((qb * (LOG2E / math.sqrt(DIFF_DQK))).reshape(b, t, GROUP_W).astype(BF16),
                             kb_all.astype(BF16), jnp.swapaxes(vb_all, 1, 2).astype(BF16),
                             DIFF_HEADS, scalars, W['diff_subln'][l])

    qc, kc, vc = jnp.split(uc, 3, axis=-1)
    qc = _rms(qc.reshape(b, t, nh, n), W['na_qk_norm'][l, 0])
    kc = _rms(kc.reshape(b, t, nh, n), W['na_qk_norm'][l, 1])
    if latent:
        vch = vc.reshape(b, t, nh, n).transpose(0, 2, 1, 3)
        oc = _neighborhood_attention((qc * (1.0 / math.sqrt(n))).transpose(0, 2, 1, 3).astype(BF16),
                                     kc.transpose(0, 2, 1, 3).astype(BF16), vch.astype(BF16),
                                     jnp.swapaxes(ctx[3], -1, -2).astype(BF16), ctx[4].astype(BF16),
                                     _na_bias_table(W['na_rel_bias'][l]))
        out_c = oc.transpose(0, 2, 1, 3).reshape(b, t, GROUP_W)
    else:
        out_c = _flash_attention((qc * (LOG2E / math.sqrt(n))).reshape(b, t, GROUP_W).astype(BF16),
                                 kc.reshape(b, t, GROUP_W).astype(BF16), jnp.swapaxes(vc, 1, 2).astype(BF16),
                                 DENSE_HEADS)

    kvw = GQA_KV_HEADS * n
    qd, kd, vd = jnp.split(ud, [GROUP_W, GROUP_W + kvw], axis=-1)
    qd = _rms(qd.reshape(b, t, nh, n), W['gqa_qk_norm'][l, 0])
    kd = _rms(kd.reshape(b, t, GQA_KV_HEADS, n), W['gqa_qk_norm'][l, 1])
    if latent:
        qd = _rope_tokens(qd, rope[1])
        k_ctx = jnp.transpose(ctx[5], (0, 2, 1, 3)).reshape(b, -1, kvw)
        v_ctx = jnp.transpose(ctx[6], (0, 2, 1, 3)).reshape(b, -1, kvw)
        kd_all = jnp.concatenate([k_ctx, _rope_tokens(kd, rope[1]).reshape(b, t, kvw)], axis=1)
        vd_all = jnp.concatenate([v_ctx, vd], axis=1)
    else:
        kd_all, vd_all = kd.reshape(b, t, kvw), vd
    out_d = _flash_attention((qd * (LOG2E / math.sqrt(n))).reshape(b, t, GROUP_W).astype(BF16),
                             kd_all.astype(BF16), jnp.swapaxes(vd_all, 1, 2).astype(BF16), GQA_HEADS)

    mix = (yf, yb, gb) + tuple(z.reshape(b * t, GROUP_W) for z in (out_b, out_c, out_d))
    if latent:
        return mix, None
    new_ctx = (s_fin, jnp.transpose(kb, (0, 2, 3, 1, 4)), vb.reshape(b, t, nh, n).transpose(0, 2, 1, 3),
               kc.transpose(0, 2, 1, 3), vc.reshape(b, t, nh, n).transpose(0, 2, 1, 3),
               kd.transpose(0, 2, 1, 3), vd.reshape(b, t, GQA_KV_HEADS, n).transpose(0, 2, 1, 3))
    return mix, new_ctx


def _mixing_sublayer(x, cond, rope, ctx, l, W):
    b, t, d = x.shape
    n = b * t
    mod = _modulation(cond, W['w_mod'][l], W['b_mod'][l])
    sh1, sc1, g1, sh2, sc2, g2 = jnp.split(mod, 6, axis=-1)
    x2 = x.reshape(n, d)
    ua, ub, uc, ud = _in_projection(x2, W['norm_mix'][l], 1.0 + sc1, sh1, W['w_in'][l], t)
    mix, new_ctx = _token_mixers(ua, ub.reshape(b, t, -1), uc.reshape(b, t, -1), ud.reshape(b, t, -1), rope, ctx, l, W)
    w_router = jnp.concatenate([W['moe_w_group'][l], W['moe_w_router'][l]], axis=1)
    w_router = jnp.pad(w_router, ((0, 0), (0, ROUTER_W - w_router.shape[1])))
    b_router = jnp.concatenate([W['moe_b_group'][l], W['moe_b_router'][l]])
    b_router = jnp.pad(b_router, (0, ROUTER_W - b_router.shape[0])).reshape(1, ROUTER_W)
    x2, h2, logits = _out_projection(*mix, W['rw_gn_w'][l], W['rw_gn_b'][l], W['w_out'][l], x2, g1,
                                     W['norm_ffn'][l], 1.0 + sc2, sh2, w_router, b_router, t)
    g2r = jnp.broadcast_to(g2[:, None, :], (g2.shape[0], n // g2.shape[0], d)).reshape(n, d)
    return x2, h2, logits, g2r, new_ctx


def kernel(x_prompt, x_sample, c, state_rwkv, cache_diff_k, cache_diff_v, cache_na_k, cache_na_v,
           cache_gqa_k, cache_gqa_v, c_ctx, norm_mix, norm_ffn, w_mod, b_mod, w_in, w_out,
           rw_shift, rw_w0, rw_w_up, rw_a0, rw_a_up, rw_g_up, rw_k_k, rw_k_a, rw_r_k, rw_gn_w, rw_gn_b,
           diff_qk_norm, diff_lambda, diff_subln, na_qk_norm, na_rel_bias, gqa_qk_norm,
           moe_w_group, moe_b_group, moe_w_router, moe_b_router, moe_w1, moe_w3, moe_w2):
    W = {
        'norm_mix': norm_mix, 'norm_ffn': norm_ffn, 'w_mod': w_mod, 'b_mod': b_mod,
        'w_in': w_in, 'w_out': w_out, 'rw_shift': rw_shift, 'rw_w0': rw_w0, 'rw_w_up': rw_w_up,
        'rw_a0': rw_a0, 'rw_a_up': rw_a_up, 'rw_g_up': rw_g_up, 'rw_k_k': rw_k_k, 'rw_k_a': rw_k_a,
        'rw_r_k': rw_r_k, 'rw_gn_w': rw_gn_w, 'rw_gn_b': rw_gn_b, 'diff_qk_norm': diff_qk_norm,
        'diff_lambda': diff_lambda, 'diff_subln': diff_subln, 'na_qk_norm': na_qk_norm,
        'na_rel_bias': na_rel_bias, 'gqa_qk_norm': gqa_qk_norm, 'moe_w_group': moe_w_group,
        'moe_b_group': moe_b_group, 'moe_w_router': moe_w_router, 'moe_b_router': moe_b_router,
        'moe_w1': moe_w1, 'moe_w3': moe_w3, 'moe_w2': moe_w2,
    }
    depth = w_in.shape[0]

    xp = x_prompt
    xs = x_sample
    cond_ctx = c_ctx[None, :]
    t_lat = x_sample.shape[1]
    rope = (_axial_rope(t_lat, DIFF_DQK), _axial_rope(t_lat, HEAD_DIM))
    ctx_layers = []
    for l in range(depth):
        rp, hp, lgp, gp, new_ctx = _mixing_sublayer(xp, cond_ctx, None, None, l, W)
        ctx_layers.append(new_ctx)
        ctx = (state_rwkv[:, l], cache_diff_k[:, l], cache_diff_v[:, l], cache_na_k[:, l],
               cache_na_v[:, l], cache_gqa_k[:, l], cache_gqa_v[:, l])
        rs, hs, lgs, gs, _ = _mixing_sublayer(xs, c, rope, ctx, l, W)
        y = _hier_moe(jnp.concatenate([hp, hs], axis=0), jnp.concatenate([lgp, lgs], axis=0),
                      W['moe_w1'][l], W['moe_w3'][l], W['moe_w2'][l])
        npr = rp.shape[0]
        xp = (rp + gp * y[:npr]).reshape(xp.shape)
        xs = (rs + gs * y[npr:]).reshape(xs.shape)
    new_caches = tuple(jnp.stack([z[i] for z in ctx_layers], axis=1) for i in range(7))
    return (xp, xs) + new_caches
```

```python
import functools
import math

import numpy as np
import jax
import jax.numpy as jnp
from jax import lax
from jax.experimental import pallas as pl
from jax.experimental.pallas import tpu as pltpu

F32 = jnp.float32
BF16 = jnp.bfloat16

D_MODEL = 1024
GRID_W = 64
HEAD_DIM = 64
GROUP_W = 256
GROUP_HEADS = 4
LORA_W = 64
LORA_A = 64
LORA_G = 128
RWKV_GN_EPS = 64e-5
DIFF_DQK = 32
GQA_KV_HEADS = 2
GQA_GROUP = 2
NA_ROWS = 8
NA_COLS = 16
ROPE_THETA = 10000.0
NORM_EPS = 1e-6
N_EXPERT_GROUPS = 4
EXPERTS_PER_GROUP = 8
N_EXPERTS = 32
TOP_K = 2
D_EXPERT = 512
A_IN = 3 * GROUP_W + 2 * LORA_W + 2 * LORA_A + LORA_G
B_IN = 3 * GROUP_W
C_IN = 3 * GROUP_W
D_IN = GROUP_W + 2 * GQA_KV_HEADS * HEAD_DIM
IN_W = A_IN + B_IN + C_IN + D_IN

LANES = 128
SUBLANES = 8
VMEM_LIMIT = 48 * 1024 * 1024

ROW_TILE = 256
Q_TILE = 256
SCAN_STEPS = 16
KV_CHUNK = 512
MOE_TILE = 256
ROUTER_W = 128
MASK_VALUE = -1e30
LOG2E = 1.4426950408889634


def _cparams(sem):
    return pltpu.CompilerParams(dimension_semantics=sem, vmem_limit_bytes=VMEM_LIMIT)


def _mod_kernel(c_ref, w_ref, b_ref, o_ref):
    c = c_ref[...]
    a = c * jax.nn.sigmoid(c)
    o_ref[...] = jnp.dot(a, w_ref[...], preferred_element_type=F32,
                         precision=lax.Precision.HIGHEST) + b_ref[...]


def _modulation(cond, w, b):
    m, d = cond.shape
    n = w.shape[1]
    mp = -(-m // SUBLANES) * SUBLANES
    cp = jnp.pad(cond, ((0, mp - m), (0, 0)))
    tn = 768
    out = pl.pallas_call(
        _mod_kernel,
        out_shape=jax.ShapeDtypeStruct((mp, n), F32),
        grid=(n // tn,),
        in_specs=[pl.BlockSpec((mp, d), lambda j: (0, 0)),
                  pl.BlockSpec((d, tn), lambda j: (0, j)),
                  pl.BlockSpec((1, tn), lambda j: (0, j))],
        out_specs=pl.BlockSpec((mp, tn), lambda j: (0, j)),
        compiler_params=_cparams(("arbitrary",)),
        name="modulation",
    )(cp, w, b.reshape(1, n))
    return out[:m]


def _inproj_kernel(x_ref, g_ref, sc_ref, sh_ref, w_ref, oa_ref, ob_ref, oc_ref, od_ref):
    x = x_ref[...]
    ms = jnp.mean(x * x, axis=-1, keepdims=True)
    h = x * lax.rsqrt(ms + NORM_EPS) * g_ref[...]
    h = h * sc_ref[0] + sh_ref[0]
    u = jnp.dot(h.astype(BF16), w_ref[...].astype(BF16), preferred_element_type=F32)
    oa_ref[...] = u[:, :A_IN]
    ob_ref[...] = u[:, A_IN:A_IN + B_IN]
    oc_ref[...] = u[:, A_IN + B_IN:A_IN + B_IN + C_IN]
    od_ref[...] = u[:, A_IN + B_IN + C_IN:]


def _in_projection(x2, gain, scale1p, shift, w_in, t_len):
    n, d = x2.shape
    bm = scale1p.shape[0]
    tm = ROW_TILE
    per_seq = t_len // tm
    b = n // t_len

    def mod_idx(i):
        return ((i // per_seq) if bm > 1 else 0, 0, 0)

    widths = (B_IN, C_IN, D_IN)
    ua, ub, uc, ud = pl.pallas_call(
        _inproj_kernel,
        out_shape=(jax.ShapeDtypeStruct((t_len, b * A_IN), F32),)
        + tuple(jax.ShapeDtypeStruct((n, w), F32) for w in widths),
        grid=(n // tm,),
        in_specs=[pl.BlockSpec((tm, d), lambda i: (i, 0)),
                  pl.BlockSpec((1, d), lambda i: (0, 0)),
                  pl.BlockSpec((1, 1, d), mod_idx),
                  pl.BlockSpec((1, 1, d), mod_idx),
                  pl.BlockSpec((d, IN_W), lambda i: (0, 0))],
        out_specs=(pl.BlockSpec((tm, A_IN), lambda i: (i % per_seq, i // per_seq)),)
        + tuple(pl.BlockSpec((tm, w), lambda i: (i, 0)) for w in widths),
        compiler_params=_cparams(("arbitrary",)),
        name="in_projection",
    )(x2, gain.reshape(1, d), scale1p.reshape(bm, 1, d), shift.reshape(bm, 1, d), w_in)
    return ua.reshape(t_len, b, A_IN), ub, uc, ud


def _head_sums(x, seg):
    hi = x.astype(BF16)
    lo = (x - hi.astype(F32)).astype(BF16)
    return (jnp.dot(hi, seg, preferred_element_type=F32) + jnp.dot(lo, seg, preferred_element_type=F32))


def _head_segments():
    idx = np.arange(GROUP_W) // HEAD_DIM
    return jnp.asarray(idx[:, None] == idx[None, :], BF16)


def _outproj_kernel(yf_ref, yb_ref, gb_ref, ob_ref, oc_ref, od_ref, gnw_ref, gnb_ref, seg_ref, w_ref, x_ref, g1_ref,
                    gn_ref, sc_ref, sh_ref, wr_ref, br_ref, xo_ref, h_ref, lg_ref):
    gw = GROUP_W
    o = yf_ref[...] + yb_ref[...]
    seg = seg_ref[...]
    mean = _head_sums(o, seg) * (1.0 / HEAD_DIM)
    cen = o - mean
    var = _head_sums(cen * cen, seg) * (1.0 / HEAD_DIM)
    oa = cen * lax.rsqrt(var + RWKV_GN_EPS) * gnw_ref[...] + gnb_ref[...]
    oa = (oa + gb_ref[:, gw:2 * gw]) * gb_ref[:, 0:gw]
    y = jnp.dot(oa.astype(BF16), w_ref[0:gw, :].astype(BF16), preferred_element_type=F32)
    for j, m_ref in enumerate((ob_ref, oc_ref, od_ref)):
        y = y + jnp.dot(m_ref[...].astype(BF16), w_ref[(j + 1) * gw:(j + 2) * gw, :].astype(BF16),
                        preferred_element_type=F32)
    xn = x_ref[...] + g1_ref[0] * y
    xo_ref[...] = xn
    ms = jnp.mean(xn * xn, axis=-1, keepdims=True)
    h = xn * lax.rsqrt(ms + NORM_EPS) * gn_ref[...]
    h = (h * sc_ref[0] + sh_ref[0]).astype(BF16)
    h_ref[...] = h
    lg_ref[...] = jnp.dot(h, wr_ref[...].astype(BF16), preferred_element_type=F32) + br_ref[...]


def _out_projection(yf, yb, gb, ob, oc, od, gn_w, gn_b, w_out, x2, gate1, gain, scale1p, shift, w_router, b_router,
                    t_len):
    n, d = x2.shape
    bm = gate1.shape[0]
    tm = ROW_TILE
    per_seq = t_len // tm
    gw = GROUP_W

    def mod_idx(i):
        return ((i // per_seq) if bm > 1 else 0, 0, 0)

    def tm_idx(i):
        return (i % per_seq, i // per_seq)

    def rows(w):
        return pl.BlockSpec((tm, w), lambda i: (i, 0))

    def full(shape):
        return pl.BlockSpec(shape, lambda i: (0,) * len(shape))

    mod_spec = pl.BlockSpec((1, 1, d), mod_idx)
    return pl.pallas_call(
        _outproj_kernel,
        out_shape=(jax.ShapeDtypeStruct((n, d), F32),
                   jax.ShapeDtypeStruct((n, d), BF16),
                   jax.ShapeDtypeStruct((n, ROUTER_W), F32)),
        grid=(n // tm,),
        in_specs=[pl.BlockSpec((tm, gw), tm_idx), pl.BlockSpec((tm, gw), tm_idx), pl.BlockSpec((tm, 2 * gw), tm_idx),
                  rows(gw), rows(gw), rows(gw), full((1, gw)), full((1, gw)), full((gw, gw)), full((d, d)), rows(d),
                  mod_spec, full((1, d)), mod_spec, mod_spec, full((d, ROUTER_W)), full((1, ROUTER_W))],
        out_specs=(rows(d), rows(d), rows(ROUTER_W)),
        compiler_params=_cparams(("arbitrary",)),
        name="out_projection",
    )(yf.reshape(t_len, -1), yb.reshape(t_len, -1), gb.reshape(t_len, -1), ob, oc, od,
      gn_w.reshape(1, gw), gn_b.reshape(1, gw), _head_segments(), w_out, x2, gate1.reshape(bm, 1, d),
      gain.reshape(1, d), scale1p.reshape(bm, 1, d), shift.reshape(bm, 1, d), w_router, b_router)


ACC_ROWS = HEAD_DIM + 16


def _flash_kernel(sc_ref, q_ref, *refs, heads, diff, chunk, has_ctx):
    if has_ctx:
        kc_ref, vtc_ref, k_ref, vt_ref, g_ref, o_ref, m_ref, acc_ref, sa_ref, sb_ref = refs
        n_ctx = kc_ref.shape[1] // chunk
    else:
        k_ref, vt_ref, g_ref, o_ref, m_ref, acc_ref, sa_ref, sb_ref = refs
        kc_ref = vtc_ref = None
        n_ctx = 0
    tq = q_ref.shape[1]
    kdim = k_ref.shape[2]
    n_lat = k_ref.shape[1] // chunk
    qt = q_ref[0].astype(F32).T.astype(BF16)

    def stationary(q_rows, k_row0):
        pieces = []
        if k_row0 > 0:
            pieces.append(jnp.zeros((k_row0, tq), BF16))
        pieces.append(q_rows)
        rest = kdim - k_row0 - q_rows.shape[0]
        if rest > 0:
            pieces.append(jnp.zeros((rest, tq), BF16))
        return jnp.concatenate(pieces, axis=0) if len(pieces) > 1 else pieces[0]

    streams = []
    for (maps, v_row0) in heads:
        for (qs, qn, ks) in maps:
            streams.append((stationary(qt[qs:qs + qn], ks), v_row0))
    m_ref[...] = jnp.full(m_ref.shape, MASK_VALUE, F32)
    acc_ref[...] = jnp.zeros(acc_ref.shape, F32)
    ones_rows = (lax.broadcasted_iota(jnp.int32, (ACC_ROWS - HEAD_DIM, chunk), 0) == 0).astype(BF16)

    def key_rows(src):
        kind, c = src
        if kind == "ctx":
            return kc_ref[0, c * chunk:(c + 1) * chunk, :]
        return k_ref[0, pl.ds(pl.multiple_of(c * chunk, chunk), chunk), :]

    def value_cols(src, v_row0):
        kind, c = src
        if kind == "ctx":
            return vtc_ref[0, v_row0:v_row0 + HEAD_DIM, c * chunk:(c + 1) * chunk]
        return vt_ref[0, pl.ds(v_row0, HEAD_DIM), pl.ds(pl.multiple_of(c * chunk, chunk), chunk)]

    def score(src, s_ref, j):
        s_ref[j] = jnp.dot(key_rows(src), streams[j][0], preferred_element_type=F32)

    def accumulate(src, s_ref, j):
        s = s_ref[j]
        m = m_ref[j]
        m_new = jnp.maximum(m, jnp.max(s, axis=0, keepdims=True))
        p = jnp.exp2(s - m_new).astype(BF16)
        alpha = jnp.exp2(m - m_new)
        m_ref[j] = m_new
        vc = jnp.concatenate([value_cols(src, streams[j][1]), ones_rows], axis=0)
        acc_ref[j] = alpha * acc_ref[j] + jnp.dot(vc, p, preferred_element_type=F32)

    ns = len(streams)

    def stage(nxt, next_ref, cur, cur_ref):
        for j in range(ns):
            if nxt is not None:
                score(nxt, next_ref, j)
            if cur is not None:
                accumulate(cur, cur_ref, j)

    bufs = (sa_ref, sb_ref)
    first = ("ctx", 0) if n_ctx else ("lat", 0)
    stage(first, bufs[0], None, None)
    for g in range(n_ctx):
        nxt = ("ctx", g + 1) if g + 1 < n_ctx else ("lat", 0)
        stage(nxt, bufs[(g + 1) % 2], ("ctx", g), bufs[g % 2])
    ra, rb = bufs[n_ctx % 2], bufs[(n_ctx + 1) % 2]

    def body(jj, carry):
        c = 2 * jj
        stage(("lat", c + 1), rb, ("lat", c), ra)
        stage(("lat", c + 2), ra, ("lat", c + 1), rb)
        return carry

    lax.fori_loop(0, (n_lat - 1) // 2, body, 0)
    if n_lat % 2 == 0:
        stage(("lat", n_lat - 1), rb, ("lat", n_lat - 2), ra)
        stage(None, None, ("lat", n_lat - 1), rb)
    else:
        stage(None, None, ("lat", n_lat - 1), ra)

    outs = []
    j = 0
    for (maps, v_row0) in heads:
        a1 = acc_ref[j]
        o = a1[:HEAD_DIM] / a1[HEAD_DIM:HEAD_DIM + 1]
        if diff:
            a2 = acc_ref[j + 1]
            o = o - (sc_ref[0] / a2[HEAD_DIM:HEAD_DIM + 1]) * a2[:HEAD_DIM]
            ms = jnp.mean(o * o, axis=0, keepdims=True)
            o = o * lax.rsqrt(ms + NORM_EPS) * g_ref[...] * sc_ref[1]
        j += len(maps)
        outs.append(o)
    o_ref[0] = jnp.concatenate(outs, axis=0).T


def _flash_attention(q, k, vt, heads, scalars=None, gain=None, k_ctx=None, vt_ctx=None):
    b, s, c = q.shape
    l, ck = k.shape[1:]
    cv = vt.shape[1]
    tq = min(Q_TILE, s)
    chunk = min(KV_CHUNK, l)
    assert l % chunk == 0 and s % tq == 0
    diff = scalars is not None
    has_ctx = k_ctx is not None
    ns = sum(len(maps) for maps, _ in heads)
    if not diff:
        scalars = jnp.zeros((2,), F32)
        gain = jnp.ones((HEAD_DIM,), F32)
    in_specs = [pl.BlockSpec((1, tq, c), lambda bi, i, sc: (bi, i, 0))]
    args = [q]
    if has_ctx:
        lc = k_ctx.shape[1]
        assert lc % chunk == 0
        in_specs += [pl.BlockSpec((1, lc, ck), lambda bi, i, sc: (bi, 0, 0)),
                     pl.BlockSpec((1, cv, lc), lambda bi, i, sc: (bi, 0, 0))]
        args += [k_ctx, vt_ctx]
    in_specs += [pl.BlockSpec((1, l, ck), lambda bi, i, sc: (bi, 0, 0)),
                 pl.BlockSpec((1, cv, l), lambda bi, i, sc: (bi, 0, 0)),
                 pl.BlockSpec((HEAD_DIM, 1), lambda bi, i, sc: (0, 0))]
    args += [k, vt, gain.reshape(HEAD_DIM, 1)]
    return pl.pallas_call(
        functools.partial(_flash_kernel, heads=heads, diff=diff, chunk=chunk, has_ctx=has_ctx),
        out_shape=jax.ShapeDtypeStruct((b, s, c), F32),
        grid_spec=pltpu.PrefetchScalarGridSpec(
            num_scalar_prefetch=1,
            grid=(b, s // tq),
            in_specs=in_specs,
            out_specs=pl.BlockSpec((1, tq, c), lambda bi, i, sc: (bi, i, 0)),
            scratch_shapes=[pltpu.VMEM((ns, 1, tq), F32), pltpu.VMEM((ns, ACC_ROWS, tq), F32),
                            pltpu.VMEM((ns, chunk, tq), F32), pltpu.VMEM((ns, chunk, tq), F32)]),
        compiler_params=_cparams(("arbitrary", "arbitrary")),
        name="flash_attention",
    )(scalars, *args)


DIFF_HEADS = tuple((((h * HEAD_DIM, HEAD_DIM // 2, h * HEAD_DIM),
                     (h * HEAD_DIM + HEAD_DIM // 2, HEAD_DIM // 2, h * HEAD_DIM + HEAD_DIM // 2)), h * HEAD_DIM)
                   for h in range(GROUP_HEADS))
DENSE_HEADS = tuple((((h * HEAD_DIM, HEAD_DIM, h * HEAD_DIM),), h * HEAD_DIM) for h in range(GROUP_HEADS))
GQA_HEADS = tuple((((h * HEAD_DIM, HEAD_DIM, (h // 2) * HEAD_DIM),), (h // 2) * HEAD_DIM) for h in range(GROUP_HEADS))


def _group_segments(width, group):
    idx = np.arange(width) // group
    return jnp.asarray(idx[:, None] == idx[None, :], BF16)


def _rope_tables(cs, dim, width):
    cos, sin = cs
    t = cos.shape[0]
    zero = jnp.zeros_like(sin)
    cos_l = jnp.stack([cos, cos], axis=2).reshape(t, dim)
    ta = jnp.stack([-sin, zero], axis=2).reshape(t, dim)
    tb = jnp.stack([zero, sin], axis=2).reshape(t, dim)
    rep = width // dim
    return tuple(jnp.tile(z, (1, rep)) for z in (cos_l, ta, tb))


def _aprep_kernel(ub_ref, ud_ref, cb_ref, ab_ref, bb_ref, cd_ref, ad_ref, bd_ref, gb_ref, gd_ref, segb_ref, segd_ref,
                  qb_ref, kb_ref, vbt_ref, qd_ref, kd_ref, vdt_ref):
    gw = GROUP_W
    kvw = GQA_KV_HEADS * HEAD_DIM

    def norm_rope(x, seg, group, gain, cos, ta, tb, scale):
        ms = _head_sums(x * x, seg) * (1.0 / group)
        x = x * lax.rsqrt(ms + NORM_EPS) * gain
        q4 = group // 4
        outs = []
        for c0 in range(0, x.shape[1], LANES):
            xc = x[:, c0:c0 + LANES]
            up = pltpu.roll(xc, LANES - q4, axis=1)
            dn = pltpu.roll(xc, q4, axis=1)
            outs.append(xc * cos[:, c0:c0 + LANES] + up * ta[:, c0:c0 + LANES] + dn * tb[:, c0:c0 + LANES])
        y = jnp.concatenate(outs, axis=1) if len(outs) > 1 else outs[0]
        if scale != 1.0:
            y = y * scale
        return y.astype(BF16)

    ub = ub_ref[...]
    segb = segb_ref[...]
    cb, ab, bb = cb_ref[...], ab_ref[...], bb_ref[...]
    qb_ref[...] = norm_rope(ub[:, 0:gw], segb, DIFF_DQK, gb_ref[0:1, :], cb, ab, bb, LOG2E / math.sqrt(DIFF_DQK))
    kb_ref[...] = norm_rope(ub[:, gw:2 * gw], segb, DIFF_DQK, gb_ref[1:2, :], cb, ab, bb, 1.0)
    vbt_ref[0] = ub[:, 2 * gw:3 * gw].T.astype(BF16)
    ud = ud_ref[...]
    segd = segd_ref[...]
    cd, ad, bd = cd_ref[...], ad_ref[...], bd_ref[...]
    qd_ref[...] = norm_rope(ud[:, 0:gw], segd, HEAD_DIM, gd_ref[0:1, :], cd, ad, bd, LOG2E / math.sqrt(HEAD_DIM))
    kd_ref[...] = norm_rope(ud[:, gw:gw + kvw], segd[0:kvw, 0:kvw], HEAD_DIM, gd_ref[1:2, 0:kvw],
                            cd[:, 0:kvw], ad[:, 0:kvw], bd[:, 0:kvw], 1.0)
    vdt_ref[0] = ud[:, gw + kvw:gw + 2 * kvw].T.astype(BF16)


def _attention_prepare(ub, ud, rope, diff_norm, gqa_norm, t_len):
    n = ub.shape[0]
    b = n // t_len
    tm = ROW_TILE
    per_seq = t_len // tm
    gw = GROUP_W
    kvw = GQA_KV_HEADS * HEAD_DIM
    tabs_b, tabs_d = rope
    gains_b = jnp.tile(diff_norm, (1, gw // DIFF_DQK))
    gains_d = jnp.tile(gqa_norm, (1, gw // HEAD_DIM))

    def rows(w):
        return pl.BlockSpec((tm, w), lambda i: (i, 0))

    def tab():
        return pl.BlockSpec((tm, gw), lambda i: (i % per_seq, 0))

    def full(shape):
        return pl.BlockSpec(shape, lambda i: (0,) * len(shape))

    def chan(w):
        return pl.BlockSpec((1, w, tm), lambda i: (i // per_seq, 0, i % per_seq))

    return pl.pallas_call(
        _aprep_kernel,
        out_shape=(jax.ShapeDtypeStruct((n, gw), BF16), jax.ShapeDtypeStruct((n, gw), BF16),
                   jax.ShapeDtypeStruct((b, gw, t_len), BF16),
                   jax.ShapeDtypeStruct((n, gw), BF16), jax.ShapeDtypeStruct((n, kvw), BF16),
                   jax.ShapeDtypeStruct((b, kvw, t_len), BF16)),
        grid=(n // tm,),
        in_specs=[rows(B_IN), rows(D_IN), tab(), tab(), tab(), tab(), tab(), tab(), full((2, gw)), full((2, gw)),
                  full((gw, gw)), full((gw, gw))],
        out_specs=(rows(gw), rows(gw), chan(gw), rows(gw), rows(kvw), chan(kvw)),
        compiler_params=_cparams(("arbitrary",)),
        name="attention_prepare",
    )(ub, ud, *tabs_b, *tabs_d, gains_b, gains_d, _group_segments(gw, DIFF_DQK), _group_segments(gw, HEAD_DIM))


def _na_kernel(q_ref, k_ref, v_ref, kct_ref, vc_ref, bias_ref, o_ref, *, rows):
    r = pl.program_id(1)
    rs = jnp.clip(r - NA_ROWS // 2, 0, rows - NA_ROWS)
    start = pl.multiple_of(rs * GRID_W, GRID_W)
    win = NA_ROWS * GRID_W
    for h in range(GROUP_HEADS):
        q = q_ref[0, h]
        kw = k_ref[0, h, pl.ds(start, win), :]
        vw = v_ref[0, h, pl.ds(start, win), :]
        s_nb = lax.dot_general(q, kw, (((1,), (1,)), ((), ())), preferred_element_type=F32)
        s_nb = s_nb + bias_ref[0, h]
        s_cx = jnp.dot(q, kct_ref[0, h], preferred_element_type=F32)
        m = jnp.maximum(jnp.max(s_nb, axis=-1, keepdims=True), jnp.max(s_cx, axis=-1, keepdims=True))
        p_nb = jnp.exp(s_nb - m)
        p_cx = jnp.exp(s_cx - m)
        l = jnp.sum(p_nb, axis=-1, keepdims=True) + jnp.sum(p_cx, axis=-1, keepdims=True)
        o = (jnp.dot(p_nb.astype(BF16), vw, preferred_element_type=F32)
             + jnp.dot(p_cx.astype(BF16), vc_ref[0, h], preferred_element_type=F32))
        o_ref[0, h] = o / l


def _na_bias_table(rel_bias):
    w = np.arange(GRID_W)
    col_start = np.clip(w - NA_COLS // 2, 0, GRID_W - NA_COLS)
    c = np.arange(GRID_W)
    valid = (c[None, :] >= col_start[:, None]) & (c[None, :] < col_start[:, None] + NA_COLS)
    cidx = np.clip(c[None, :] - w[:, None] + NA_COLS - 1, 0, 2 * NA_COLS - 2)
    delta = np.arange(NA_ROWS)
    i = np.arange(NA_ROWS)
    ridx = i[None, :] - delta[:, None] + NA_ROWS - 1
    t = rel_bias[:, ridx]
    t = t[:, :, :, cidx]
    t = jnp.where(valid[None, None, None], t, MASK_VALUE)
    t = jnp.transpose(t, (1, 0, 3, 2, 4))
    return t.reshape(NA_ROWS, rel_bias.shape[0], GRID_W, NA_ROWS * GRID_W).astype(F32)


def _neighborhood_attention(q, k, v, kct, vc, bias_tab):
    b, h, s, d = q.shape
    l = kct.shape[-1]
    rows = s // GRID_W
    assert rows >= NA_ROWS
    half = NA_ROWS // 2

    def bias_idx(bi, r):
        return (r - jnp.clip(r - half, 0, rows - NA_ROWS), 0, 0, 0)

    return pl.pallas_call(
        functools.partial(_na_kernel, rows=rows),
        out_shape=jax.ShapeDtypeStruct((b, h, s, d), F32),
        grid=(b, rows),
        in_specs=[pl.BlockSpec((1, h, GRID_W, d), lambda bi, r: (bi, 0, r, 0)),
                  pl.BlockSpec((1, h, s, d), lambda bi, r: (bi, 0, 0, 0)),
                  pl.BlockSpec((1, h, s, d), lambda bi, r: (bi, 0, 0, 0)),
                  pl.BlockSpec((1, h, d, l), lambda bi, r: (bi, 0, 0, 0)),
                  pl.BlockSpec((1, h, l, d), lambda bi, r: (bi, 0, 0, 0)),
                  pl.BlockSpec((1, h, GRID_W, NA_ROWS * GRID_W), bias_idx)],
        out_specs=pl.BlockSpec((1, h, GRID_W, d), lambda bi, r: (bi, 0, r, 0)),
        compiler_params=_cparams(("arbitrary", "arbitrary")),
        name="neighborhood_attention",
    )(q, k, v, kct, vc, bias_tab)


SCAN_BATCH = SUBLANES
N_KQ = 5
FWD_W = (N_KQ + 1) * GROUP_W
BWD_W = (N_KQ - 1) * GROUP_W
V_SPLIT = 4
V_TILES = HEAD_DIM // (V_SPLIT * SUBLANES)
KT_BUFFERS = 4


def _scan_kernel(zf_ref, zb_ref, zrv_ref, s0_ref, yf_ref, yb_ref, sfin_ref, st_ref, sa_ref, *kt_refs, steps):
    i = pl.program_id(1)

    @pl.when(i == 0)
    def _():
        st_ref[...] = s0_ref[0]

    lane = lax.broadcasted_iota(jnp.int32, (SUBLANES, LANES), 1)
    lane_vq = lane // (LANES // V_SPLIT)
    chan_vq = (lane // (V_TILES * SUBLANES)) % V_SPLIT

    def operand_tile(q, s):
        sb = steps - 1 - s
        pieces = []
        for hp in range(2):
            pieces.append(zf_ref[s, :, pl.ds(q * GROUP_W + hp * LANES, LANES)])
        for hp in range(2):
            if q < N_KQ - 1:
                pieces.append(zb_ref[sb, :, pl.ds(q * GROUP_W + hp * LANES, LANES)])
            else:
                pieces.append(zrv_ref[sb, :, pl.ds((q - (N_KQ - 1)) * GROUP_W + hp * LANES, LANES)])
        tile = jnp.concatenate(pieces * V_SPLIT, axis=0)
        return tile.T

    def prepare(kt_ref, s):
        for q in range(N_KQ + 1):
            kt_ref[q] = operand_tile(q, s)

    nacc = 4

    def tree(parts):
        return (parts[0] + parts[1]) + (parts[2] + parts[3])

    prepare(kt_refs[0], 0)
    prepare(kt_refs[1], 1)
    for hpar in range(2):
        for vb in range(V_TILES):
            parts = [None] * nacc
            for k in range(HEAD_DIM):
                term = st_ref[hpar, k, vb] * kt_refs[0][0, pl.ds(hpar * HEAD_DIM + k, 1), :]
                parts[k % nacc] = term if parts[k % nacc] is None else parts[k % nacc] + term
            sa_ref[hpar, vb] = -tree(parts)

    def step(s, kt_ref, nxt_ref, far_ref):
        prepare(far_ref, jnp.minimum(s + 2, steps - 1))
        vt = kt_ref[N_KQ]
        ys = []
        for hpar in range(2):
            base = hpar * HEAD_DIM
            vops = []
            for vb in range(V_TILES):
                acc = None
                for vq in range(V_SPLIT):
                    r0 = base + vq * V_TILES * SUBLANES + vb * SUBLANES
                    piece = vt[r0:r0 + SUBLANES, :]
                    acc = piece if acc is None else jnp.where(lane_vq == vq, piece, acc)
                vops.append(acc)
            sas = [sa_ref[hpar, vb] for vb in range(V_TILES)]
            yp = [[None] * nacc for _ in range(V_TILES)]
            sp = [[None] * nacc for _ in range(V_TILES)]
            for k in range(HEAD_DIM):
                row = pl.ds(base + k, 1)
                w = kt_ref[1, row, :]
                ka = kt_ref[2, row, :]
                kd = kt_ref[3, row, :]
                rr = kt_ref[4, row, :]
                kn = nxt_ref[0, row, :]
                j = k % nacc
                for vb in range(V_TILES):
                    sn = st_ref[hpar, k, vb] * w + sas[vb] * ka + vops[vb] * kd
                    st_ref[hpar, k, vb] = sn
                    ty = sn * rr
                    yp[vb][j] = ty if yp[vb][j] is None else yp[vb][j] + ty
                    tn = sn * kn
                    sp[vb][j] = tn if sp[vb][j] is None else sp[vb][j] + tn
            for vb in range(V_TILES):
                sa_ref[hpar, vb] = -tree(sp[vb])
            ys.append([tree(yp[vb]) for vb in range(V_TILES)])
        ytile = jnp.concatenate([ys[hpar][vb] for hpar in range(2) for _ in range(V_SPLIT) for vb in range(V_TILES)],
                                axis=0)
        yt = ytile.T
        sb = steps - 1 - s
        for d in range(2):
            for hp in range(2):
                acc = None
                for vq in range(V_SPLIT):
                    r0 = vq * (LANES // V_SPLIT) + d * 2 * SUBLANES + hp * SUBLANES
                    piece = yt[r0:r0 + SUBLANES, :]
                    acc = piece if acc is None else jnp.where(chan_vq == vq, piece, acc)
                if d == 0:
                    yf_ref[s, :, pl.ds(hp * LANES, LANES)] = acc
                else:
                    yb_ref[sb, :, pl.ds(hp * LANES, LANES)] = acc

    nbuf = len(kt_refs)

    def body(j, carry):
        for u in range(nbuf):
            step(nbuf * j + u, kt_refs[u], kt_refs[(u + 1) % nbuf], kt_refs[(u + 2) % nbuf])
        return carry

    lax.fori_loop(0, steps // nbuf, body, 0)

    @pl.when(i == pl.num_programs(1) - 1)
    def _():
        sfin_ref[0] = st_ref[...]


def _rwkv_scan(zf, zb, s0):
    t_len, b, _ = zf.shape
    groups = b // SCAN_BATCH
    ts = SCAN_STEPS
    nblk = t_len // ts
    st_shape = (2, HEAD_DIM, V_TILES, SUBLANES, LANES)
    return pl.pallas_call(
        functools.partial(_scan_kernel, steps=ts),
        out_shape=(jax.ShapeDtypeStruct((t_len, b, GROUP_W), F32),
                   jax.ShapeDtypeStruct((t_len, b, GROUP_W), F32),
                   jax.ShapeDtypeStruct((groups,) + st_shape, F32)),
        grid=(groups, nblk),
        in_specs=[pl.BlockSpec((ts, SCAN_BATCH, FWD_W), lambda g, i: (i, g, 0)),
                  pl.BlockSpec((ts, SCAN_BATCH, BWD_W), lambda g, i: (nblk - 1 - i, g, 0)),
                  pl.BlockSpec((ts, SCAN_BATCH, 2 * GROUP_W), lambda g, i: (nblk - 1 - i, g, (N_KQ - 1) // 2)),
                  pl.BlockSpec((1,) + st_shape, lambda g, i: (g, 0, 0, 0, 0, 0))],
        out_specs=(pl.BlockSpec((ts, SCAN_BATCH, GROUP_W), lambda g, i: (i, g, 0)),
                   pl.BlockSpec((ts, SCAN_BATCH, GROUP_W), lambda g, i: (nblk - 1 - i, g, 0)),
                   pl.BlockSpec((1,) + st_shape, lambda g, i: (g, 0, 0, 0, 0, 0))),
        scratch_shapes=[pltpu.VMEM(st_shape, F32), pltpu.VMEM((2, V_TILES, SUBLANES, LANES), F32)]
        + [pltpu.VMEM((N_KQ + 1, LANES, LANES), F32)] * KT_BUFFERS,
        compiler_params=_cparams(("arbitrary", "arbitrary")),
        name="rwkv_scan",
    )(zf, zb, zf, s0)


PREP_STEPS = 32
DECAY_RATE = math.exp(-0.5)


def _prep_kernel(u_ref, up_ref, un_ref, mu_ref, w0_ref, wup_ref, a0_ref, aup_ref, gup_ref, kk_ref, ka_ref, rk_ref,
                 seg_ref, zf_ref, zb_ref, gb_ref):
    i = pl.program_id(1)
    last = pl.num_programs(1) - 1
    tb = u_ref.shape[0]
    u = u_ref[...]
    before = jnp.where(i > 0, up_ref[...], 0.0)
    after = jnp.where(i < last, un_ref[...], 0.0)
    prev = jnp.concatenate([before, u[:-1]], axis=0)
    nxt = jnp.concatenate([u[1:], after], axis=0)
    mu = mu_ref[...]
    x = (u + mu[0:1] * (prev - u) + mu[1:2] * (nxt - u)).reshape(tb * SCAN_BATCH, A_IN)
    gw = GROUP_W
    r = x[:, 0:gw]
    k = x[:, gw:2 * gw]
    v = x[:, 2 * gw:3 * gw]
    wd = x[:, 3 * gw:3 * gw + 2 * LORA_W]
    ad = x[:, 3 * gw + 2 * LORA_W:3 * gw + 2 * LORA_W + 2 * LORA_A]
    gd = x[:, 3 * gw + 2 * LORA_W + 2 * LORA_A:]
    seg = seg_ref[...]
    w_raw = w0_ref[...] + jnp.dot(jnp.tanh(wd).astype(BF16), wup_ref[...].astype(BF16), preferred_element_type=F32)
    decay = jnp.exp(-DECAY_RATE * jax.nn.sigmoid(w_raw))
    a = jax.nn.sigmoid(a0_ref[...] + jnp.dot(ad.astype(BF16), aup_ref[...].astype(BF16),
                                             preferred_element_type=F32))
    g = jnp.dot(jax.nn.sigmoid(gd).astype(BF16), gup_ref[...].astype(BF16), preferred_element_type=F32)
    bonus = jnp.zeros_like(r)
    for d, z_ref in enumerate((zf_ref, zb_ref)):
        sl = slice(d * gw, (d + 1) * gw)
        kk = k * kk_ref[:, sl]
        a_d = a[:, sl]
        kd = k * (1.0 + (a_d - 1.0) * ka_ref[:, sl])
        kk = kk * lax.rsqrt(_head_sums(kk * kk, seg) + 1e-12)
        bonus = bonus + _head_sums(r * kd * rk_ref[:, sl], seg) * v
        for q, val in enumerate((kk, decay[:, sl], kk * a_d, kd)):
            z_ref[:, :, q * gw:(q + 1) * gw] = val.reshape(tb, SCAN_BATCH, gw)
    zf_ref[:, :, 4 * gw:5 * gw] = r.reshape(tb, SCAN_BATCH, gw)
    zf_ref[:, :, 5 * gw:6 * gw] = v.reshape(tb, SCAN_BATCH, gw)
    gb_ref[:, :, 0:gw] = g.reshape(tb, SCAN_BATCH, gw)
    gb_ref[:, :, gw:2 * gw] = bonus.reshape(tb, SCAN_BATCH, gw)


def _block_diag(w):
    z = jnp.zeros_like(w[0])
    return jnp.concatenate([jnp.concatenate([w[0], z], axis=1), jnp.concatenate([z, w[1]], axis=1)], axis=0)


def _rwkv_prepare(ua_tm, mu, w0, w_up, a0, a_up, g_up, k_k, k_a, r_k):
    t, b, _ = ua_tm.shape
    tb = min(PREP_STEPS, t)
    nblk = t // tb
    gw2 = 2 * GROUP_W

    def full(shape):
        return pl.BlockSpec(shape, lambda g, i: (0,) * len(shape))

    return pl.pallas_call(
        _prep_kernel,
        out_shape=(jax.ShapeDtypeStruct((t, b, FWD_W), F32), jax.ShapeDtypeStruct((t, b, BWD_W), F32),
                   jax.ShapeDtypeStruct((t, b, gw2), F32)),
        grid=(b // SCAN_BATCH, nblk),
        in_specs=[pl.BlockSpec((tb, SCAN_BATCH, A_IN), lambda g, i: (i, g, 0)),
                  pl.BlockSpec((1, SCAN_BATCH, A_IN), lambda g, i: (jnp.maximum(i * tb - 1, 0), g, 0)),
                  pl.BlockSpec((1, SCAN_BATCH, A_IN), lambda g, i: (jnp.minimum((i + 1) * tb, t - 1), g, 0)),
                  full((2, A_IN)), full((1, gw2)), full((2 * LORA_W, gw2)), full((1, gw2)), full((2 * LORA_A, gw2)),
                  full((LORA_G, GROUP_W)), full((1, gw2)), full((1, gw2)), full((1, gw2)),
                  full((GROUP_W, GROUP_W))],
        out_specs=(pl.BlockSpec((tb, SCAN_BATCH, FWD_W), lambda g, i: (i, g, 0)),
                   pl.BlockSpec((tb, SCAN_BATCH, BWD_W), lambda g, i: (i, g, 0)),
                   pl.BlockSpec((tb, SCAN_BATCH, gw2), lambda g, i: (i, g, 0))),
        compiler_params=_cparams(("arbitrary", "arbitrary")),
        name="rwkv_prepare",
    )(ua_tm, ua_tm, ua_tm, mu, w0.reshape(1, gw2), _block_diag(w_up), a0.reshape(1, gw2), _block_diag(a_up), g_up,
      k_k.reshape(1, gw2), k_a.reshape(1, gw2), r_k.reshape(1, gw2), _head_segments())


def _rwkv_time_mix(ua_tm, s0, mu, w0, w_up, a0, a_up, g_up, k_k, k_a, r_k):
    t, b, _ = ua_tm.shape
    h, n = GROUP_HEADS, HEAD_DIM
    bp = -(-b // SCAN_BATCH) * SCAN_BATCH
    groups = bp // SCAN_BATCH
    if bp != b:
        ua_tm = jnp.pad(ua_tm, ((0, 0), (0, bp - b), (0, 0)))
    zf, zb, gb = _rwkv_prepare(ua_tm, mu, w0, w_up, a0, a_up, g_up, k_k, k_a, r_k)
    if s0 is None:
        s0l = jnp.zeros((groups, 2, n, V_TILES, SUBLANES, LANES), F32)
    else:
        s0 = jnp.pad(s0.astype(F32), ((0, bp - b),) + ((0, 0),) * 4)
        s0l = s0.reshape(groups, SCAN_BATCH, 2, 2, 2, V_SPLIT, V_TILES, SUBLANES, n)
        s0l = jnp.transpose(s0l, (0, 4, 8, 6, 7, 5, 2, 3, 1)).reshape(groups, 2, n, V_TILES, SUBLANES, LANES)
    yf, yb, s_fin = _rwkv_scan(zf, zb, s0l)
    s_fin = s_fin.reshape(groups, 2, n, V_TILES, SUBLANES, V_SPLIT, 2, 2, SCAN_BATCH)
    s_fin = jnp.transpose(s_fin, (0, 8, 6, 7, 1, 5, 3, 4, 2)).reshape(bp, 2, h, n, n)[:b]
    return yf, yb, gb, s_fin


def _expert_kernel(be_ref, nu_ref, x_ref, w1_ref, w3_ref, w2_ref, o_ref, w1s, w3s, w2s):
    i = pl.program_id(0)
    e = be_ref[i]
    prev = be_ref[jnp.maximum(i - 1, 0)]

    @pl.when((i == 0) | (e != prev))
    def _():
        w1s[...] = w1_ref[0].astype(BF16)
        w3s[...] = w3_ref[0].astype(BF16)
        w2s[...] = w2_ref[0].astype(BF16)

    @pl.when(i < nu_ref[0])
    def _():
        x = x_ref[...]
        a = jnp.dot(x, w1s[...], preferred_element_type=F32)
        g = jnp.dot(x, w3s[...], preferred_element_type=F32)
        hmid = (a * jax.nn.sigmoid(a)) * g
        o_ref[...] = jnp.dot(hmid.astype(BF16), w2s[...], preferred_element_type=F32)

    @pl.when(i >= nu_ref[0])
    def _():
        o_ref[...] = jnp.zeros_like(o_ref)


def _expert_mlp(xb, block_e, n_used, w1, w3, w2):
    cap, d = xb.shape
    bm = MOE_TILE
    de = w1.shape[-1]
    return pl.pallas_call(
        _expert_kernel,
        out_shape=jax.ShapeDtypeStruct((cap, d), F32),
        grid_spec=pltpu.PrefetchScalarGridSpec(
            num_scalar_prefetch=2,
            grid=(cap // bm,),
            in_specs=[pl.BlockSpec((bm, d), lambda i, be, nu: (i, 0)),
                      pl.BlockSpec((1, d, de), lambda i, be, nu: (be[i], 0, 0)),
                      pl.BlockSpec((1, d, de), lambda i, be, nu: (be[i], 0, 0)),
                      pl.BlockSpec((1, de, d), lambda i, be, nu: (be[i], 0, 0))],
            out_specs=pl.BlockSpec((bm, d), lambda i, be, nu: (i, 0)),
            scratch_shapes=[pltpu.VMEM((d, de), BF16), pltpu.VMEM((d, de), BF16), pltpu.VMEM((de, d), BF16)]),
        compiler_params=_cparams(("arbitrary",)),
        name="expert_mlp",
    )(block_e, n_used, xb, w1, w3, w2)


def _hier_moe(h_bf, logits, w1, w3, w2):
    n, d = h_bf.shape
    bm = MOE_TILE
    g_logits = logits[:, :N_EXPERT_GROUPS]
    g_idx = jnp.argmax(g_logits, axis=-1)
    g_top = jnp.max(jax.nn.softmax(g_logits, axis=-1), axis=-1)
    e_logits = logits[:, N_EXPERT_GROUPS:N_EXPERT_GROUPS + N_EXPERTS].reshape(n, N_EXPERT_GROUPS, EXPERTS_PER_GROUP)
    e_logits = jnp.take_along_axis(e_logits, g_idx[:, None, None], axis=1)[:, 0]
    top_v, top_i = lax.top_k(e_logits, TOP_K)
    gate = jax.nn.softmax(top_v, axis=-1) * g_top[:, None]
    eid = (g_idx[:, None] * EXPERTS_PER_GROUP + top_i).reshape(-1).astype(jnp.int32)
    tok = jnp.repeat(jnp.arange(n, dtype=jnp.int32), TOP_K)
    n_assign = n * TOP_K
    onehot = (eid[:, None] == jnp.arange(N_EXPERTS, dtype=jnp.int32)[None, :]).astype(jnp.int32)
    csum = jnp.cumsum(onehot, axis=0)
    counts = csum[-1]
    rank = jnp.take_along_axis(csum, eid[:, None], axis=1)[:, 0] - 1
    padded = (counts + bm - 1) // bm * bm
    pad_end = jnp.cumsum(padded)
    pad_start = pad_end - padded
    dest = pad_start[eid] + rank
    n_blocks = -(-(n_assign + N_EXPERTS * (bm - 1)) // bm)
    cap = n_blocks * bm
    buf_tok = jnp.zeros((cap,), jnp.int32).at[dest].set(tok)
    block_start = jnp.arange(n_blocks, dtype=jnp.int32) * bm
    block_e = jnp.minimum(jnp.sum((pad_end[None, :] <= block_start[:, None]).astype(jnp.int32), axis=1),
                          N_EXPERTS - 1).astype(jnp.int32)
    n_used = (pad_end[-1:] // bm).astype(jnp.int32)
    xb = h_bf[buf_tok]
    yb = _expert_mlp(xb, block_e, n_used, w1, w3, w2)
    d2 = dest.reshape(n, TOP_K)
    return yb[d2[:, 0]] * gate[:, 0:1] + yb[d2[:, 1]] * gate[:, 1:2]


def _rms(x, g):
    return x * lax.rsqrt(jnp.mean(x * x, axis=-1, keepdims=True) + NORM_EPS) * g


def _axial_rope(t_len, dim):
    q4 = dim // 4
    inv = ROPE_THETA ** (-jnp.arange(q4, dtype=F32) / q4)
    t = jnp.arange(t_len)
    row = (t // GRID_W).astype(F32)
    col = (t % GRID_W).astype(F32)
    ang = jnp.stack([row[:, None] * inv, col[:, None] * inv], axis=1)
    return jnp.cos(ang), jnp.sin(ang)


def _token_mixers(ua_tm, ub, uc, ud, rope, ctx, l, W):
    b, t, _ = ub.shape
    nh, n = GROUP_HEADS, HEAD_DIM
    latent = ctx is not None

    yf, yb, gb, s_fin = _rwkv_time_mix(ua_tm, ctx[0] if latent else None, W['rw_shift'][l], W['rw_w0'][l],
                                       W['rw_w_up'][l], W['rw_a0'][l], W['rw_a_up'][l], W['rw_g_up'][l],
                                       W['rw_k_k'][l], W['rw_k_a'][l], W['rw_r_k'][l])

    kvw = GQA_KV_HEADS * n
    if latent:
        qb, kb, vbt, qd, kd, vdt = _attention_prepare(ub.reshape(b * t, -1), ud.reshape(b * t, -1), rope,
                                                      W['diff_qk_norm'][l], W['gqa_qk_norm'][l], t)

    lam_init = 0.8 - 0.6 * math.exp(-0.3 * l)
    lv = W['diff_lambda'][l].astype(F32)
    lam = jnp.exp(jnp.sum(lv[0] * lv[1])) - jnp.exp(jnp.sum(lv[2] * lv[3])) + lam_init
    scalars = jnp.stack([lam, jnp.asarray(1.0 - lam_init, F32)]).astype(F32)
    if latent:
        k_ctx = jnp.transpose(ctx[1], (0, 3, 1, 2, 4)).reshape(b, -1, GROUP_W).astype(BF16)
        vt_ctx = jnp.transpose(ctx[2], (0, 1, 3, 2)).reshape(b, GROUP_W, -1).astype(BF16)
        out_b = _flash_attention(qb.reshape(b, t, GROUP_W), kb.reshape(b, t, GROUP_W), vbt, DIFF_HEADS, scalars,
                                 W['diff_subln'][l], k_ctx=k_ctx, vt_ctx=vt_ctx)
    else:
        qb, kb, vb = jnp.split(ub, 3, axis=-1)
        qb = _rms(qb.reshape(b, t, nh, 2, DIFF_DQK), W['diff_qk_norm'][l, 0])
        kb = _rms(kb.reshape(b, t, nh, 2, DIFF_DQK), W['diff_qk_norm'][l, 1])
        out_b = _flash_attention((qb * (LOG2E / math.sqrt(DIFF_DQK))).reshape(b, t, GROUP_W).astype(BF16),
                                 kb.reshape(b, t, GROUP_W).astype(BF16), jnp.swapaxes(vb, 1, 2).astype(BF16),
                                 DIFF_HEADS, scalars, W['diff_subln'][l])

    qc, kc, vc = jnp.split(uc, 3, axis=-1)
    qc = _rms(qc.reshape(b, t, nh, n), W['na_qk_norm'][l, 0])
    kc = _rms(kc.reshape(b, t, nh, n), W['na_qk_norm'][l, 1])
    if latent:
        vch = vc.reshape(b, t, nh, n).transpose(0, 2, 1, 3)
        oc = _neighborhood_attention((qc * (1.0 / math.sqrt(n))).transpose(0, 2, 1, 3).astype(BF16),
                                     kc.transpose(0, 2, 1, 3).astype(BF16), vch.astype(BF16),
                                     jnp.swapaxes(ctx[3], -1, -2).astype(BF16), ctx[4].astype(BF16),
                                     _na_bias_table(W['na_rel_bias'][l]))
        out_c = oc.transpose(0, 2, 1, 3).reshape(b, t, GROUP_W)
    else:
        out_c = _flash_attention((qc * (LOG2E / math.sqrt(n))).reshape(b, t, GROUP_W).astype(BF16),
                                 kc.reshape(b, t, GROUP_W).astype(BF16), jnp.swapaxes(vc, 1, 2).astype(BF16),
                                 DENSE_HEADS)

    if latent:
        k_ctx = jnp.transpose(ctx[5], (0, 2, 1, 3)).reshape(b, -1, kvw).astype(BF16)
        vt_ctx = jnp.transpose(ctx[6], (0, 1, 3, 2)).reshape(b, kvw, -1).astype(BF16)
        out_d = _flash_attention(qd.reshape(b, t, GROUP_W), kd.reshape(b, t, kvw), vdt, GQA_HEADS,
                                 k_ctx=k_ctx, vt_ctx=vt_ctx)
    else:
        qd, kd, vd = jnp.split(ud, [GROUP_W, GROUP_W + kvw], axis=-1)
        qd = _rms(qd.reshape(b, t, nh, n), W['gqa_qk_norm'][l, 0])
        kd = _rms(kd.reshape(b, t, GQA_KV_HEADS, n), W['gqa_qk_norm'][l, 1])
        out_d = _flash_attention((qd * (LOG2E / math.sqrt(n))).reshape(b, t, GROUP_W).astype(BF16),
                                 kd.reshape(b, t, kvw).astype(BF16), jnp.swapaxes(vd, 1, 2).astype(BF16), GQA_HEADS)

    mix = (yf, yb, gb) + tuple(z.reshape(b * t, GROUP_W) for z in (out_b, out_c, out_d))
    if latent:
        return mix, None
    new_ctx = (s_fin, jnp.transpose(kb, (0, 2, 3, 1, 4)), vb.reshape(b, t, nh, n).transpose(0, 2, 1, 3),
               kc.transpose(0, 2, 1, 3), vc.reshape(b, t, nh, n).transpose(0, 2, 1, 3),
               kd.transpose(0, 2, 1, 3), vd.reshape(b, t, GQA_KV_HEADS, n).transpose(0, 2, 1, 3))
    return mix, new_ctx


def _mixing_sublayer(x, cond, rope, ctx, l, W):
    b, t, d = x.shape
    n = b * t
    mod = _modulation(cond, W['w_mod'][l], W['b_mod'][l])
    sh1, sc1, g1, sh2, sc2, g2 = jnp.split(mod, 6, axis=-1)
    x2 = x.reshape(n, d)
    ua, ub, uc, ud = _in_projection(x2, W['norm_mix'][l], 1.0 + sc1, sh1, W['w_in'][l], t)
    mix, new_ctx = _token_mixers(ua, ub.reshape(b, t, -1), uc.reshape(b, t, -1), ud.reshape(b, t, -1), rope, ctx, l, W)
    w_router = jnp.concatenate([W['moe_w_group'][l], W['moe_w_router'][l]], axis=1)
    w_router = jnp.pad(w_router, ((0, 0), (0, ROUTER_W - w_router.shape[1])))
    b_router = jnp.concatenate([W['moe_b_group'][l], W['moe_b_router'][l]])
    b_router = jnp.pad(b_router, (0, ROUTER_W - b_router.shape[0])).reshape(1, ROUTER_W)
    x2, h2, logits = _out_projection(*mix, W['rw_gn_w'][l], W['rw_gn_b'][l], W['w_out'][l], x2, g1,
                                     W['norm_ffn'][l], 1.0 + sc2, sh2, w_router, b_router, t)
    g2r = jnp.broadcast_to(g2[:, None, :], (g2.shape[0], n // g2.shape[0], d)).reshape(n, d)
    return x2, h2, logits, g2r, new_ctx


def kernel(x_prompt, x_sample, c, state_rwkv, cache_diff_k, cache_diff_v, cache_na_k, cache_na_v,
           cache_gqa_k, cache_gqa_v, c_ctx, norm_mix, norm_ffn, w_mod, b_mod, w_in, w_out,
           rw_shift, rw_w0, rw_w_up, rw_a0, rw_a_up, rw_g_up, rw_k_k, rw_k_a, rw_r_k, rw_gn_w, rw_gn_b,
           diff_qk_norm, diff_lambda, diff_subln, na_qk_norm, na_rel_bias, gqa_qk_norm,
           moe_w_group, moe_b_group, moe_w_router, moe_b_router, moe_w1, moe_w3, moe_w2):
    W = {
        'norm_mix': norm_mix, 'norm_ffn': norm_ffn, 'w_mod': w_mod, 'b_mod': b_mod,
        'w_in': w_in, 'w_out': w_out, 'rw_shift': rw_shift, 'rw_w0': rw_w0, 'rw_w_up': rw_w_up,
        'rw_a0': rw_a0, 'rw_a_up': rw_a_up, 'rw_g_up': rw_g_up, 'rw_k_k': rw_k_k, 'rw_k_a': rw_k_a,
        'rw_r_k': rw_r_k, 'rw_gn_w': rw_gn_w, 'rw_gn_b': rw_gn_b, 'diff_qk_norm': diff_qk_norm,
        'diff_lambda': diff_lambda, 'diff_subln': diff_subln, 'na_qk_norm': na_qk_norm,
        'na_rel_bias': na_rel_bias, 'gqa_qk_norm': gqa_qk_norm, 'moe_w_group': moe_w_group,
        'moe_b_group': moe_b_group, 'moe_w_router': moe_w_router, 'moe_b_router': moe_b_router,
        'moe_w1': moe_w1, 'moe_w3': moe_w3, 'moe_w2': moe_w2,
    }
    depth = w_in.shape[0]

    def run_pass(x, cond, rope, caches):
        new_ctx = []
        for l in range(depth):
            ctx = None if caches is None else tuple(z[:, l] for z in caches)
            r2, h2, logits, g2r, nc = _mixing_sublayer(x, cond, rope, ctx, l, W)
            y = _hier_moe(h2, logits, W['moe_w1'][l], W['moe_w3'][l], W['moe_w2'][l])
            x = (r2 + g2r * y).reshape(x.shape)
            new_ctx.append(nc)
        return x, new_ctx

    xp, ctx_layers = run_pass(x_prompt, c_ctx[None, :], None, None)
    new_caches = tuple(jnp.stack([z[i] for z in ctx_layers], axis=1) for i in range(7))

    t_lat = x_sample.shape[1]
    rope = (_rope_tables(_axial_rope(t_lat, DIFF_DQK), DIFF_DQK, GROUP_W),
            _rope_tables(_axial_rope(t_lat, HEAD_DIM), HEAD_DIM, GROUP_W))
    xs, _ = run_pass(x_sample, c, rope, (state_rwkv, cache_diff_k, cache_diff_v, cache_na_k, cache_na_v,
                                        cache_gqa_k, cache_gqa_v))
    return (xp, xs) + new_caches
```

```python
import functools
import math

import numpy as np
import jax
import jax.numpy as jnp
from jax import lax
from jax.experimental import pallas as pl
from jax.experimental.pallas import tpu as pltpu

F32 = jnp.float32
BF16 = jnp.bfloat16

D_MODEL = 1024
GRID_W = 64
HEAD_DIM = 64
GROUP_W = 256
GROUP_HEADS = 4
LORA_W = 64
LORA_A = 64
LORA_G = 128
RWKV_GN_EPS = 64e-5
DIFF_DQK = 32
GQA_KV_HEADS = 2
GQA_GROUP = 2
NA_ROWS = 8
NA_COLS = 16
ROPE_THETA = 10000.0
NORM_EPS = 1e-6
N_EXPERT_GROUPS = 4
EXPERTS_PER_GROUP = 8
N_EXPERTS = 32
TOP_K = 2
D_EXPERT = 512
A_IN = 3 * GROUP_W + 2 * LORA_W + 2 * LORA_A + LORA_G
B_IN = 3 * GROUP_W
C_IN = 3 * GROUP_W
D_IN = GROUP_W + 2 * GQA_KV_HEADS * HEAD_DIM
IN_W = A_IN + B_IN + C_IN + D_IN

LANES = 128
SUBLANES = 8
VMEM_LIMIT = 48 * 1024 * 1024

ROW_TILE = 256
Q_TILE = 256
SCAN_STEPS = 32
KV_CHUNK = 512
MOE_TILE = 256
ROUTER_W = 128
MASK_VALUE = -1e30
LOG2E = 1.4426950408889634


def _cparams(sem):
    return pltpu.CompilerParams(dimension_semantics=sem, vmem_limit_bytes=VMEM_LIMIT)


def _mod_kernel(c_ref, w_ref, b_ref, o_ref):
    c = c_ref[...]
    a = c * jax.nn.sigmoid(c)
    o_ref[...] = jnp.dot(a, w_ref[...], preferred_element_type=F32,
                         precision=lax.Precision.HIGHEST) + b_ref[...]


def _modulation(cond, w, b):
    m, d = cond.shape
    n = w.shape[1]
    mp = -(-m // SUBLANES) * SUBLANES
    cp = jnp.pad(cond, ((0, mp - m), (0, 0)))
    tn = 768
    out = pl.pallas_call(
        _mod_kernel,
        out_shape=jax.ShapeDtypeStruct((mp, n), F32),
        grid=(n // tn,),
        in_specs=[pl.BlockSpec((mp, d), lambda j: (0, 0)),
                  pl.BlockSpec((d, tn), lambda j: (0, j)),
                  pl.BlockSpec((1, tn), lambda j: (0, j))],
        out_specs=pl.BlockSpec((mp, tn), lambda j: (0, j)),
        compiler_params=_cparams(("arbitrary",)),
        name="modulation",
    )(cp, w, b.reshape(1, n))
    return out[:m]


def _inproj_kernel(x_ref, g_ref, sc_ref, sh_ref, w_ref, oa_ref, ob_ref, oc_ref, od_ref):
    x = x_ref[...]
    ms = jnp.mean(x * x, axis=-1, keepdims=True)
    h = x * lax.rsqrt(ms + NORM_EPS) * g_ref[...]
    h = h * sc_ref[0] + sh_ref[0]
    u = jnp.dot(h.astype(BF16), w_ref[...].astype(BF16), preferred_element_type=F32)
    oa_ref[...] = u[:, :A_IN]
    ob_ref[...] = u[:, A_IN:A_IN + B_IN]
    oc_ref[...] = u[:, A_IN + B_IN:A_IN + B_IN + C_IN]
    od_ref[...] = u[:, A_IN + B_IN + C_IN:]


def _in_projection(x2, gain, scale1p, shift, w_in, t_len):
    n, d = x2.shape
    bm = scale1p.shape[0]
    tm = ROW_TILE
    per_seq = t_len // tm
    b = n // t_len

    def mod_idx(i):
        return ((i // per_seq) if bm > 1 else 0, 0, 0)

    widths = (B_IN, C_IN, D_IN)
    ua, ub, uc, ud = pl.pallas_call(
        _inproj_kernel,
        out_shape=(jax.ShapeDtypeStruct((t_len, b * A_IN), F32),)
        + tuple(jax.ShapeDtypeStruct((n, w), F32) for w in widths),
        grid=(n // tm,),
        in_specs=[pl.BlockSpec((tm, d), lambda i: (i, 0)),
                  pl.BlockSpec((1, d), lambda i: (0, 0)),
                  pl.BlockSpec((1, 1, d), mod_idx),
                  pl.BlockSpec((1, 1, d), mod_idx),
                  pl.BlockSpec((d, IN_W), lambda i: (0, 0))],
        out_specs=(pl.BlockSpec((tm, A_IN), lambda i: (i % per_seq, i // per_seq)),)
        + tuple(pl.BlockSpec((tm, w), lambda i: (i, 0)) for w in widths),
        compiler_params=_cparams(("arbitrary",)),
        name="in_projection",
    )(x2, gain.reshape(1, d), scale1p.reshape(bm, 1, d), shift.reshape(bm, 1, d), w_in)
    return ua.reshape(t_len, b, A_IN), ub, uc, ud


def _head_sums(x, seg):
    hi = x.astype(BF16)
    lo = (x - hi.astype(F32)).astype(BF16)
    return (jnp.dot(hi, seg, preferred_element_type=F32) + jnp.dot(lo, seg, preferred_element_type=F32))


def _head_segments():
    idx = np.arange(GROUP_W) // HEAD_DIM
    return jnp.asarray(idx[:, None] == idx[None, :], BF16)


def _outproj_kernel(yf_ref, yb_ref, gb_ref, ob_ref, oc_ref, od_ref, gnw_ref, gnb_ref, seg_ref, w_ref, x_ref, g1_ref,
                    gn_ref, sc_ref, sh_ref, wr_ref, br_ref, xo_ref, h_ref, lg_ref):
    gw = GROUP_W
    o = yf_ref[...] + yb_ref[...]
    seg = seg_ref[...]
    mean = _head_sums(o, seg) * (1.0 / HEAD_DIM)
    cen = o - mean
    var = _head_sums(cen * cen, seg) * (1.0 / HEAD_DIM)
    oa = cen * lax.rsqrt(var + RWKV_GN_EPS) * gnw_ref[...] + gnb_ref[...]
    oa = (oa + gb_ref[:, gw:2 * gw]) * gb_ref[:, 0:gw]
    y = jnp.dot(oa.astype(BF16), w_ref[0:gw, :].astype(BF16), preferred_element_type=F32)
    for j, m_ref in enumerate((ob_ref, oc_ref, od_ref)):
        y = y + jnp.dot(m_ref[...].astype(BF16), w_ref[(j + 1) * gw:(j + 2) * gw, :].astype(BF16),
                        preferred_element_type=F32)
    xn = x_ref[...] + g1_ref[0] * y
    xo_ref[...] = xn
    ms = jnp.mean(xn * xn, axis=-1, keepdims=True)
    h = xn * lax.rsqrt(ms + NORM_EPS) * gn_ref[...]
    h = (h * sc_ref[0] + sh_ref[0]).astype(BF16)
    h_ref[...] = h
    lg_ref[...] = jnp.dot(h, wr_ref[...].astype(BF16), preferred_element_type=F32) + br_ref[...]


def _out_projection(yf, yb, gb, ob, oc, od, gn_w, gn_b, w_out, x2, gate1, gain, scale1p, shift, w_router, b_router,
                    t_len):
    n, d = x2.shape
    bm = gate1.shape[0]
    tm = ROW_TILE
    per_seq = t_len // tm
    gw = GROUP_W

    def mod_idx(i):
        return ((i // per_seq) if bm > 1 else 0, 0, 0)

    def tm_idx(i):
        return (i % per_seq, i // per_seq)

    def rows(w):
        return pl.BlockSpec((tm, w), lambda i: (i, 0))

    def full(shape):
        return pl.BlockSpec(shape, lambda i: (0,) * len(shape))

    mod_spec = pl.BlockSpec((1, 1, d), mod_idx)
    return pl.pallas_call(
        _outproj_kernel,
        out_shape=(jax.ShapeDtypeStruct((n, d), F32),
                   jax.ShapeDtypeStruct((n, d), BF16),
                   jax.ShapeDtypeStruct((n, ROUTER_W), F32)),
        grid=(n // tm,),
        in_specs=[pl.BlockSpec((tm, gw), tm_idx), pl.BlockSpec((tm, gw), tm_idx), pl.BlockSpec((tm, 2 * gw), tm_idx),
                  rows(gw), rows(gw), rows(gw), full((1, gw)), full((1, gw)), full((gw, gw)), full((d, d)), rows(d),
                  mod_spec, full((1, d)), mod_spec, mod_spec, full((d, ROUTER_W)), full((1, ROUTER_W))],
        out_specs=(rows(d), rows(d), rows(ROUTER_W)),
        compiler_params=_cparams(("arbitrary",)),
        name="out_projection",
    )(yf.reshape(t_len, -1), yb.reshape(t_len, -1), gb.reshape(t_len, -1), ob, oc, od,
      gn_w.reshape(1, gw), gn_b.reshape(1, gw), _head_segments(), w_out, x2, gate1.reshape(bm, 1, d),
      gain.reshape(1, d), scale1p.reshape(bm, 1, d), shift.reshape(bm, 1, d), w_router, b_router)


ACC_ROWS = HEAD_DIM + 16


def _flash_kernel(sc_ref, q_ref, *refs, heads, diff, chunk, has_ctx):
    if has_ctx:
        kc_ref, vtc_ref, k_ref, vt_ref, g_ref, o_ref, m_ref, acc_ref, sa_ref, sb_ref = refs
        n_ctx = kc_ref.shape[1] // chunk
    else:
        k_ref, vt_ref, g_ref, o_ref, m_ref, acc_ref, sa_ref, sb_ref = refs
        kc_ref = vtc_ref = None
        n_ctx = 0
    tq = q_ref.shape[1]
    kdim = k_ref.shape[2]
    n_lat = k_ref.shape[1] // chunk
    qt = q_ref[0].astype(F32).T.astype(BF16)

    def stationary(q_rows, k_row0):
        pieces = []
        if k_row0 > 0:
            pieces.append(jnp.zeros((k_row0, tq), BF16))
        pieces.append(q_rows)
        rest = kdim - k_row0 - q_rows.shape[0]
        if rest > 0:
            pieces.append(jnp.zeros((rest, tq), BF16))
        return jnp.concatenate(pieces, axis=0) if len(pieces) > 1 else pieces[0]

    streams = []
    for (maps, v_row0) in heads:
        for (qs, qn, ks) in maps:
            streams.append((stationary(qt[qs:qs + qn], ks), v_row0))
    m_ref[...] = jnp.full(m_ref.shape, MASK_VALUE, F32)
    acc_ref[...] = jnp.zeros(acc_ref.shape, F32)
    ones_rows = (lax.broadcasted_iota(jnp.int32, (ACC_ROWS - HEAD_DIM, chunk), 0) == 0).astype(BF16)

    def key_rows(src):
        kind, c = src
        if kind == "ctx":
            return kc_ref[0, c * chunk:(c + 1) * chunk, :]
        return k_ref[0, pl.ds(pl.multiple_of(c * chunk, chunk), chunk), :]

    def value_cols(src, v_row0):
        kind, c = src
        if kind == "ctx":
            return vtc_ref[0, v_row0:v_row0 + HEAD_DIM, c * chunk:(c + 1) * chunk]
        return vt_ref[0, pl.ds(v_row0, HEAD_DIM), pl.ds(pl.multiple_of(c * chunk, chunk), chunk)]

    def score(src, s_ref, j):
        s_ref[j] = jnp.dot(key_rows(src), streams[j][0], preferred_element_type=F32)

    def accumulate(src, s_ref, j):
        s = s_ref[j]
        m = m_ref[j]
        m_new = jnp.maximum(m, jnp.max(s, axis=0, keepdims=True))
        p = jnp.exp2(s - m_new).astype(BF16)
        alpha = jnp.exp2(m - m_new)
        m_ref[j] = m_new
        vc = jnp.concatenate([value_cols(src, streams[j][1]), ones_rows], axis=0)
        acc_ref[j] = alpha * acc_ref[j] + jnp.dot(vc, p, preferred_element_type=F32)

    ns = len(streams)

    def stage(nxt, next_ref, cur, cur_ref):
        for j in range(ns):
            if nxt is not None:
                score(nxt, next_ref, j)
            if cur is not None:
                accumulate(cur, cur_ref, j)

    bufs = (sa_ref, sb_ref)
    first = ("ctx", 0) if n_ctx else ("lat", 0)
    stage(first, bufs[0], None, None)
    for g in range(n_ctx):
        nxt = ("ctx", g + 1) if g + 1 < n_ctx else ("lat", 0)
        stage(nxt, bufs[(g + 1) % 2], ("ctx", g), bufs[g % 2])
    ra, rb = bufs[n_ctx % 2], bufs[(n_ctx + 1) % 2]

    def body(jj, carry):
        c = 2 * jj
        stage(("lat", c + 1), rb, ("lat", c), ra)
        stage(("lat", c + 2), ra, ("lat", c + 1), rb)
        return carry

    lax.fori_loop(0, (n_lat - 1) // 2, body, 0)
    if n_lat % 2 == 0:
        stage(("lat", n_lat - 1), rb, ("lat", n_lat - 2), ra)
        stage(None, None, ("lat", n_lat - 1), rb)
    else:
        stage(None, None, ("lat", n_lat - 1), ra)

    outs = []
    j = 0
    for (maps, v_row0) in heads:
        a1 = acc_ref[j]
        o = a1[:HEAD_DIM] / a1[HEAD_DIM:HEAD_DIM + 1]
        if diff:
            a2 = acc_ref[j + 1]
            o = o - (sc_ref[0] / a2[HEAD_DIM:HEAD_DIM + 1]) * a2[:HEAD_DIM]
            ms = jnp.mean(o * o, axis=0, keepdims=True)
            o = o * lax.rsqrt(ms + NORM_EPS) * g_ref[...] * sc_ref[1]
        j += len(maps)
        outs.append(o)
    o_ref[0] = jnp.concatenate(outs, axis=0).T


def _flash_attention(q, k, vt, heads, scalars=None, gain=None, k_ctx=None, vt_ctx=None):
    b, s, c = q.shape
    l, ck = k.shape[1:]
    cv = vt.shape[1]
    tq = min(Q_TILE, s)
    chunk = min(KV_CHUNK, l)
    assert l % chunk == 0 and s % tq == 0
    diff = scalars is not None
    has_ctx = k_ctx is not None
    ns = sum(len(maps) for maps, _ in heads)
    if not diff:
        scalars = jnp.zeros((2,), F32)
        gain = jnp.ones((HEAD_DIM,), F32)
    in_specs = [pl.BlockSpec((1, tq, c), lambda bi, i, sc: (bi, i, 0))]
    args = [q]
    if has_ctx:
        lc = k_ctx.shape[1]
        assert lc % chunk == 0
        in_specs += [pl.BlockSpec((1, lc, ck), lambda bi, i, sc: (bi, 0, 0)),
                     pl.BlockSpec((1, cv, lc), lambda bi, i, sc: (bi, 0, 0))]
        args += [k_ctx, vt_ctx]
    in_specs += [pl.BlockSpec((1, l, ck), lambda bi, i, sc: (bi, 0, 0)),
                 pl.BlockSpec((1, cv, l), lambda bi, i, sc: (bi, 0, 0)),
                 pl.BlockSpec((HEAD_DIM, 1), lambda bi, i, sc: (0, 0))]
    args += [k, vt, gain.reshape(HEAD_DIM, 1)]
    return pl.pallas_call(
        functools.partial(_flash_kernel, heads=heads, diff=diff, chunk=chunk, has_ctx=has_ctx),
        out_shape=jax.ShapeDtypeStruct((b, s, c), F32),
        grid_spec=pltpu.PrefetchScalarGridSpec(
            num_scalar_prefetch=1,
            grid=(b, s // tq),
            in_specs=in_specs,
            out_specs=pl.BlockSpec((1, tq, c), lambda bi, i, sc: (bi, i, 0)),
            scratch_shapes=[pltpu.VMEM((ns, 1, tq), F32), pltpu.VMEM((ns, ACC_ROWS, tq), F32),
                            pltpu.VMEM((ns, chunk, tq), F32), pltpu.VMEM((ns, chunk, tq), F32)]),
        compiler_params=_cparams(("arbitrary", "arbitrary")),
        name="flash_attention",
    )(scalars, *args)


DIFF_HEADS = tuple((((h * HEAD_DIM, HEAD_DIM // 2, h * HEAD_DIM),
                     (h * HEAD_DIM + HEAD_DIM // 2, HEAD_DIM // 2, h * HEAD_DIM + HEAD_DIM // 2)), h * HEAD_DIM)
                   for h in range(GROUP_HEADS))
DENSE_HEADS = tuple((((h * HEAD_DIM, HEAD_DIM, h * HEAD_DIM),), h * HEAD_DIM) for h in range(GROUP_HEADS))
GQA_HEADS = tuple((((h * HEAD_DIM, HEAD_DIM, (h // 2) * HEAD_DIM),), (h // 2) * HEAD_DIM) for h in range(GROUP_HEADS))


def _group_segments(width, group):
    idx = np.arange(width) // group
    return jnp.asarray(idx[:, None] == idx[None, :], BF16)


def _rope_tables(cs, dim, width):
    cos, sin = cs
    t = cos.shape[0]
    zero = jnp.zeros_like(sin)
    cos_l = jnp.stack([cos, cos], axis=2).reshape(t, dim)
    ta = jnp.stack([-sin, zero], axis=2).reshape(t, dim)
    tb = jnp.stack([zero, sin], axis=2).reshape(t, dim)
    rep = width // dim
    return tuple(jnp.tile(z, (1, rep)) for z in (cos_l, ta, tb))


def _aprep_kernel(ub_ref, uc_ref, ud_ref, cb_ref, ab_ref, bb_ref, cd_ref, ad_ref, bd_ref, gb_ref, gc_ref, gd_ref,
                  segb_ref, segd_ref, qb_ref, kb_ref, vbt_ref, qc_ref, kc_ref, vct_ref, qd_ref, kd_ref, vdt_ref):
    gw = GROUP_W
    kvw = GQA_KV_HEADS * HEAD_DIM

    def norm(x, seg, group, gain):
        ms = _head_sums(x * x, seg) * (1.0 / group)
        return x * lax.rsqrt(ms + NORM_EPS) * gain

    def norm_rope(x, seg, group, gain, cos, ta, tb, scale):
        x = norm(x, seg, group, gain)
        q4 = group // 4
        outs = []
        for c0 in range(0, x.shape[1], LANES):
            xc = x[:, c0:c0 + LANES]
            up = pltpu.roll(xc, LANES - q4, axis=1)
            dn = pltpu.roll(xc, q4, axis=1)
            outs.append(xc * cos[:, c0:c0 + LANES] + up * ta[:, c0:c0 + LANES] + dn * tb[:, c0:c0 + LANES])
        y = jnp.concatenate(outs, axis=1) if len(outs) > 1 else outs[0]
        if scale != 1.0:
            y = y * scale
        return y.astype(BF16)

    ub = ub_ref[...]
    segb = segb_ref[...]
    cb, ab, bb = cb_ref[...], ab_ref[...], bb_ref[...]
    qb_ref[...] = norm_rope(ub[:, 0:gw], segb, DIFF_DQK, gb_ref[0:1, :], cb, ab, bb, LOG2E / math.sqrt(DIFF_DQK))
    kb_ref[...] = norm_rope(ub[:, gw:2 * gw], segb, DIFF_DQK, gb_ref[1:2, :], cb, ab, bb, 1.0)
    vbt_ref[0] = ub[:, 2 * gw:3 * gw].T.astype(BF16)
    segd = segd_ref[...]
    uc = uc_ref[...]
    qc_ref[...] = (norm(uc[:, 0:gw], segd, HEAD_DIM, gc_ref[0:1, :]) * (LOG2E / math.sqrt(HEAD_DIM))).astype(BF16)
    kc_ref[...] = norm(uc[:, gw:2 * gw], segd, HEAD_DIM, gc_ref[1:2, :]).astype(BF16)
    vct_ref[0] = uc[:, 2 * gw:3 * gw].T.astype(BF16)
    ud = ud_ref[...]
    cd, ad, bd = cd_ref[...], ad_ref[...], bd_ref[...]
    qd_ref[...] = norm_rope(ud[:, 0:gw], segd, HEAD_DIM, gd_ref[0:1, :], cd, ad, bd, LOG2E / math.sqrt(HEAD_DIM))
    kd_ref[...] = norm_rope(ud[:, gw:gw + kvw], segd[0:kvw, 0:kvw], HEAD_DIM, gd_ref[1:2, 0:kvw],
                            cd[:, 0:kvw], ad[:, 0:kvw], bd[:, 0:kvw], 1.0)
    vdt_ref[0] = ud[:, gw + kvw:gw + 2 * kvw].T.astype(BF16)


def _attention_prepare(ub, uc, ud, rope, diff_norm, na_norm, gqa_norm, t_len):
    n = ub.shape[0]
    b = n // t_len
    tm = ROW_TILE
    per_seq = t_len // tm
    gw = GROUP_W
    kvw = GQA_KV_HEADS * HEAD_DIM
    tabs_b, tabs_d = rope
    gains_b = jnp.tile(diff_norm, (1, gw // DIFF_DQK))
    gains_c = jnp.tile(na_norm, (1, gw // HEAD_DIM))
    gains_d = jnp.tile(gqa_norm, (1, gw // HEAD_DIM))

    def rows(w):
        return pl.BlockSpec((tm, w), lambda i: (i, 0))

    def tab():
        return pl.BlockSpec((tm, gw), lambda i: (i % per_seq, 0))

    def full(shape):
        return pl.BlockSpec(shape, lambda i: (0,) * len(shape))

    def chan(w):
        return pl.BlockSpec((1, w, tm), lambda i: (i // per_seq, 0, i % per_seq))

    return pl.pallas_call(
        _aprep_kernel,
        out_shape=(jax.ShapeDtypeStruct((n, gw), BF16), jax.ShapeDtypeStruct((n, gw), BF16),
                   jax.ShapeDtypeStruct((b, gw, t_len), BF16),
                   jax.ShapeDtypeStruct((n, gw), BF16), jax.ShapeDtypeStruct((n, gw), BF16),
                   jax.ShapeDtypeStruct((b, gw, t_len), BF16),
                   jax.ShapeDtypeStruct((n, gw), BF16), jax.ShapeDtypeStruct((n, kvw), BF16),
                   jax.ShapeDtypeStruct((b, kvw, t_len), BF16)),
        grid=(n // tm,),
        in_specs=[rows(B_IN), rows(C_IN), rows(D_IN), tab(), tab(), tab(), tab(), tab(), tab(), full((2, gw)),
                  full((2, gw)), full((2, gw)), full((gw, gw)), full((gw, gw))],
        out_specs=(rows(gw), rows(gw), chan(gw), rows(gw), rows(gw), chan(gw), rows(gw), rows(kvw), chan(kvw)),
        compiler_params=_cparams(("arbitrary",)),
        name="attention_prepare",
    )(ub, uc, ud, *tabs_b, *tabs_d, gains_b, gains_c, gains_d, _group_segments(gw, DIFF_DQK),
      _group_segments(gw, HEAD_DIM))


NA_QROWS = 4
NA_SPAN = 12


def _na_kernel(q_ref, k_ref, vt_ref, kc_ref, vtc_ref, bias_ref, o_ref, *, grid_rows):
    i = pl.program_id(1)
    start = jnp.clip(i * NA_QROWS - NA_ROWS // 2, 0, grid_rows - NA_SPAN)
    off = pl.multiple_of(start * GRID_W, NA_QROWS * GRID_W)
    span = NA_SPAN * GRID_W
    tq = q_ref.shape[1]
    qt = q_ref[0].astype(F32).T.astype(BF16)
    kw = k_ref[0, pl.ds(off, span), :]
    kc = kc_ref[0]
    ones_w = (lax.broadcasted_iota(jnp.int32, (ACC_ROWS - HEAD_DIM, span), 0) == 0).astype(BF16)
    ones_c = (lax.broadcasted_iota(jnp.int32, (ACC_ROWS - HEAD_DIM, kc.shape[0]), 0) == 0).astype(BF16)
    outs = []
    for h in range(GROUP_HEADS):
        r0 = h * HEAD_DIM
        pieces = []
        if r0 > 0:
            pieces.append(jnp.zeros((r0, tq), BF16))
        pieces.append(qt[r0:r0 + HEAD_DIM])
        if r0 + HEAD_DIM < GROUP_W:
            pieces.append(jnp.zeros((GROUP_W - r0 - HEAD_DIM, tq), BF16))
        w = jnp.concatenate(pieces, axis=0)
        s_w = jnp.dot(kw, w, preferred_element_type=F32) + bias_ref[0, h]
        s_c = jnp.dot(kc, w, preferred_element_type=F32)
        m = jnp.maximum(jnp.max(s_w, axis=0, keepdims=True), jnp.max(s_c, axis=0, keepdims=True))
        p_w = jnp.exp2(s_w - m).astype(BF16)
        p_c = jnp.exp2(s_c - m).astype(BF16)
        vw = jnp.concatenate([vt_ref[0, pl.ds(r0, HEAD_DIM), pl.ds(off, span)], ones_w], axis=0)
        vc = jnp.concatenate([vtc_ref[0, r0:r0 + HEAD_DIM, :], ones_c], axis=0)
        acc = jnp.dot(vw, p_w, preferred_element_type=F32) + jnp.dot(vc, p_c, preferred_element_type=F32)
        outs.append(acc[:HEAD_DIM] / acc[HEAD_DIM:HEAD_DIM + 1])
    o_ref[0] = jnp.concatenate(outs, axis=0).T


def _na_bias_tables(rel_bias, grid_rows):
    half_r, half_c = NA_ROWS // 2, NA_COLS // 2
    kcol = np.arange(GRID_W)[:, None]
    qcol = np.arange(GRID_W)[None, :]
    cs = np.clip(qcol - half_c, 0, GRID_W - NA_COLS)
    valid_c = (kcol >= cs) & (kcol < cs + NA_COLS)
    cidx = np.clip(kcol - qcol + NA_COLS - 1, 0, 2 * NA_COLS - 2)
    col_sel = jnp.asarray(cidx[:, :, None] == np.arange(2 * NA_COLS - 1), F32)
    kr_l = np.arange(NA_SPAN)[:, None]
    qr_l = np.arange(NA_QROWS)[None, :]
    tabs = []
    for r0 in (0, NA_QROWS * (grid_rows // NA_QROWS // 2), grid_rows - NA_QROWS):
        start = int(np.clip(r0 - half_r, 0, grid_rows - NA_SPAN))
        kr = start + kr_l
        qr = r0 + qr_l
        rs = np.clip(qr - half_r, 0, grid_rows - NA_ROWS)
        valid_r = (kr >= rs) & (kr < rs + NA_ROWS)
        ridx = np.clip(kr - qr + NA_ROWS - 1, 0, 2 * NA_ROWS - 2)
        row_sel = jnp.asarray(ridx[:, :, None] == np.arange(2 * NA_ROWS - 1), F32)
        t = jnp.einsum('kqr,hrc,xyc->hkxqy', row_sel, rel_bias.astype(F32), col_sel,
                       precision=lax.Precision.HIGHEST) * LOG2E
        valid = valid_r[:, None, :, None] & valid_c[None, :, None, :]
        t = jnp.where(valid[None], t, MASK_VALUE)
        tabs.append(t.reshape(rel_bias.shape[0], NA_SPAN * GRID_W, NA_QROWS * GRID_W))
    return jnp.stack(tabs, axis=0)


def _neighborhood_attention(q, k, vt, k_ctx, vt_ctx, bias_tabs):
    b, s, c = q.shape
    lc = k_ctx.shape[1]
    grid_rows = s // GRID_W
    assert grid_rows >= NA_SPAN and grid_rows % NA_QROWS == 0
    nblk = grid_rows // NA_QROWS
    tq = NA_QROWS * GRID_W

    def bias_idx(bi, i):
        return (jnp.where(i == 0, 0, jnp.where(i == nblk - 1, 2, 1)), 0, 0, 0)

    return pl.pallas_call(
        functools.partial(_na_kernel, grid_rows=grid_rows),
        out_shape=jax.ShapeDtypeStruct((b, s, c), F32),
        grid=(b, nblk),
        in_specs=[pl.BlockSpec((1, tq, c), lambda bi, i: (bi, i, 0)),
                  pl.BlockSpec((1, s, c), lambda bi, i: (bi, 0, 0)),
                  pl.BlockSpec((1, c, s), lambda bi, i: (bi, 0, 0)),
                  pl.BlockSpec((1, lc, c), lambda bi, i: (bi, 0, 0)),
                  pl.BlockSpec((1, c, lc), lambda bi, i: (bi, 0, 0)),
                  pl.BlockSpec((1, GROUP_HEADS, NA_SPAN * GRID_W, tq), bias_idx)],
        out_specs=pl.BlockSpec((1, tq, c), lambda bi, i: (bi, i, 0)),
        compiler_params=_cparams(("arbitrary", "arbitrary")),
        name="neighborhood_attention",
    )(q, k, vt, k_ctx, vt_ctx, bias_tabs)


SCAN_BATCH = SUBLANES
N_KQ = 5
FWD_W = (N_KQ + 1) * GROUP_W
BWD_W = (N_KQ - 1) * GROUP_W
V_SPLIT = 4
V_TILES = HEAD_DIM // (V_SPLIT * SUBLANES)
KT_BUFFERS = 4


def _scan_kernel(zf_ref, zb_ref, zrv_ref, s0_ref, yf_ref, yb_ref, sfin_ref, st_ref, sa_ref, *kt_refs, steps):
    i = pl.program_id(1)

    @pl.when(i == 0)
    def _():
        st_ref[...] = s0_ref[0]

    lane = lax.broadcasted_iota(jnp.int32, (SUBLANES, LANES), 1)
    lane_vq = lane // (LANES // V_SPLIT)
    chan_vq = (lane // (V_TILES * SUBLANES)) % V_SPLIT

    def operand_tile(q, s):
        sb = steps - 1 - s
        pieces = []
        for hp in range(2):
            pieces.append(zf_ref[s, :, pl.ds(q * GROUP_W + hp * LANES, LANES)])
        for hp in range(2):
            if q < N_KQ - 1:
                pieces.append(zb_ref[sb, :, pl.ds(q * GROUP_W + hp * LANES, LANES)])
            else:
                pieces.append(zrv_ref[sb, :, pl.ds((q - (N_KQ - 1)) * GROUP_W + hp * LANES, LANES)])
        tile = jnp.concatenate(pieces * V_SPLIT, axis=0)
        return tile.T

    def prepare(kt_ref, s):
        for q in range(N_KQ + 1):
            kt_ref[q] = operand_tile(q, s)

    nacc = 4

    def tree(parts):
        return (parts[0] + parts[1]) + (parts[2] + parts[3])

    prepare(kt_refs[0], 0)
    prepare(kt_refs[1], 1)
    for hpar in range(2):
        for vb in range(V_TILES):
            parts = [None] * nacc
            for k in range(HEAD_DIM):
                term = st_ref[hpar, k, vb] * kt_refs[0][0, pl.ds(hpar * HEAD_DIM + k, 1), :]
                parts[k % nacc] = term if parts[k % nacc] is None else parts[k % nacc] + term
            sa_ref[hpar, vb] = -tree(parts)

    def step(s, kt_ref, nxt_ref, far_ref):
        prepare(far_ref, jnp.minimum(s + 2, steps - 1))
        vt = kt_ref[N_KQ]
        ys = []
        for hpar in range(2):
            base = hpar * HEAD_DIM
            vops = []
            for vb in range(V_TILES):
                acc = None
                for vq in range(V_SPLIT):
                    r0 = base + vq * V_TILES * SUBLANES + vb * SUBLANES
                    piece = vt[r0:r0 + SUBLANES, :]
                    acc = piece if acc is None else jnp.where(lane_vq == vq, piece, acc)
                vops.append(acc)
            sas = [sa_ref[hpar, vb] for vb in range(V_TILES)]
            yp = [[None] * nacc for _ in range(V_TILES)]
            sp = [[None] * nacc for _ in range(V_TILES)]
            for k in range(HEAD_DIM):
                row = pl.ds(base + k, 1)
                w = kt_ref[1, row, :]
                ka = kt_ref[2, row, :]
                kd = kt_ref[3, row, :]
                rr = kt_ref[4, row, :]
                kn = nxt_ref[0, row, :]
                j = k % nacc
                for vb in range(V_TILES):
                    sn = st_ref[hpar, k, vb] * w + sas[vb] * ka + vops[vb] * kd
                    st_ref[hpar, k, vb] = sn
                    ty = sn * rr
                    yp[vb][j] = ty if yp[vb][j] is None else yp[vb][j] + ty
                    tn = sn * kn
                    sp[vb][j] = tn if sp[vb][j] is None else sp[vb][j] + tn
            for vb in range(V_TILES):
                sa_ref[hpar, vb] = -tree(sp[vb])
            ys.append([tree(yp[vb]) for vb in range(V_TILES)])
        ytile = jnp.concatenate([ys[hpar][vb] for hpar in range(2) for _ in range(V_SPLIT) for vb in range(V_TILES)],
                                axis=0)
        yt = ytile.T
        sb = steps - 1 - s
        for d in range(2):
            for hp in range(2):
                acc = None
                for vq in range(V_SPLIT):
                    r0 = vq * (LANES // V_SPLIT) + d * 2 * SUBLANES + hp * SUBLANES
                    piece = yt[r0:r0 + SUBLANES, :]
                    acc = piece if acc is None else jnp.where(chan_vq == vq, piece, acc)
                if d == 0:
                    yf_ref[s, :, pl.ds(hp * LANES, LANES)] = acc
                else:
                    yb_ref[sb, :, pl.ds(hp * LANES, LANES)] = acc

    nbuf = len(kt_refs)

    def body(j, carry):
        for u in range(nbuf):
            step(nbuf * j + u, kt_refs[u], kt_refs[(u + 1) % nbuf], kt_refs[(u + 2) % nbuf])
        return carry

    lax.fori_loop(0, steps // nbuf, body, 0)

    @pl.when(i == pl.num_programs(1) - 1)
    def _():
        sfin_ref[0] = st_ref[...]


def _rwkv_scan(zf, zb, s0):
    t_len, b, _ = zf.shape
    groups = b // SCAN_BATCH
    ts = SCAN_STEPS
    nblk = t_len // ts
    st_shape = (2, HEAD_DIM, V_TILES, SUBLANES, LANES)
    return pl.pallas_call(
        functools.partial(_scan_kernel, steps=ts),
        out_shape=(jax.ShapeDtypeStruct((t_len, b, GROUP_W), F32),
                   jax.ShapeDtypeStruct((t_len, b, GROUP_W), F32),
                   jax.ShapeDtypeStruct((groups,) + st_shape, F32)),
        grid=(groups, nblk),
        in_specs=[pl.BlockSpec((ts, SCAN_BATCH, FWD_W), lambda g, i: (i, g, 0)),
                  pl.BlockSpec((ts, SCAN_BATCH, BWD_W), lambda g, i: (nblk - 1 - i, g, 0)),
                  pl.BlockSpec((ts, SCAN_BATCH, 2 * GROUP_W), lambda g, i: (nblk - 1 - i, g, (N_KQ - 1) // 2)),
                  pl.BlockSpec((1,) + st_shape, lambda g, i: (g, 0, 0, 0, 0, 0))],
        out_specs=(pl.BlockSpec((ts, SCAN_BATCH, GROUP_W), lambda g, i: (i, g, 0)),
                   pl.BlockSpec((ts, SCAN_BATCH, GROUP_W), lambda g, i: (nblk - 1 - i, g, 0)),
                   pl.BlockSpec((1,) + st_shape, lambda g, i: (g, 0, 0, 0, 0, 0))),
        scratch_shapes=[pltpu.VMEM(st_shape, F32), pltpu.VMEM((2, V_TILES, SUBLANES, LANES), F32)]
        + [pltpu.VMEM((N_KQ + 1, LANES, LANES), F32)] * KT_BUFFERS,
        compiler_params=_cparams(("arbitrary", "arbitrary")),
        name="rwkv_scan",
    )(zf, zb, zf, s0)


PREP_STEPS = 32
DECAY_RATE = math.exp(-0.5)


def _prep_kernel(u_ref, up_ref, un_ref, mu_ref, w0_ref, wup_ref, a0_ref, aup_ref, gup_ref, kk_ref, ka_ref, rk_ref,
                 seg_ref, zf_ref, zb_ref, gb_ref):
    i = pl.program_id(1)
    last = pl.num_programs(1) - 1
    tb = u_ref.shape[0]
    u = u_ref[...]
    before = jnp.where(i > 0, up_ref[...], 0.0)
    after = jnp.where(i < last, un_ref[...], 0.0)
    prev = jnp.concatenate([before, u[:-1]], axis=0)
    nxt = jnp.concatenate([u[1:], after], axis=0)
    mu = mu_ref[...]
    x = (u + mu[0:1] * (prev - u) + mu[1:2] * (nxt - u)).reshape(tb * SCAN_BATCH, A_IN)
    gw = GROUP_W
    r = x[:, 0:gw]
    k = x[:, gw:2 * gw]
    v = x[:, 2 * gw:3 * gw]
    wd = x[:, 3 * gw:3 * gw + 2 * LORA_W]
    ad = x[:, 3 * gw + 2 * LORA_W:3 * gw + 2 * LORA_W + 2 * LORA_A]
    gd = x[:, 3 * gw + 2 * LORA_W + 2 * LORA_A:]
    seg = seg_ref[...]
    w_raw = w0_ref[...] + jnp.dot(jnp.tanh(wd).astype(BF16), wup_ref[...].astype(BF16), preferred_element_type=F32)
    decay = jnp.exp(-DECAY_RATE * jax.nn.sigmoid(w_raw))
    a = jax.nn.sigmoid(a0_ref[...] + jnp.dot(ad.astype(BF16), aup_ref[...].astype(BF16),
                                             preferred_element_type=F32))
    g = jnp.dot(jax.nn.sigmoid(gd).astype(BF16), gup_ref[...].astype(BF16), preferred_element_type=F32)
    bonus = jnp.zeros_like(r)
    for d, z_ref in enumerate((zf_ref, zb_ref)):
        sl = slice(d * gw, (d + 1) * gw)
        kk = k * kk_ref[:, sl]
        a_d = a[:, sl]
        kd = k * (1.0 + (a_d - 1.0) * ka_ref[:, sl])
        kk = kk * lax.rsqrt(_head_sums(kk * kk, seg) + 1e-12)
        bonus = bonus + _head_sums(r * kd * rk_ref[:, sl], seg) * v
        for q, val in enumerate((kk, decay[:, sl], kk * a_d, kd)):
            z_ref[:, :, q * gw:(q + 1) * gw] = val.reshape(tb, SCAN_BATCH, gw)
    zf_ref[:, :, 4 * gw:5 * gw] = r.reshape(tb, SCAN_BATCH, gw)
    zf_ref[:, :, 5 * gw:6 * gw] = v.reshape(tb, SCAN_BATCH, gw)
    gb_ref[:, :, 0:gw] = g.reshape(tb, SCAN_BATCH, gw)
    gb_ref[:, :, gw:2 * gw] = bonus.reshape(tb, SCAN_BATCH, gw)


def _block_diag(w):
    z = jnp.zeros_like(w[0])
    return jnp.concatenate([jnp.concatenate([w[0], z], axis=1), jnp.concatenate([z, w[1]], axis=1)], axis=0)


def _rwkv_prepare(ua_tm, mu, w0, w_up, a0, a_up, g_up, k_k, k_a, r_k):
    t, b, _ = ua_tm.shape
    tb = min(PREP_STEPS, t)
    nblk = t // tb
    gw2 = 2 * GROUP_W

    def full(shape):
        return pl.BlockSpec(shape, lambda g, i: (0,) * len(shape))

    return pl.pallas_call(
        _prep_kernel,
        out_shape=(jax.ShapeDtypeStruct((t, b, FWD_W), F32), jax.ShapeDtypeStruct((t, b, BWD_W), F32),
                   jax.ShapeDtypeStruct((t, b, gw2), F32)),
        grid=(b // SCAN_BATCH, nblk),
        in_specs=[pl.BlockSpec((tb, SCAN_BATCH, A_IN), lambda g, i: (i, g, 0)),
                  pl.BlockSpec((1, SCAN_BATCH, A_IN), lambda g, i: (jnp.maximum(i * tb - 1, 0), g, 0)),
                  pl.BlockSpec((1, SCAN_BATCH, A_IN), lambda g, i: (jnp.minimum((i + 1) * tb, t - 1), g, 0)),
                  full((2, A_IN)), full((1, gw2)), full((2 * LORA_W, gw2)), full((1, gw2)), full((2 * LORA_A, gw2)),
                  full((LORA_G, GROUP_W)), full((1, gw2)), full((1, gw2)), full((1, gw2)),
                  full((GROUP_W, GROUP_W))],
        out_specs=(pl.BlockSpec((tb, SCAN_BATCH, FWD_W), lambda g, i: (i, g, 0)),
                   pl.BlockSpec((tb, SCAN_BATCH, BWD_W), lambda g, i: (i, g, 0)),
                   pl.BlockSpec((tb, SCAN_BATCH, gw2), lambda g, i: (i, g, 0))),
        compiler_params=_cparams(("arbitrary", "arbitrary")),
        name="rwkv_prepare",
    )(ua_tm, ua_tm, ua_tm, mu, w0.reshape(1, gw2), _block_diag(w_up), a0.reshape(1, gw2), _block_diag(a_up), g_up,
      k_k.reshape(1, gw2), k_a.reshape(1, gw2), r_k.reshape(1, gw2), _head_segments())


def _rwkv_time_mix(ua_tm, s0, mu, w0, w_up, a0, a_up, g_up, k_k, k_a, r_k):
    t, b, _ = ua_tm.shape
    h, n = GROUP_HEADS, HEAD_DIM
    bp = -(-b // SCAN_BATCH) * SCAN_BATCH
    groups = bp // SCAN_BATCH
    if bp != b:
        ua_tm = jnp.pad(ua_tm, ((0, 0), (0, bp - b), (0, 0)))
    zf, zb, gb = _rwkv_prepare(ua_tm, mu, w0, w_up, a0, a_up, g_up, k_k, k_a, r_k)
    if s0 is None:
        s0l = jnp.zeros((groups, 2, n, V_TILES, SUBLANES, LANES), F32)
    else:
        s0 = jnp.pad(s0.astype(F32), ((0, bp - b),) + ((0, 0),) * 4)
        s0l = s0.reshape(groups, SCAN_BATCH, 2, 2, 2, V_SPLIT, V_TILES, SUBLANES, n)
        s0l = jnp.transpose(s0l, (0, 4, 8, 6, 7, 5, 2, 3, 1)).reshape(groups, 2, n, V_TILES, SUBLANES, LANES)
    yf, yb, s_fin = _rwkv_scan(zf, zb, s0l)
    s_fin = s_fin.reshape(groups, 2, n, V_TILES, SUBLANES, V_SPLIT, 2, 2, SCAN_BATCH)
    s_fin = jnp.transpose(s_fin, (0, 8, 6, 7, 1, 5, 3, 4, 2)).reshape(bp, 2, h, n, n)[:b]
    return yf, yb, gb, s_fin


def _expert_kernel(be_ref, nu_ref, x_ref, w1_ref, w3_ref, w2_ref, o_ref, w1s, w3s, w2s):
    i = pl.program_id(0)
    e = be_ref[i]
    prev = be_ref[jnp.maximum(i - 1, 0)]

    @pl.when((i == 0) | (e != prev))
    def _():
        w1s[...] = w1_ref[0].astype(BF16)
        w3s[...] = w3_ref[0].astype(BF16)
        w2s[...] = w2_ref[0].astype(BF16)

    @pl.when(i < nu_ref[0])
    def _():
        x = x_ref[...]
        a = jnp.dot(x, w1s[...], preferred_element_type=F32)
        g = jnp.dot(x, w3s[...], preferred_element_type=F32)
        hmid = (a * jax.nn.sigmoid(a)) * g
        o_ref[...] = jnp.dot(hmid.astype(BF16), w2s[...], preferred_element_type=F32)

    @pl.when(i >= nu_ref[0])
    def _():
        o_ref[...] = jnp.zeros_like(o_ref)


def _expert_mlp(xb, block_e, n_used, w1, w3, w2):
    cap, d = xb.shape
    bm = MOE_TILE
    de = w1.shape[-1]
    return pl.pallas_call(
        _expert_kernel,
        out_shape=jax.ShapeDtypeStruct((cap, d), F32),
        grid_spec=pltpu.PrefetchScalarGridSpec(
            num_scalar_prefetch=2,
            grid=(cap // bm,),
            in_specs=[pl.BlockSpec((bm, d), lambda i, be, nu: (i, 0)),
                      pl.BlockSpec((1, d, de), lambda i, be, nu: (be[i], 0, 0)),
                      pl.BlockSpec((1, d, de), lambda i, be, nu: (be[i], 0, 0)),
                      pl.BlockSpec((1, de, d), lambda i, be, nu: (be[i], 0, 0))],
            out_specs=pl.BlockSpec((bm, d), lambda i, be, nu: (i, 0)),
            scratch_shapes=[pltpu.VMEM((d, de), BF16), pltpu.VMEM((d, de), BF16), pltpu.VMEM((de, d), BF16)]),
        compiler_params=_cparams(("arbitrary",)),
        name="expert_mlp",
    )(block_e, n_used, xb, w1, w3, w2)


def _hier_moe(h_bf, logits, w1, w3, w2):
    n, d = h_bf.shape
    bm = MOE_TILE
    g_logits = logits[:, :N_EXPERT_GROUPS]
    g_idx = jnp.argmax(g_logits, axis=-1)
    g_top = jnp.max(jax.nn.softmax(g_logits, axis=-1), axis=-1)
    e_logits = logits[:, N_EXPERT_GROUPS:N_EXPERT_GROUPS + N_EXPERTS].reshape(n, N_EXPERT_GROUPS, EXPERTS_PER_GROUP)
    e_logits = jnp.take_along_axis(e_logits, g_idx[:, None, None], axis=1)[:, 0]
    top_v, top_i = lax.top_k(e_logits, TOP_K)
    gate = jax.nn.softmax(top_v, axis=-1) * g_top[:, None]
    eid = (g_idx[:, None] * EXPERTS_PER_GROUP + top_i).reshape(-1).astype(jnp.int32)
    tok = jnp.repeat(jnp.arange(n, dtype=jnp.int32), TOP_K)
    n_assign = n * TOP_K
    onehot = (eid[:, None] == jnp.arange(N_EXPERTS, dtype=jnp.int32)[None, :]).astype(jnp.int32)
    csum = jnp.cumsum(onehot, axis=0)
    counts = csum[-1]
    rank = jnp.take_along_axis(csum, eid[:, None], axis=1)[:, 0] - 1
    padded = (counts + bm - 1) // bm * bm
    pad_end = jnp.cumsum(padded)
    pad_start = pad_end - padded
    dest = pad_start[eid] + rank
    n_blocks = -(-(n_assign + N_EXPERTS * (bm - 1)) // bm)
    cap = n_blocks * bm
    buf_tok = jnp.zeros((cap,), jnp.int32).at[dest].set(tok)
    block_start = jnp.arange(n_blocks, dtype=jnp.int32) * bm
    block_e = jnp.minimum(jnp.sum((pad_end[None, :] <= block_start[:, None]).astype(jnp.int32), axis=1),
                          N_EXPERTS - 1).astype(jnp.int32)
    n_used = (pad_end[-1:] // bm).astype(jnp.int32)
    xb = h_bf[buf_tok]
    yb = _expert_mlp(xb, block_e, n_used, w1, w3, w2)
    d2 = dest.reshape(n, TOP_K)
    return yb[d2[:, 0]] * gate[:, 0:1] + yb[d2[:, 1]] * gate[:, 1:2]


def _rms(x, g):
    return x * lax.rsqrt(jnp.mean(x * x, axis=-1, keepdims=True) + NORM_EPS) * g


def _axial_rope(t_len, dim):
    q4 = dim // 4
    inv = ROPE_THETA ** (-jnp.arange(q4, dtype=F32) / q4)
    t = jnp.arange(t_len)
    row = (t // GRID_W).astype(F32)
    col = (t % GRID_W).astype(F32)
    ang = jnp.stack([row[:, None] * inv, col[:, None] * inv], axis=1)
    return jnp.cos(ang), jnp.sin(ang)


def _token_mixers(ua_tm, ub, uc, ud, rope, ctx, l, W):
    b, t, _ = ub.shape
    nh, n = GROUP_HEADS, HEAD_DIM
    latent = ctx is not None

    yf, yb, gb, s_fin = _rwkv_time_mix(ua_tm, ctx[0] if latent else None, W['rw_shift'][l], W['rw_w0'][l],
                                       W['rw_w_up'][l], W['rw_a0'][l], W['rw_a_up'][l], W['rw_g_up'][l],
                                       W['rw_k_k'][l], W['rw_k_a'][l], W['rw_r_k'][l])

    kvw = GQA_KV_HEADS * n
    if latent:
        qb, kb, vbt, qc, kc, vct, qd, kd, vdt = _attention_prepare(
            ub.reshape(b * t, -1), uc.reshape(b * t, -1), ud.reshape(b * t, -1), rope,
            W['diff_qk_norm'][l], W['na_qk_norm'][l], W['gqa_qk_norm'][l], t)

    lam_init = 0.8 - 0.6 * math.exp(-0.3 * l)
    lv = W['diff_lambda'][l].astype(F32)
    lam = jnp.exp(jnp.sum(lv[0] * lv[1])) - jnp.exp(jnp.sum(lv[2] * lv[3])) + lam_init
    scalars = jnp.stack([lam, jnp.asarray(1.0 - lam_init, F32)]).astype(F32)
    if latent:
        k_ctx = jnp.transpose(ctx[1], (0, 3, 1, 2, 4)).reshape(b, -1, GROUP_W).astype(BF16)
        vt_ctx = jnp.transpose(ctx[2], (0, 1, 3, 2)).reshape(b, GROUP_W, -1).astype(BF16)
        out_b = _flash_attention(qb.reshape(b, t, GROUP_W), kb.reshape(b, t, GROUP_W), vbt, DIFF_HEADS, scalars,
                                 W['diff_subln'][l], k_ctx=k_ctx, vt_ctx=vt_ctx)
    else:
        qb, kb, vb = jnp.split(ub, 3, axis=-1)
        qb = _rms(qb.reshape(b, t, nh, 2, DIFF_DQK), W['diff_qk_norm'][l, 0])
        kb = _rms(kb.reshape(b, t, nh, 2, DIFF_DQK), W['diff_qk_norm'][l, 1])
        out_b = _flash_attention((qb * (LOG2E / math.sqrt(DIFF_DQK))).reshape(b, t, GROUP_W).astype(BF16),
                                 kb.reshape(b, t, GROUP_W).astype(BF16), jnp.swapaxes(vb, 1, 2).astype(BF16),
                                 DIFF_HEADS, scalars, W['diff_subln'][l])

    if latent:
        k_ctx = jnp.transpose(ctx[3], (0, 2, 1, 3)).reshape(b, -1, GROUP_W).astype(BF16)
        vt_ctx = jnp.transpose(ctx[4], (0, 1, 3, 2)).reshape(b, GROUP_W, -1).astype(BF16)
        out_c = _neighborhood_attention(qc.reshape(b, t, GROUP_W), kc.reshape(b, t, GROUP_W), vct, k_ctx, vt_ctx,
                                        _na_bias_tables(W['na_rel_bias'][l], t // GRID_W))
    else:
        qc, kc, vc = jnp.split(uc, 3, axis=-1)
        qc = _rms(qc.reshape(b, t, nh, n), W['na_qk_norm'][l, 0])
        kc = _rms(kc.reshape(b, t, nh, n), W['na_qk_norm'][l, 1])
        out_c = _flash_attention((qc * (LOG2E / math.sqrt(n))).reshape(b, t, GROUP_W).astype(BF16),
                                 kc.reshape(b, t, GROUP_W).astype(BF16), jnp.swapaxes(vc, 1, 2).astype(BF16),
                                 DENSE_HEADS)

    if latent:
        k_ctx = jnp.transpose(ctx[5], (0, 2, 1, 3)).reshape(b, -1, kvw).astype(BF16)
        vt_ctx = jnp.transpose(ctx[6], (0, 1, 3, 2)).reshape(b, kvw, -1).astype(BF16)
        out_d = _flash_attention(qd.reshape(b, t, GROUP_W), kd.reshape(b, t, kvw), vdt, GQA_HEADS,
                                 k_ctx=k_ctx, vt_ctx=vt_ctx)
    else:
        qd, kd, vd = jnp.split(ud, [GROUP_W, GROUP_W + kvw], axis=-1)
        qd = _rms(qd.reshape(b, t, nh, n), W['gqa_qk_norm'][l, 0])
        kd = _rms(kd.reshape(b, t, GQA_KV_HEADS, n), W['gqa_qk_norm'][l, 1])
        out_d = _flash_attention((qd * (LOG2E / math.sqrt(n))).reshape(b, t, GROUP_W).astype(BF16),
                                 kd.reshape(b, t, kvw).astype(BF16), jnp.swapaxes(vd, 1, 2).astype(BF16), GQA_HEADS)

    mix = (yf, yb, gb) + tuple(z.reshape(b * t, GROUP_W) for z in (out_b, out_c, out_d))
    if latent:
        return mix, None
    new_ctx = (s_fin, jnp.transpose(kb, (0, 2, 3, 1, 4)), vb.reshape(b, t, nh, n).transpose(0, 2, 1, 3),
               kc.transpose(0, 2, 1, 3), vc.reshape(b, t, nh, n).transpose(0, 2, 1, 3),
               kd.transpose(0, 2, 1, 3), vd.reshape(b, t, GQA_KV_HEADS, n).transpose(0, 2, 1, 3))
    return mix, new_ctx


def _mixing_sublayer(x, cond, rope, ctx, l, W):
    b, t, d = x.shape
    n = b * t
    mod = _modulation(cond, W['w_mod'][l], W['b_mod'][l])
    sh1, sc1, g1, sh2, sc2, g2 = jnp.split(mod, 6, axis=-1)
    x2 = x.reshape(n, d)
    ua, ub, uc, ud = _in_projection(x2, W['norm_mix'][l], 1.0 + sc1, sh1, W['w_in'][l], t)
    mix, new_ctx = _token_mixers(ua, ub.reshape(b, t, -1), uc.reshape(b, t, -1), ud.reshape(b, t, -1), rope, ctx, l, W)
    w_router = jnp.concatenate([W['moe_w_group'][l], W['moe_w_router'][l]], axis=1)
    w_router = jnp.pad(w_router, ((0, 0), (0, ROUTER_W - w_router.shape[1])))
    b_router = jnp.concatenate([W['moe_b_group'][l], W['moe_b_router'][l]])
    b_router = jnp.pad(b_router, (0, ROUTER_W - b_router.shape[0])).reshape(1, ROUTER_W)
    x2, h2, logits = _out_projection(*mix, W['rw_gn_w'][l], W['rw_gn_b'][l], W['w_out'][l], x2, g1,
                                     W['norm_ffn'][l], 1.0 + sc2, sh2, w_router, b_router, t)
    g2r = jnp.broadcast_to(g2[:, None, :], (g2.shape[0], n // g2.shape[0], d)).reshape(n, d)
    return x2, h2, logits, g2r, new_ctx


def kernel(x_prompt, x_sample, c, state_rwkv, cache_diff_k, cache_diff_v, cache_na_k, cache_na_v,
           cache_gqa_k, cache_gqa_v, c_ctx, norm_mix, norm_ffn, w_mod, b_mod, w_in, w_out,
           rw_shift, rw_w0, rw_w_up, rw_a0, rw_a_up, rw_g_up, rw_k_k, rw_k_a, rw_r_k, rw_gn_w, rw_gn_b,
           diff_qk_norm, diff_lambda, diff_subln, na_qk_norm, na_rel_bias, gqa_qk_norm,
           moe_w_group, moe_b_group, moe_w_router, moe_b_router, moe_w1, moe_w3, moe_w2):
    W = {
        'norm_mix': norm_mix, 'norm_ffn': norm_ffn, 'w_mod': w_mod, 'b_mod': b_mod,
        'w_in': w_in, 'w_out': w_out, 'rw_shift': rw_shift, 'rw_w0': rw_w0, 'rw_w_up': rw_w_up,
        'rw_a0': rw_a0, 'rw_a_up': rw_a_up, 'rw_g_up': rw_g_up, 'rw_k_k': rw_k_k, 'rw_k_a': rw_k_a,
        'rw_r_k': rw_r_k, 'rw_gn_w': rw_gn_w, 'rw_gn_b': rw_gn_b, 'diff_qk_norm': diff_qk_norm,
        'diff_lambda': diff_lambda, 'diff_subln': diff_subln, 'na_qk_norm': na_qk_norm,
        'na_rel_bias': na_rel_bias, 'gqa_qk_norm': gqa_qk_norm, 'moe_w_group': moe_w_group,
        'moe_b_group': moe_b_group, 'moe_w_router': moe_w_router, 'moe_b_router': moe_b_router,
        'moe_w1': moe_w1, 'moe_w3': moe_w3, 'moe_w2': moe_w2,
    }
    depth = w_in.shape[0]

    def run_pass(x, cond, rope, caches):
        new_ctx = []
        for l in range(depth):
            ctx = None if caches is None else tuple(z[:, l] for z in caches)
            r2, h2, logits, g2r, nc = _mixing_sublayer(x, cond, rope, ctx, l, W)
            y = _hier_moe(h2, logits, W['moe_w1'][l], W['moe_w3'][l], W['moe_w2'][l])
            x = (r2 + g2r * y).reshape(x.shape)
            new_ctx.append(nc)
        return x, new_ctx

    xp, ctx_layers = run_pass(x_prompt, c_ctx[None, :], None, None)
    new_caches = tuple(jnp.stack([z[i] for z in ctx_layers], axis=1) for i in range(7))

    t_lat = x_sample.shape[1]
    rope = (_rope_tables(_axial_rope(t_lat, DIFF_DQK), DIFF_DQK, GROUP_W),
            _rope_tables(_axial_rope(t_lat, HEAD_DIM), HEAD_DIM, GROUP_W))
    xs, _ = run_pass(x_sample, c, rope, (state_rwkv, cache_diff_k, cache_diff_v, cache_na_k, cache_na_v,
                                        cache_gqa_k, cache_gqa_v))
    return (xp, xs) + new_caches
```

```python
import functools
import math

import numpy as np
import jax
import jax.numpy as jnp
from jax import lax
from jax.experimental import pallas as pl
from jax.experimental.pallas import tpu as pltpu

F32 = jnp.float32
BF16 = jnp.bfloat16

D_MODEL = 1024
GRID_W = 64
HEAD_DIM = 64
GROUP_W = 256
GROUP_HEADS = 4
LORA_W = 64
LORA_A = 64
LORA_G = 128
RWKV_GN_EPS = 64e-5
DIFF_DQK = 32
GQA_KV_HEADS = 2
GQA_GROUP = 2
NA_ROWS = 8
NA_COLS = 16
ROPE_THETA = 10000.0
NORM_EPS = 1e-6
N_EXPERT_GROUPS = 4
EXPERTS_PER_GROUP = 8
N_EXPERTS = 32
TOP_K = 2
D_EXPERT = 512
A_IN = 3 * GROUP_W + 2 * LORA_W + 2 * LORA_A + LORA_G
B_IN = 3 * GROUP_W
C_IN = 3 * GROUP_W
D_IN = GROUP_W + 2 * GQA_KV_HEADS * HEAD_DIM
IN_W = A_IN + B_IN + C_IN + D_IN

LANES = 128
SUBLANES = 8
VMEM_LIMIT = 48 * 1024 * 1024

ROW_TILE = 256
Q_TILE = 512
SCAN_STEPS = 32
KV_CHUNK = 512
MOE_TILE = 256
ROUTER_W = 128
MASK_VALUE = -1e30
LOG2E = 1.4426950408889634


def _cparams(sem):
    return pltpu.CompilerParams(dimension_semantics=sem, vmem_limit_bytes=VMEM_LIMIT)


def _mod_kernel(c_ref, w_ref, b_ref, o_ref):
    c = c_ref[...]
    a = c * jax.nn.sigmoid(c)
    o_ref[...] = jnp.dot(a, w_ref[...], preferred_element_type=F32,
                         precision=lax.Precision.HIGHEST) + b_ref[...]


def _modulation(cond, w, b):
    m, d = cond.shape
    n = w.shape[1]
    mp = -(-m // SUBLANES) * SUBLANES
    cp = jnp.pad(cond, ((0, mp - m), (0, 0)))
    tn = 768
    out = pl.pallas_call(
        _mod_kernel,
        out_shape=jax.ShapeDtypeStruct((mp, n), F32),
        grid=(n // tn,),
        in_specs=[pl.BlockSpec((mp, d), lambda j: (0, 0)),
                  pl.BlockSpec((d, tn), lambda j: (0, j)),
                  pl.BlockSpec((1, tn), lambda j: (0, j))],
        out_specs=pl.BlockSpec((mp, tn), lambda j: (0, j)),
        compiler_params=_cparams(("arbitrary",)),
        name="modulation",
    )(cp, w, b.reshape(1, n))
    return out[:m]


def _inproj_kernel(x_ref, g_ref, sc_ref, sh_ref, w_ref, oa_ref, ob_ref, oc_ref, od_ref):
    x = x_ref[...]
    ms = jnp.mean(x * x, axis=-1, keepdims=True)
    h = x * lax.rsqrt(ms + NORM_EPS) * g_ref[...]
    h = h * sc_ref[0] + sh_ref[0]
    u = jnp.dot(h.astype(BF16), w_ref[...].astype(BF16), preferred_element_type=F32)
    oa_ref[...] = u[:, :A_IN]
    ob_ref[...] = u[:, A_IN:A_IN + B_IN]
    oc_ref[...] = u[:, A_IN + B_IN:A_IN + B_IN + C_IN]
    od_ref[...] = u[:, A_IN + B_IN + C_IN:]


def _in_projection(x2, gain, scale1p, shift, w_in, t_len):
    n, d = x2.shape
    bm = scale1p.shape[0]
    tm = ROW_TILE
    per_seq = t_len // tm
    b = n // t_len

    def mod_idx(i):
        return ((i // per_seq) if bm > 1 else 0, 0, 0)

    widths = (B_IN, C_IN, D_IN)
    ua, ub, uc, ud = pl.pallas_call(
        _inproj_kernel,
        out_shape=(jax.ShapeDtypeStruct((t_len, b * A_IN), F32),)
        + tuple(jax.ShapeDtypeStruct((n, w), F32) for w in widths),
        grid=(n // tm,),
        in_specs=[pl.BlockSpec((tm, d), lambda i: (i, 0)),
                  pl.BlockSpec((1, d), lambda i: (0, 0)),
                  pl.BlockSpec((1, 1, d), mod_idx),
                  pl.BlockSpec((1, 1, d), mod_idx),
                  pl.BlockSpec((d, IN_W), lambda i: (0, 0))],
        out_specs=(pl.BlockSpec((tm, A_IN), lambda i: (i % per_seq, i // per_seq)),)
        + tuple(pl.BlockSpec((tm, w), lambda i: (i, 0)) for w in widths),
        compiler_params=_cparams(("arbitrary",)),
        name="in_projection",
    )(x2, gain.reshape(1, d), scale1p.reshape(bm, 1, d), shift.reshape(bm, 1, d), w_in)
    return ua.reshape(t_len, b, A_IN), ub, uc, ud


def _head_sums(x, seg):
    hi = x.astype(BF16)
    lo = (x - hi.astype(F32)).astype(BF16)
    return (jnp.dot(hi, seg, preferred_element_type=F32) + jnp.dot(lo, seg, preferred_element_type=F32))


def _head_segments():
    idx = np.arange(GROUP_W) // HEAD_DIM
    return jnp.asarray(idx[:, None] == idx[None, :], BF16)


def _outproj_kernel(yf_ref, yb_ref, gb_ref, ob_ref, oc_ref, od_ref, gnw_ref, gnb_ref, seg_ref, w_ref, x_ref, g1_ref,
                    gn_ref, sc_ref, sh_ref, wr_ref, br_ref, xo_ref, h_ref, lg_ref):
    gw = GROUP_W
    o = yf_ref[...] + yb_ref[...]
    seg = seg_ref[...]
    mean = _head_sums(o, seg) * (1.0 / HEAD_DIM)
    cen = o - mean
    var = _head_sums(cen * cen, seg) * (1.0 / HEAD_DIM)
    oa = cen * lax.rsqrt(var + RWKV_GN_EPS) * gnw_ref[...] + gnb_ref[...]
    oa = (oa + gb_ref[:, gw:2 * gw]) * gb_ref[:, 0:gw]
    y = jnp.dot(oa.astype(BF16), w_ref[0:gw, :].astype(BF16), preferred_element_type=F32)
    for j, m_ref in enumerate((ob_ref, oc_ref, od_ref)):
        y = y + jnp.dot(m_ref[...].astype(BF16), w_ref[(j + 1) * gw:(j + 2) * gw, :].astype(BF16),
                        preferred_element_type=F32)
    xn = x_ref[...] + g1_ref[0] * y
    xo_ref[...] = xn
    ms = jnp.mean(xn * xn, axis=-1, keepdims=True)
    h = xn * lax.rsqrt(ms + NORM_EPS) * gn_ref[...]
    h = (h * sc_ref[0] + sh_ref[0]).astype(BF16)
    h_ref[...] = h
    lg_ref[...] = jnp.dot(h, wr_ref[...].astype(BF16), preferred_element_type=F32) + br_ref[...]


def _out_projection(yf, yb, gb, ob, oc, od, gn_w, gn_b, w_out, x2, gate1, gain, scale1p, shift, w_router, b_router,
                    t_len):
    n, d = x2.shape
    bm = gate1.shape[0]
    tm = ROW_TILE
    per_seq = t_len // tm
    gw = GROUP_W

    def mod_idx(i):
        return ((i // per_seq) if bm > 1 else 0, 0, 0)

    def tm_idx(i):
        return (i % per_seq, i // per_seq)

    def rows(w):
        return pl.BlockSpec((tm, w), lambda i: (i, 0))

    def full(shape):
        return pl.BlockSpec(shape, lambda i: (0,) * len(shape))

    mod_spec = pl.BlockSpec((1, 1, d), mod_idx)
    return pl.pallas_call(
        _outproj_kernel,
        out_shape=(jax.ShapeDtypeStruct((n, d), F32),
                   jax.ShapeDtypeStruct((n, d), BF16),
                   jax.ShapeDtypeStruct((n, ROUTER_W), F32)),
        grid=(n // tm,),
        in_specs=[pl.BlockSpec((tm, gw), tm_idx), pl.BlockSpec((tm, gw), tm_idx), pl.BlockSpec((tm, 2 * gw), tm_idx),
                  rows(gw), rows(gw), rows(gw), full((1, gw)), full((1, gw)), full((gw, gw)), full((d, d)), rows(d),
                  mod_spec, full((1, d)), mod_spec, mod_spec, full((d, ROUTER_W)), full((1, ROUTER_W))],
        out_specs=(rows(d), rows(d), rows(ROUTER_W)),
        compiler_params=_cparams(("arbitrary",)),
        name="out_projection",
    )(yf.reshape(t_len, -1), yb.reshape(t_len, -1), gb.reshape(t_len, -1), ob, oc, od,
      gn_w.reshape(1, gw), gn_b.reshape(1, gw), _head_segments(), w_out, x2, gate1.reshape(bm, 1, d),
      gain.reshape(1, d), scale1p.reshape(bm, 1, d), shift.reshape(bm, 1, d), w_router, b_router)


ACC_ROWS = HEAD_DIM + 16


def _flash_kernel(sc_ref, q_ref, *refs, heads, diff, chunk, has_ctx):
    if has_ctx:
        kc_ref, vtc_ref, k_ref, vt_ref, g_ref, o_ref, m_ref, acc_ref, sa_ref, sb_ref = refs
        n_ctx = kc_ref.shape[1] // chunk
    else:
        k_ref, vt_ref, g_ref, o_ref, m_ref, acc_ref, sa_ref, sb_ref = refs
        kc_ref = vtc_ref = None
        n_ctx = 0
    tq = q_ref.shape[1]
    kdim = k_ref.shape[2]
    n_lat = k_ref.shape[1] // chunk
    qt = q_ref[0].astype(F32).T.astype(BF16)

    def stationary(q_rows, k_row0):
        pieces = []
        if k_row0 > 0:
            pieces.append(jnp.zeros((k_row0, tq), BF16))
        pieces.append(q_rows)
        rest = kdim - k_row0 - q_rows.shape[0]
        if rest > 0:
            pieces.append(jnp.zeros((rest, tq), BF16))
        return jnp.concatenate(pieces, axis=0) if len(pieces) > 1 else pieces[0]

    streams = []
    for (maps, v_row0) in heads:
        for (qs, qn, ks) in maps:
            streams.append((stationary(qt[qs:qs + qn], ks), v_row0))
    m_ref[...] = jnp.full(m_ref.shape, MASK_VALUE, F32)
    acc_ref[...] = jnp.zeros(acc_ref.shape, F32)
    ones_rows = (lax.broadcasted_iota(jnp.int32, (ACC_ROWS - HEAD_DIM, chunk), 0) == 0).astype(BF16)

    def key_rows(src):
        kind, c = src
        if kind == "ctx":
            return kc_ref[0, c * chunk:(c + 1) * chunk, :]
        return k_ref[0, pl.ds(pl.multiple_of(c * chunk, chunk), chunk), :]

    def value_cols(src, v_row0):
        kind, c = src
        if kind == "ctx":
            return vtc_ref[0, v_row0:v_row0 + HEAD_DIM, c * chunk:(c + 1) * chunk]
        return vt_ref[0, pl.ds(v_row0, HEAD_DIM), pl.ds(pl.multiple_of(c * chunk, chunk), chunk)]

    def score(src, s_ref, j):
        s_ref[j] = jnp.dot(key_rows(src), streams[j][0], preferred_element_type=F32)

    def accumulate(src, s_ref, j):
        s = s_ref[j]
        m = m_ref[j]
        m_new = jnp.maximum(m, jnp.max(s, axis=0, keepdims=True))
        p = jnp.exp2(s - m_new).astype(BF16)
        alpha = jnp.exp2(m - m_new)
        m_ref[j] = m_new
        vc = jnp.concatenate([value_cols(src, streams[j][1]), ones_rows], axis=0)
        acc_ref[j] = alpha * acc_ref[j] + jnp.dot(vc, p, preferred_element_type=F32)

    ns = len(streams)

    def stage(nxt, next_ref, cur, cur_ref):
        for j in range(ns):
            if nxt is not None:
                score(nxt, next_ref, j)
            if cur is not None:
                accumulate(cur, cur_ref, j)

    bufs = (sa_ref, sb_ref)
    first = ("ctx", 0) if n_ctx else ("lat", 0)
    stage(first, bufs[0], None, None)
    for g in range(n_ctx):
        nxt = ("ctx", g + 1) if g + 1 < n_ctx else ("lat", 0)
        stage(nxt, bufs[(g + 1) % 2], ("ctx", g), bufs[g % 2])
    ra, rb = bufs[n_ctx % 2], bufs[(n_ctx + 1) % 2]

    def body(jj, carry):
        c = 2 * jj
        stage(("lat", c + 1), rb, ("lat", c), ra)
        stage(("lat", c + 2), ra, ("lat", c + 1), rb)
        return carry

    lax.fori_loop(0, (n_lat - 1) // 2, body, 0)
    if n_lat % 2 == 0:
        stage(("lat", n_lat - 1), rb, ("lat", n_lat - 2), ra)
        stage(None, None, ("lat", n_lat - 1), rb)
    else:
        stage(None, None, ("lat", n_lat - 1), ra)

    outs = []
    j = 0
    for (maps, v_row0) in heads:
        a1 = acc_ref[j]
        o = a1[:HEAD_DIM] / a1[HEAD_DIM:HEAD_DIM + 1]
        if diff:
            a2 = acc_ref[j + 1]
            o = o - (sc_ref[0] / a2[HEAD_DIM:HEAD_DIM + 1]) * a2[:HEAD_DIM]
            ms = jnp.mean(o * o, axis=0, keepdims=True)
            o = o * lax.rsqrt(ms + NORM_EPS) * g_ref[...] * sc_ref[1]
        j += len(maps)
        outs.append(o)
    o_ref[0] = jnp.concatenate(outs, axis=0).T


def _flash_attention(q, k, vt, heads, scalars=None, gain=None, k_ctx=None, vt_ctx=None):
    b, s, c = q.shape
    l, ck = k.shape[1:]
    cv = vt.shape[1]
    tq = min(Q_TILE, s)
    chunk = min(KV_CHUNK, l)
    assert l % chunk == 0 and s % tq == 0
    diff = scalars is not None
    has_ctx = k_ctx is not None
    ns = sum(len(maps) for maps, _ in heads)
    if not diff:
        scalars = jnp.zeros((2,), F32)
        gain = jnp.ones((HEAD_DIM,), F32)
    in_specs = [pl.BlockSpec((1, tq, c), lambda bi, i, sc: (bi, i, 0))]
    args = [q]
    if has_ctx:
        lc = k_ctx.shape[1]
        assert lc % chunk == 0
        in_specs += [pl.BlockSpec((1, lc, ck), lambda bi, i, sc: (bi, 0, 0)),
                     pl.BlockSpec((1, cv, lc), lambda bi, i, sc: (bi, 0, 0))]
        args += [k_ctx, vt_ctx]
    in_specs += [pl.BlockSpec((1, l, ck), lambda bi, i, sc: (bi, 0, 0)),
                 pl.BlockSpec((1, cv, l), lambda bi, i, sc: (bi, 0, 0)),
                 pl.BlockSpec((HEAD_DIM, 1), lambda bi, i, sc: (0, 0))]
    args += [k, vt, gain.reshape(HEAD_DIM, 1)]
    return pl.pallas_call(
        functools.partial(_flash_kernel, heads=heads, diff=diff, chunk=chunk, has_ctx=has_ctx),
        out_shape=jax.ShapeDtypeStruct((b, s, c), F32),
        grid_spec=pltpu.PrefetchScalarGridSpec(
            num_scalar_prefetch=1,
            grid=(b, s // tq),
            in_specs=in_specs,
            out_specs=pl.BlockSpec((1, tq, c), lambda bi, i, sc: (bi, i, 0)),
            scratch_shapes=[pltpu.VMEM((ns, 1, tq), F32), pltpu.VMEM((ns, ACC_ROWS, tq), F32),
                            pltpu.VMEM((ns, chunk, tq), F32), pltpu.VMEM((ns, chunk, tq), F32)]),
        compiler_params=_cparams(("arbitrary", "arbitrary")),
        name="flash_attention",
    )(scalars, *args)


DIFF_HEADS = tuple((((h * HEAD_DIM, HEAD_DIM // 2, h * HEAD_DIM),
                     (h * HEAD_DIM + HEAD_DIM // 2, HEAD_DIM // 2, h * HEAD_DIM + HEAD_DIM // 2)), h * HEAD_DIM)
                   for h in range(GROUP_HEADS))
DENSE_HEADS = tuple((((h * HEAD_DIM, HEAD_DIM, h * HEAD_DIM),), h * HEAD_DIM) for h in range(GROUP_HEADS))
GQA_HEADS = tuple((((h * HEAD_DIM, HEAD_DIM, (h // 2) * HEAD_DIM),), (h // 2) * HEAD_DIM) for h in range(GROUP_HEADS))


def _group_segments(width, group):
    idx = np.arange(width) // group
    return jnp.asarray(idx[:, None] == idx[None, :], BF16)


def _rope_tables(cs, dim, width):
    cos, sin = cs
    t = cos.shape[0]
    zero = jnp.zeros_like(sin)
    cos_l = jnp.stack([cos, cos], axis=2).reshape(t, dim)
    ta = jnp.stack([-sin, zero], axis=2).reshape(t, dim)
    tb = jnp.stack([zero, sin], axis=2).reshape(t, dim)
    rep = width // dim
    return tuple(jnp.tile(z, (1, rep)) for z in (cos_l, ta, tb))


def _aprep_kernel(ub_ref, uc_ref, ud_ref, cb_ref, ab_ref, bb_ref, cd_ref, ad_ref, bd_ref, gb_ref, gc_ref, gd_ref,
                  segb_ref, segd_ref, qb_ref, kb_ref, vbt_ref, qc_ref, kc_ref, vct_ref, qd_ref, kd_ref, vdt_ref):
    gw = GROUP_W
    kvw = GQA_KV_HEADS * HEAD_DIM

    def norm(x, seg, group, gain):
        ms = _head_sums(x * x, seg) * (1.0 / group)
        return x * lax.rsqrt(ms + NORM_EPS) * gain

    def norm_rope(x, seg, group, gain, cos, ta, tb, scale):
        x = norm(x, seg, group, gain)
        q4 = group // 4
        outs = []
        for c0 in range(0, x.shape[1], LANES):
            xc = x[:, c0:c0 + LANES]
            up = pltpu.roll(xc, LANES - q4, axis=1)
            dn = pltpu.roll(xc, q4, axis=1)
            outs.append(xc * cos[:, c0:c0 + LANES] + up * ta[:, c0:c0 + LANES] + dn * tb[:, c0:c0 + LANES])
        y = jnp.concatenate(outs, axis=1) if len(outs) > 1 else outs[0]
        if scale != 1.0:
            y = y * scale
        return y.astype(BF16)

    ub = ub_ref[...]
    segb = segb_ref[...]
    cb, ab, bb = cb_ref[...], ab_ref[...], bb_ref[...]
    qb_ref[...] = norm_rope(ub[:, 0:gw], segb, DIFF_DQK, gb_ref[0:1, :], cb, ab, bb, LOG2E / math.sqrt(DIFF_DQK))
    kb_ref[...] = norm_rope(ub[:, gw:2 * gw], segb, DIFF_DQK, gb_ref[1:2, :], cb, ab, bb, 1.0)
    vbt_ref[0] = ub[:, 2 * gw:3 * gw].T.astype(BF16)
    segd = segd_ref[...]
    uc = uc_ref[...]
    qc_ref[...] = (norm(uc[:, 0:gw], segd, HEAD_DIM, gc_ref[0:1, :]) * (LOG2E / math.sqrt(HEAD_DIM))).astype(BF16)
    kc_ref[...] = norm(uc[:, gw:2 * gw], segd, HEAD_DIM, gc_ref[1:2, :]).astype(BF16)
    vct_ref[0] = uc[:, 2 * gw:3 * gw].T.astype(BF16)
    ud = ud_ref[...]
    cd, ad, bd = cd_ref[...], ad_ref[...], bd_ref[...]
    qd_ref[...] = norm_rope(ud[:, 0:gw], segd, HEAD_DIM, gd_ref[0:1, :], cd, ad, bd, LOG2E / math.sqrt(HEAD_DIM))
    kd_ref[...] = norm_rope(ud[:, gw:gw + kvw], segd[0:kvw, 0:kvw], HEAD_DIM, gd_ref[1:2, 0:kvw],
                            cd[:, 0:kvw], ad[:, 0:kvw], bd[:, 0:kvw], 1.0)
    vdt_ref[0] = ud[:, gw + kvw:gw + 2 * kvw].T.astype(BF16)


def _attention_prepare(ub, uc, ud, rope, diff_norm, na_norm, gqa_norm, t_len):
    n = ub.shape[0]
    b = n // t_len
    tm = ROW_TILE
    per_seq = t_len // tm
    gw = GROUP_W
    kvw = GQA_KV_HEADS * HEAD_DIM
    tabs_b, tabs_d = rope
    gains_b = jnp.tile(diff_norm, (1, gw // DIFF_DQK))
    gains_c = jnp.tile(na_norm, (1, gw // HEAD_DIM))
    gains_d = jnp.tile(gqa_norm, (1, gw // HEAD_DIM))

    def rows(w):
        return pl.BlockSpec((tm, w), lambda i: (i, 0))

    def tab():
        return pl.BlockSpec((tm, gw), lambda i: (i % per_seq, 0))

    def full(shape):
        return pl.BlockSpec(shape, lambda i: (0,) * len(shape))

    def chan(w):
        return pl.BlockSpec((1, w, tm), lambda i: (i // per_seq, 0, i % per_seq))

    return pl.pallas_call(
        _aprep_kernel,
        out_shape=(jax.ShapeDtypeStruct((n, gw), BF16), jax.ShapeDtypeStruct((n, gw), BF16),
                   jax.ShapeDtypeStruct((b, gw, t_len), BF16),
                   jax.ShapeDtypeStruct((n, gw), BF16), jax.ShapeDtypeStruct((n, gw), BF16),
                   jax.ShapeDtypeStruct((b, gw, t_len), BF16),
                   jax.ShapeDtypeStruct((n, gw), BF16), jax.ShapeDtypeStruct((n, kvw), BF16),
                   jax.ShapeDtypeStruct((b, kvw, t_len), BF16)),
        grid=(n // tm,),
        in_specs=[rows(B_IN), rows(C_IN), rows(D_IN), tab(), tab(), tab(), tab(), tab(), tab(), full((2, gw)),
                  full((2, gw)), full((2, gw)), full((gw, gw)), full((gw, gw))],
        out_specs=(rows(gw), rows(gw), chan(gw), rows(gw), rows(gw), chan(gw), rows(gw), rows(kvw), chan(kvw)),
        compiler_params=_cparams(("arbitrary",)),
        name="attention_prepare",
    )(ub, uc, ud, *tabs_b, *tabs_d, gains_b, gains_c, gains_d, _group_segments(gw, DIFF_DQK),
      _group_segments(gw, HEAD_DIM))


NA_QROWS = 4
NA_SPAN = 12


def _na_kernel(q_ref, k_ref, vt_ref, kc_ref, vtc_ref, bias_ref, o_ref, *, grid_rows):
    i = pl.program_id(1)
    start = jnp.clip(i * NA_QROWS - NA_ROWS // 2, 0, grid_rows - NA_SPAN)
    off = pl.multiple_of(start * GRID_W, NA_QROWS * GRID_W)
    span = NA_SPAN * GRID_W
    tq = q_ref.shape[1]
    qt = q_ref[0].astype(F32).T.astype(BF16)
    kw = k_ref[0, pl.ds(off, span), :]
    kc = kc_ref[0]
    ones_w = (lax.broadcasted_iota(jnp.int32, (ACC_ROWS - HEAD_DIM, span), 0) == 0).astype(BF16)
    ones_c = (lax.broadcasted_iota(jnp.int32, (ACC_ROWS - HEAD_DIM, kc.shape[0]), 0) == 0).astype(BF16)
    outs = []
    for h in range(GROUP_HEADS):
        r0 = h * HEAD_DIM
        pieces = []
        if r0 > 0:
            pieces.append(jnp.zeros((r0, tq), BF16))
        pieces.append(qt[r0:r0 + HEAD_DIM])
        if r0 + HEAD_DIM < GROUP_W:
            pieces.append(jnp.zeros((GROUP_W - r0 - HEAD_DIM, tq), BF16))
        w = jnp.concatenate(pieces, axis=0)
        s_w = jnp.dot(kw, w, preferred_element_type=F32) + bias_ref[0, h]
        s_c = jnp.dot(kc, w, preferred_element_type=F32)
        m = jnp.maximum(jnp.max(s_w, axis=0, keepdims=True), jnp.max(s_c, axis=0, keepdims=True))
        p_w = jnp.exp2(s_w - m).astype(BF16)
        p_c = jnp.exp2(s_c - m).astype(BF16)
        vw = jnp.concatenate([vt_ref[0, pl.ds(r0, HEAD_DIM), pl.ds(off, span)], ones_w], axis=0)
        vc = jnp.concatenate([vtc_ref[0, r0:r0 + HEAD_DIM, :], ones_c], axis=0)
        acc = jnp.dot(vw, p_w, preferred_element_type=F32) + jnp.dot(vc, p_c, preferred_element_type=F32)
        outs.append(acc[:HEAD_DIM] / acc[HEAD_DIM:HEAD_DIM + 1])
    o_ref[0] = jnp.concatenate(outs, axis=0).T


def _na_bias_tables(rel_bias, grid_rows):
    half_r, half_c = NA_ROWS // 2, NA_COLS // 2
    kcol = np.arange(GRID_W)[:, None]
    qcol = np.arange(GRID_W)[None, :]
    cs = np.clip(qcol - half_c, 0, GRID_W - NA_COLS)
    valid_c = (kcol >= cs) & (kcol < cs + NA_COLS)
    cidx = np.clip(kcol - qcol + NA_COLS - 1, 0, 2 * NA_COLS - 2)
    col_sel = jnp.asarray(cidx[:, :, None] == np.arange(2 * NA_COLS - 1), F32)
    kr_l = np.arange(NA_SPAN)[:, None]
    qr_l = np.arange(NA_QROWS)[None, :]
    tabs = []
    for r0 in (0, NA_QROWS * (grid_rows // NA_QROWS // 2), grid_rows - NA_QROWS):
        start = int(np.clip(r0 - half_r, 0, grid_rows - NA_SPAN))
        kr = start + kr_l
        qr = r0 + qr_l
        rs = np.clip(qr - half_r, 0, grid_rows - NA_ROWS)
        valid_r = (kr >= rs) & (kr < rs + NA_ROWS)
        ridx = np.clip(kr - qr + NA_ROWS - 1, 0, 2 * NA_ROWS - 2)
        row_sel = jnp.asarray(ridx[:, :, None] == np.arange(2 * NA_ROWS - 1), F32)
        t = jnp.einsum('kqr,hrc,xyc->hkxqy', row_sel, rel_bias.astype(F32), col_sel,
                       precision=lax.Precision.HIGHEST) * LOG2E
        valid = valid_r[:, None, :, None] & valid_c[None, :, None, :]
        t = jnp.where(valid[None], t, MASK_VALUE)
        tabs.append(t.reshape(rel_bias.shape[0], NA_SPAN * GRID_W, NA_QROWS * GRID_W))
    return jnp.stack(tabs, axis=0)


def _neighborhood_attention(q, k, vt, k_ctx, vt_ctx, bias_tabs):
    b, s, c = q.shape
    lc = k_ctx.shape[1]
    grid_rows = s // GRID_W
    assert grid_rows >= NA_SPAN and grid_rows % NA_QROWS == 0
    nblk = grid_rows // NA_QROWS
    tq = NA_QROWS * GRID_W

    def bias_idx(bi, i):
        return (jnp.where(i == 0, 0, jnp.where(i == nblk - 1, 2, 1)), 0, 0, 0)

    return pl.pallas_call(
        functools.partial(_na_kernel, grid_rows=grid_rows),
        out_shape=jax.ShapeDtypeStruct((b, s, c), F32),
        grid=(b, nblk),
        in_specs=[pl.BlockSpec((1, tq, c), lambda bi, i: (bi, i, 0)),
                  pl.BlockSpec((1, s, c), lambda bi, i: (bi, 0, 0)),
                  pl.BlockSpec((1, c, s), lambda bi, i: (bi, 0, 0)),
                  pl.BlockSpec((1, lc, c), lambda bi, i: (bi, 0, 0)),
                  pl.BlockSpec((1, c, lc), lambda bi, i: (bi, 0, 0)),
                  pl.BlockSpec((1, GROUP_HEADS, NA_SPAN * GRID_W, tq), bias_idx)],
        out_specs=pl.BlockSpec((1, tq, c), lambda bi, i: (bi, i, 0)),
        compiler_params=_cparams(("arbitrary", "arbitrary")),
        name="neighborhood_attention",
    )(q, k, vt, k_ctx, vt_ctx, bias_tabs)


SCAN_BATCH = SUBLANES
N_KQ = 5
FWD_W = (N_KQ + 1) * GROUP_W
BWD_W = (N_KQ - 1) * GROUP_W
V_SPLIT = 4
V_TILES = HEAD_DIM // (V_SPLIT * SUBLANES)
KT_BUFFERS = 4


def _scan_kernel(zf_ref, zb_ref, zrv_ref, s0_ref, yf_ref, yb_ref, sfin_ref, st_ref, sa_ref, *kt_refs, steps):
    i = pl.program_id(1)

    @pl.when(i == 0)
    def _():
        st_ref[...] = s0_ref[0]

    lane = lax.broadcasted_iota(jnp.int32, (SUBLANES, LANES), 1)
    lane_vq = lane // (LANES // V_SPLIT)
    chan_vq = (lane // (V_TILES * SUBLANES)) % V_SPLIT

    def operand_tile(q, s):
        sb = steps - 1 - s
        pieces = []
        for hp in range(2):
            pieces.append(zf_ref[s, :, pl.ds(q * GROUP_W + hp * LANES, LANES)])
        for hp in range(2):
            if q < N_KQ - 1:
                pieces.append(zb_ref[sb, :, pl.ds(q * GROUP_W + hp * LANES, LANES)])
            else:
                pieces.append(zrv_ref[sb, :, pl.ds((q - (N_KQ - 1)) * GROUP_W + hp * LANES, LANES)])
        tile = jnp.concatenate(pieces * V_SPLIT, axis=0)
        return tile.T

    def prepare(kt_ref, s):
        for q in range(N_KQ + 1):
            kt_ref[q] = operand_tile(q, s)

    nacc = 4

    def tree(parts):
        return (parts[0] + parts[1]) + (parts[2] + parts[3])

    prepare(kt_refs[0], 0)
    prepare(kt_refs[1], 1)
    for hpar in range(2):
        for vb in range(V_TILES):
            parts = [None] * nacc
            for k in range(HEAD_DIM):
                term = st_ref[hpar, k, vb] * kt_refs[0][0, pl.ds(hpar * HEAD_DIM + k, 1), :]
                parts[k % nacc] = term if parts[k % nacc] is None else parts[k % nacc] + term
            sa_ref[hpar, vb] = -tree(parts)

    def step(s, kt_ref, nxt_ref, far_ref):
        prepare(far_ref, jnp.minimum(s + 2, steps - 1))
        vt = kt_ref[N_KQ]
        ys = []
        for hpar in range(2):
            base = hpar * HEAD_DIM
            vops = []
            for vb in range(V_TILES):
                acc = None
                for vq in range(V_SPLIT):
                    r0 = base + vq * V_TILES * SUBLANES + vb * SUBLANES
                    piece = vt[r0:r0 + SUBLANES, :]
                    acc = piece if acc is None else jnp.where(lane_vq == vq, piece, acc)
                vops.append(acc)
            sas = [sa_ref[hpar, vb] for vb in range(V_TILES)]
            yp = [[None] * nacc for _ in range(V_TILES)]
            sp = [[None] * nacc for _ in range(V_TILES)]
            for k in range(HEAD_DIM):
                row = pl.ds(base + k, 1)
                w = kt_ref[1, row, :]
                ka = kt_ref[2, row, :]
                kd = kt_ref[3, row, :]
                rr = kt_ref[4, row, :]
                kn = nxt_ref[0, row, :]
                j = k % nacc
                for vb in range(V_TILES):
                    sn = st_ref[hpar, k, vb] * w + sas[vb] * ka + vops[vb] * kd
                    st_ref[hpar, k, vb] = sn
                    ty = sn * rr
                    yp[vb][j] = ty if yp[vb][j] is None else yp[vb][j] + ty
                    tn = sn * kn
                    sp[vb][j] = tn if sp[vb][j] is None else sp[vb][j] + tn
            for vb in range(V_TILES):
                sa_ref[hpar, vb] = -tree(sp[vb])
            ys.append([tree(yp[vb]) for vb in range(V_TILES)])
        ytile = jnp.concatenate([ys[hpar][vb] for hpar in range(2) for _ in range(V_SPLIT) for vb in range(V_TILES)],
                                axis=0)
        yt = ytile.T
        sb = steps - 1 - s
        for d in range(2):
            for hp in range(2):
                acc = None
                for vq in range(V_SPLIT):
                    r0 = vq * (LANES // V_SPLIT) + d * 2 * SUBLANES + hp * SUBLANES
                    piece = yt[r0:r0 + SUBLANES, :]
                    acc = piece if acc is None else jnp.where(chan_vq == vq, piece, acc)
                if d == 0:
                    yf_ref[s, :, pl.ds(hp * LANES, LANES)] = acc
                else:
                    yb_ref[sb, :, pl.ds(hp * LANES, LANES)] = acc

    nbuf = len(kt_refs)

    def body(j, carry):
        for u in range(nbuf):
            step(nbuf * j + u, kt_refs[u], kt_refs[(u + 1) % nbuf], kt_refs[(u + 2) % nbuf])
        return carry

    lax.fori_loop(0, steps // nbuf, body, 0)

    @pl.when(i == pl.num_programs(1) - 1)
    def _():
        sfin_ref[0] = st_ref[...]


def _rwkv_scan(zf, zb, s0):
    t_len, b, _ = zf.shape
    groups = b // SCAN_BATCH
    ts = SCAN_STEPS
    nblk = t_len // ts
    st_shape = (2, HEAD_DIM, V_TILES, SUBLANES, LANES)
    return pl.pallas_call(
        functools.partial(_scan_kernel, steps=ts),
        out_shape=(jax.ShapeDtypeStruct((t_len, b, GROUP_W), F32),
                   jax.ShapeDtypeStruct((t_len, b, GROUP_W), F32),
                   jax.ShapeDtypeStruct((groups,) + st_shape, F32)),
        grid=(groups, nblk),
        in_specs=[pl.BlockSpec((ts, SCAN_BATCH, FWD_W), lambda g, i: (i, g, 0)),
                  pl.BlockSpec((ts, SCAN_BATCH, BWD_W), lambda g, i: (nblk - 1 - i, g, 0)),
                  pl.BlockSpec((ts, SCAN_BATCH, 2 * GROUP_W), lambda g, i: (nblk - 1 - i, g, (N_KQ - 1) // 2)),
                  pl.BlockSpec((1,) + st_shape, lambda g, i: (g, 0, 0, 0, 0, 0))],
        out_specs=(pl.BlockSpec((ts, SCAN_BATCH, GROUP_W), lambda g, i: (i, g, 0)),
                   pl.BlockSpec((ts, SCAN_BATCH, GROUP_W), lambda g, i: (nblk - 1 - i, g, 0)),
                   pl.BlockSpec((1,) + st_shape, lambda g, i: (g, 0, 0, 0, 0, 0))),
        scratch_shapes=[pltpu.VMEM(st_shape, F32), pltpu.VMEM((2, V_TILES, SUBLANES, LANES), F32)]
        + [pltpu.VMEM((N_KQ + 1, LANES, LANES), F32)] * KT_BUFFERS,
        compiler_params=_cparams(("arbitrary", "arbitrary")),
        name="rwkv_scan",
    )(zf, zb, zf, s0)


PREP_STEPS = 32
DECAY_RATE = math.exp(-0.5)


def _prep_kernel(u_ref, up_ref, un_ref, mu_ref, w0_ref, wup_ref, a0_ref, aup_ref, gup_ref, kk_ref, ka_ref, rk_ref,
                 seg_ref, zf_ref, zb_ref, gb_ref):
    i = pl.program_id(1)
    last = pl.num_programs(1) - 1
    tb = u_ref.shape[0]
    u = u_ref[...]
    before = jnp.where(i > 0, up_ref[...], 0.0)
    after = jnp.where(i < last, un_ref[...], 0.0)
    prev = jnp.concatenate([before, u[:-1]], axis=0)
    nxt = jnp.concatenate([u[1:], after], axis=0)
    mu = mu_ref[...]
    x = (u + mu[0:1] * (prev - u) + mu[1:2] * (nxt - u)).reshape(tb * SCAN_BATCH, A_IN)
    gw = GROUP_W
    r = x[:, 0:gw]
    k = x[:, gw:2 * gw]
    v = x[:, 2 * gw:3 * gw]
    wd = x[:, 3 * gw:3 * gw + 2 * LORA_W]
    ad = x[:, 3 * gw + 2 * LORA_W:3 * gw + 2 * LORA_W + 2 * LORA_A]
    gd = x[:, 3 * gw + 2 * LORA_W + 2 * LORA_A:]
    seg = seg_ref[...]
    w_raw = w0_ref[...] + jnp.dot(jnp.tanh(wd).astype(BF16), wup_ref[...].astype(BF16), preferred_element_type=F32)
    decay = jnp.exp(-DECAY_RATE * jax.nn.sigmoid(w_raw))
    a = jax.nn.sigmoid(a0_ref[...] + jnp.dot(ad.astype(BF16), aup_ref[...].astype(BF16),
                                             preferred_element_type=F32))
    g = jnp.dot(jax.nn.sigmoid(gd).astype(BF16), gup_ref[...].astype(BF16), preferred_element_type=F32)
    bonus = jnp.zeros_like(r)
    for d, z_ref in enumerate((zf_ref, zb_ref)):
        sl = slice(d * gw, (d + 1) * gw)
        kk = k * kk_ref[:, sl]
        a_d = a[:, sl]
        kd = k * (1.0 + (a_d - 1.0) * ka_ref[:, sl])
        kk = kk * lax.rsqrt(_head_sums(kk * kk, seg) + 1e-12)
        bonus = bonus + _head_sums(r * kd * rk_ref[:, sl], seg) * v
        for q, val in enumerate((kk, decay[:, sl], kk * a_d, kd)):
            z_ref[:, :, q * gw:(q + 1) * gw] = val.reshape(tb, SCAN_BATCH, gw)
    zf_ref[:, :, 4 * gw:5 * gw] = r.reshape(tb, SCAN_BATCH, gw)
    zf_ref[:, :, 5 * gw:6 * gw] = v.reshape(tb, SCAN_BATCH, gw)
    gb_ref[:, :, 0:gw] = g.reshape(tb, SCAN_BATCH, gw)
    gb_ref[:, :, gw:2 * gw] = bonus.reshape(tb, SCAN_BATCH, gw)


def _block_diag(w):
    z = jnp.zeros_like(w[0])
    return jnp.concatenate([jnp.concatenate([w[0], z], axis=1), jnp.concatenate([z, w[1]], axis=1)], axis=0)


def _rwkv_prepare(ua_tm, mu, w0, w_up, a0, a_up, g_up, k_k, k_a, r_k):
    t, b, _ = ua_tm.shape
    tb = min(PREP_STEPS, t)
    nblk = t // tb
    gw2 = 2 * GROUP_W

    def full(shape):
        return pl.BlockSpec(shape, lambda g, i: (0,) * len(shape))

    return pl.pallas_call(
        _prep_kernel,
        out_shape=(jax.ShapeDtypeStruct((t, b, FWD_W), F32), jax.ShapeDtypeStruct((t, b, BWD_W), F32),
                   jax.ShapeDtypeStruct((t, b, gw2), F32)),
        grid=(b // SCAN_BATCH, nblk),
        in_specs=[pl.BlockSpec((tb, SCAN_BATCH, A_IN), lambda g, i: (i, g, 0)),
                  pl.BlockSpec((1, SCAN_BATCH, A_IN), lambda g, i: (jnp.maximum(i * tb - 1, 0), g, 0)),
                  pl.BlockSpec((1, SCAN_BATCH, A_IN), lambda g, i: (jnp.minimum((i + 1) * tb, t - 1), g, 0)),
                  full((2, A_IN)), full((1, gw2)), full((2 * LORA_W, gw2)), full((1, gw2)), full((2 * LORA_A, gw2)),
                  full((LORA_G, GROUP_W)), full((1, gw2)), full((1, gw2)), full((1, gw2)),
                  full((GROUP_W, GROUP_W))],
        out_specs=(pl.BlockSpec((tb, SCAN_BATCH, FWD_W), lambda g, i: (i, g, 0)),
                   pl.BlockSpec((tb, SCAN_BATCH, BWD_W), lambda g, i: (i, g, 0)),
                   pl.BlockSpec((tb, SCAN_BATCH, gw2), lambda g, i: (i, g, 0))),
        compiler_params=_cparams(("arbitrary", "arbitrary")),
        name="rwkv_prepare",
    )(ua_tm, ua_tm, ua_tm, mu, w0.reshape(1, gw2), _block_diag(w_up), a0.reshape(1, gw2), _block_diag(a_up), g_up,
      k_k.reshape(1, gw2), k_a.reshape(1, gw2), r_k.reshape(1, gw2), _head_segments())


def _rwkv_time_mix(ua_tm, s0, mu, w0, w_up, a0, a_up, g_up, k_k, k_a, r_k):
    t, b, _ = ua_tm.shape
    h, n = GROUP_HEADS, HEAD_DIM
    bp = -(-b // SCAN_BATCH) * SCAN_BATCH
    groups = bp // SCAN_BATCH
    if bp != b:
        ua_tm = jnp.pad(ua_tm, ((0, 0), (0, bp - b), (0, 0)))
    zf, zb, gb = _rwkv_prepare(ua_tm, mu, w0, w_up, a0, a_up, g_up, k_k, k_a, r_k)
    if s0 is None:
        s0l = jnp.zeros((groups, 2, n, V_TILES, SUBLANES, LANES), F32)
    else:
        s0 = jnp.pad(s0.astype(F32), ((0, bp - b),) + ((0, 0),) * 4)
        s0l = s0.reshape(groups, SCAN_BATCH, 2, 2, 2, V_SPLIT, V_TILES, SUBLANES, n)
        s0l = jnp.transpose(s0l, (0, 4, 8, 6, 7, 5, 2, 3, 1)).reshape(groups, 2, n, V_TILES, SUBLANES, LANES)
    yf, yb, s_fin = _rwkv_scan(zf, zb, s0l)
    s_fin = s_fin.reshape(groups, 2, n, V_TILES, SUBLANES, V_SPLIT, 2, 2, SCAN_BATCH)
    s_fin = jnp.transpose(s_fin, (0, 8, 6, 7, 1, 5, 3, 4, 2)).reshape(bp, 2, h, n, n)[:b]
    return yf, yb, gb, s_fin


def _expert_kernel(be_ref, nu_ref, x_ref, w1_ref, w3_ref, w2_ref, o_ref, w1s, w3s, w2s):
    i = pl.program_id(0)
    e = be_ref[i]
    prev = be_ref[jnp.maximum(i - 1, 0)]

    @pl.when((i == 0) | (e != prev))
    def _():
        w1s[...] = w1_ref[0].astype(BF16)
        w3s[...] = w3_ref[0].astype(BF16)
        w2s[...] = w2_ref[0].astype(BF16)

    @pl.when(i < nu_ref[0])
    def _():
        x = x_ref[...]
        a = jnp.dot(x, w1s[...], preferred_element_type=F32)
        g = jnp.dot(x, w3s[...], preferred_element_type=F32)
        hmid = (a * jax.nn.sigmoid(a)) * g
        o_ref[...] = jnp.dot(hmid.astype(BF16), w2s[...], preferred_element_type=F32).astype(o_ref.dtype)

    @pl.when(i >= nu_ref[0])
    def _():
        o_ref[...] = jnp.zeros_like(o_ref)


def _expert_mlp(xb, block_e, n_used, w1, w3, w2):
    cap, d = xb.shape
    bm = MOE_TILE
    de = w1.shape[-1]
    return pl.pallas_call(
        _expert_kernel,
        out_shape=jax.ShapeDtypeStruct((cap, d), BF16),
        grid_spec=pltpu.PrefetchScalarGridSpec(
            num_scalar_prefetch=2,
            grid=(cap // bm,),
            in_specs=[pl.BlockSpec((bm, d), lambda i, be, nu: (i, 0)),
                      pl.BlockSpec((1, d, de), lambda i, be, nu: (be[i], 0, 0)),
                      pl.BlockSpec((1, d, de), lambda i, be, nu: (be[i], 0, 0)),
                      pl.BlockSpec((1, de, d), lambda i, be, nu: (be[i], 0, 0))],
            out_specs=pl.BlockSpec((bm, d), lambda i, be, nu: (i, 0)),
            scratch_shapes=[pltpu.VMEM((d, de), BF16), pltpu.VMEM((d, de), BF16), pltpu.VMEM((de, d), BF16)]),
        compiler_params=_cparams(("arbitrary",)),
        name="expert_mlp",
    )(block_e, n_used, xb, w1, w3, w2)


def _hier_moe(h_bf, logits, w1, w3, w2):
    n, d = h_bf.shape
    bm = MOE_TILE
    g_logits = logits[:, :N_EXPERT_GROUPS]
    g_idx = jnp.argmax(g_logits, axis=-1)
    g_top = jnp.max(jax.nn.softmax(g_logits, axis=-1), axis=-1)
    e_logits = logits[:, N_EXPERT_GROUPS:N_EXPERT_GROUPS + N_EXPERTS].reshape(n, N_EXPERT_GROUPS, EXPERTS_PER_GROUP)
    e_logits = jnp.take_along_axis(e_logits, g_idx[:, None, None], axis=1)[:, 0]
    top_v, top_i = lax.top_k(e_logits, TOP_K)
    gate = jax.nn.softmax(top_v, axis=-1) * g_top[:, None]
    eid = (g_idx[:, None] * EXPERTS_PER_GROUP + top_i).reshape(-1).astype(jnp.int32)
    tok = jnp.repeat(jnp.arange(n, dtype=jnp.int32), TOP_K)
    n_assign = n * TOP_K
    onehot = (eid[:, None] == jnp.arange(N_EXPERTS, dtype=jnp.int32)[None, :]).astype(jnp.int32)
    csum = jnp.cumsum(onehot, axis=0)
    counts = csum[-1]
    rank = jnp.take_along_axis(csum, eid[:, None], axis=1)[:, 0] - 1
    padded = (counts + bm - 1) // bm * bm
    pad_end = jnp.cumsum(padded)
    pad_start = pad_end - padded
    dest = pad_start[eid] + rank
    n_blocks = -(-(n_assign + N_EXPERTS * (bm - 1)) // bm)
    cap = n_blocks * bm
    buf_tok = jnp.zeros((cap,), jnp.int32).at[dest].set(tok)
    block_start = jnp.arange(n_blocks, dtype=jnp.int32) * bm
    block_e = jnp.minimum(jnp.sum((pad_end[None, :] <= block_start[:, None]).astype(jnp.int32), axis=1),
                          N_EXPERTS - 1).astype(jnp.int32)
    n_used = (pad_end[-1:] // bm).astype(jnp.int32)
    xb = h_bf[buf_tok]
    yb = _expert_mlp(xb, block_e, n_used, w1, w3, w2)
    d2 = dest.reshape(n, TOP_K)
    return yb[d2[:, 0]].astype(F32) * gate[:, 0:1] + yb[d2[:, 1]].astype(F32) * gate[:, 1:2]


def _rms(x, g):
    return x * lax.rsqrt(jnp.mean(x * x, axis=-1, keepdims=True) + NORM_EPS) * g


def _axial_rope(t_len, dim):
    q4 = dim // 4
    inv = ROPE_THETA ** (-jnp.arange(q4, dtype=F32) / q4)
    t = jnp.arange(t_len)
    row = (t // GRID_W).astype(F32)
    col = (t % GRID_W).astype(F32)
    ang = jnp.stack([row[:, None] * inv, col[:, None] * inv], axis=1)
    return jnp.cos(ang), jnp.sin(ang)


def _token_mixers(ua_tm, ub, uc, ud, rope, ctx, l, W):
    b, t, _ = ub.shape
    nh, n = GROUP_HEADS, HEAD_DIM
    latent = ctx is not None

    yf, yb, gb, s_fin = _rwkv_time_mix(ua_tm, ctx[0] if latent else None, W['rw_shift'][l], W['rw_w0'][l],
                                       W['rw_w_up'][l], W['rw_a0'][l], W['rw_a_up'][l], W['rw_g_up'][l],
                                       W['rw_k_k'][l], W['rw_k_a'][l], W['rw_r_k'][l])

    kvw = GQA_KV_HEADS * n
    if latent:
        qb, kb, vbt, qc, kc, vct, qd, kd, vdt = _attention_prepare(
            ub.reshape(b * t, -1), uc.reshape(b * t, -1), ud.reshape(b * t, -1), rope,
            W['diff_qk_norm'][l], W['na_qk_norm'][l], W['gqa_qk_norm'][l], t)

    lam_init = 0.8 - 0.6 * math.exp(-0.3 * l)
    lv = W['diff_lambda'][l].astype(F32)
    lam = jnp.exp(jnp.sum(lv[0] * lv[1])) - jnp.exp(jnp.sum(lv[2] * lv[3])) + lam_init
    scalars = jnp.stack([lam, jnp.asarray(1.0 - lam_init, F32)]).astype(F32)
    if latent:
        k_ctx = jnp.transpose(ctx[1], (0, 3, 1, 2, 4)).reshape(b, -1, GROUP_W).astype(BF16)
        vt_ctx = jnp.transpose(ctx[2], (0, 1, 3, 2)).reshape(b, GROUP_W, -1).astype(BF16)
        out_b = _flash_attention(qb.reshape(b, t, GROUP_W), kb.reshape(b, t, GROUP_W), vbt, DIFF_HEADS, scalars,
                                 W['diff_subln'][l], k_ctx=k_ctx, vt_ctx=vt_ctx)
    else:
        qb, kb, vb = jnp.split(ub, 3, axis=-1)
        qb = _rms(qb.reshape(b, t, nh, 2, DIFF_DQK), W['diff_qk_norm'][l, 0])
        kb = _rms(kb.reshape(b, t, nh, 2, DIFF_DQK), W['diff_qk_norm'][l, 1])
        out_b = _flash_attention((qb * (LOG2E / math.sqrt(DIFF_DQK))).reshape(b, t, GROUP_W).astype(BF16),
                                 kb.reshape(b, t, GROUP_W).astype(BF16), jnp.swapaxes(vb, 1, 2).astype(BF16),
                                 DIFF_HEADS, scalars, W['diff_subln'][l])

    if latent:
        k_ctx = jnp.transpose(ctx[3], (0, 2, 1, 3)).reshape(b, -1, GROUP_W).astype(BF16)
        vt_ctx = jnp.transpose(ctx[4], (0, 1, 3, 2)).reshape(b, GROUP_W, -1).astype(BF16)
        out_c = _neighborhood_attention(qc.reshape(b, t, GROUP_W), kc.reshape(b, t, GROUP_W), vct, k_ctx, vt_ctx,
                                        _na_bias_tables(W['na_rel_bias'][l], t // GRID_W))
    else:
        qc, kc, vc = jnp.split(uc, 3, axis=-1)
        qc = _rms(qc.reshape(b, t, nh, n), W['na_qk_norm'][l, 0])
        kc = _rms(kc.reshape(b, t, nh, n), W['na_qk_norm'][l, 1])
        out_c = _flash_attention((qc * (LOG2E / math.sqrt(n))).reshape(b, t, GROUP_W).astype(BF16),
                                 kc.reshape(b, t, GROUP_W).astype(BF16), jnp.swapaxes(vc, 1, 2).astype(BF16),
                                 DENSE_HEADS)

    if latent:
        k_ctx = jnp.transpose(ctx[5], (0, 2, 1, 3)).reshape(b, -1, kvw).astype(BF16)
        vt_ctx = jnp.transpose(ctx[6], (0, 1, 3, 2)).reshape(b, kvw, -1).astype(BF16)
        out_d = _flash_attention(qd.reshape(b, t, GROUP_W), kd.reshape(b, t, kvw), vdt, GQA_HEADS,
                                 k_ctx=k_ctx, vt_ctx=vt_ctx)
    else:
        qd, kd, vd = jnp.split(ud, [GROUP_W, GROUP_W + kvw], axis=-1)
        qd = _rms(qd.reshape(b, t, nh, n), W['gqa_qk_norm'][l, 0])
        kd = _rms(kd.reshape(b, t, GQA_KV_HEADS, n), W['gqa_qk_norm'][l, 1])
        out_d = _flash_attention((qd * (LOG2E / math.sqrt(n))).reshape(b, t, GROUP_W).astype(BF16),
                                 kd.reshape(b, t, kvw).astype(BF16), jnp.swapaxes(vd, 1, 2).astype(BF16), GQA_HEADS)

    mix = (yf, yb, gb) + tuple(z.reshape(b * t, GROUP_W) for z in (out_b, out_c, out_d))
    if latent:
        return mix, None
    new_ctx = (s_fin, jnp.transpose(kb, (0, 2, 3, 1, 4)), vb.reshape(b, t, nh, n).transpose(0, 2, 1, 3),
               kc.transpose(0, 2, 1, 3), vc.reshape(b, t, nh, n).transpose(0, 2, 1, 3),
               kd.transpose(0, 2, 1, 3), vd.reshape(b, t, GQA_KV_HEADS, n).transpose(0, 2, 1, 3))
    return mix, new_ctx


def _mixing_sublayer(x, cond, rope, ctx, l, W):
    b, t, d = x.shape
    n = b * t
    mod = _modulation(cond, W['w_mod'][l], W['b_mod'][l])
    sh1, sc1, g1, sh2, sc2, g2 = jnp.split(mod, 6, axis=-1)
    x2 = x.reshape(n, d)
    ua, ub, uc, ud = _in_projection(x2, W['norm_mix'][l], 1.0 + sc1, sh1, W['w_in'][l], t)
    mix, new_ctx = _token_mixers(ua, ub.reshape(b, t, -1), uc.reshape(b, t, -1), ud.reshape(b, t, -1), rope, ctx, l, W)
    w_router = jnp.concatenate([W['moe_w_group'][l], W['moe_w_router'][l]], axis=1)
    w_router = jnp.pad(w_router, ((0, 0), (0, ROUTER_W - w_router.shape[1])))
    b_router = jnp.concatenate([W['moe_b_group'][l], W['moe_b_router'][l]])
    b_router = jnp.pad(b_router, (0, ROUTER_W - b_router.shape[0])).reshape(1, ROUTER_W)
    x2, h2, logits = _out_projection(*mix, W['rw_gn_w'][l], W['rw_gn_b'][l], W['w_out'][l], x2, g1,
                                     W['norm_ffn'][l], 1.0 + sc2, sh2, w_router, b_router, t)
    g2r = jnp.broadcast_to(g2[:, None, :], (g2.shape[0], n // g2.shape[0], d)).reshape(n, d)
    return x2, h2, logits, g2r, new_ctx


def kernel(x_prompt, x_sample, c, state_rwkv, cache_diff_k, cache_diff_v, cache_na_k, cache_na_v,
           cache_gqa_k, cache_gqa_v, c_ctx, norm_mix, norm_ffn, w_mod, b_mod, w_in, w_out,
           rw_shift, rw_w0, rw_w_up, rw_a0, rw_a_up, rw_g_up, rw_k_k, rw_k_a, rw_r_k, rw_gn_w, rw_gn_b,
           diff_qk_norm, diff_lambda, diff_subln, na_qk_norm, na_rel_bias, gqa_qk_norm,
           moe_w_group, moe_b_group, moe_w_router, moe_b_router, moe_w1, moe_w3, moe_w2):
    W = {
        'norm_mix': norm_mix, 'norm_ffn': norm_ffn, 'w_mod': w_mod, 'b_mod': b_mod,
        'w_in': w_in, 'w_out': w_out, 'rw_shift': rw_shift, 'rw_w0': rw_w0, 'rw_w_up': rw_w_up,
        'rw_a0': rw_a0, 'rw_a_up': rw_a_up, 'rw_g_up': rw_g_up, 'rw_k_k': rw_k_k, 'rw_k_a': rw_k_a,
        'rw_r_k': rw_r_k, 'rw_gn_w': rw_gn_w, 'rw_gn_b': rw_gn_b, 'diff_qk_norm': diff_qk_norm,
        'diff_lambda': diff_lambda, 'diff_subln': diff_subln, 'na_qk_norm': na_qk_norm,
        'na_rel_bias': na_rel_bias, 'gqa_qk_norm': gqa_qk_norm, 'moe_w_group': moe_w_group,
        'moe_b_group': moe_b_group, 'moe_w_router': moe_w_router, 'moe_b_router': moe_b_router,
        'moe_w1': moe_w1, 'moe_w3': moe_w3, 'moe_w2': moe_w2,
    }
    depth = w_in.shape[0]

    def run_pass(x, cond, rope, caches):
        new_ctx = []
        for l in range(depth):
            ctx = None if caches is None else tuple(z[:, l] for z in caches)
            r2, h2, logits, g2r, nc = _mixing_sublayer(x, cond, rope, ctx, l, W)
            y = _hier_moe(h2, logits, W['moe_w1'][l], W['moe_w3'][l], W['moe_w2'][l])
            x = (r2 + g2r * y).reshape(x.shape)
            new_ctx.append(nc)
        return x, new_ctx

    xp, ctx_layers = run_pass(x_prompt, c_ctx[None, :], None, None)
    new_caches = tuple(jnp.stack([z[i] for z in ctx_layers], axis=1) for i in range(7))

    t_lat = x_sample.shape[1]
    rope = (_rope_tables(_axial_rope(t_lat, DIFF_DQK), DIFF_DQK, GROUP_W),
            _rope_tables(_axial_rope(t_lat, HEAD_DIM), HEAD_DIM, GROUP_W))
    xs, _ = run_pass(x_sample, c, rope, (state_rwkv, cache_diff_k, cache_diff_v, cache_na_k, cache_na_v,
                                        cache_gqa_k, cache_gqa_v))
    return (xp, xs) + new_caches
```

```python
import functools
import math

import numpy as np
import jax
import jax.numpy as jnp
from jax import lax
from jax.experimental import pallas as pl
from jax.experimental.pallas import tpu as pltpu

F32 = jnp.float32
BF16 = jnp.bfloat16

D_MODEL = 1024
GRID_W = 64
HEAD_DIM = 64
GROUP_W = 256
GROUP_HEADS = 4
LORA_W = 64
LORA_A = 64
LORA_G = 128
RWKV_GN_EPS = 64e-5
DIFF_DQK = 32
GQA_KV_HEADS = 2
GQA_GROUP = 2
NA_ROWS = 8
NA_COLS = 16
ROPE_THETA = 10000.0
NORM_EPS = 1e-6
N_EXPERT_GROUPS = 4
EXPERTS_PER_GROUP = 8
N_EXPERTS = 32
TOP_K = 2
D_EXPERT = 512
A_IN = 3 * GROUP_W + 2 * LORA_W + 2 * LORA_A + LORA_G
B_IN = 3 * GROUP_W
C_IN = 3 * GROUP_W
D_IN = GROUP_W + 2 * GQA_KV_HEADS * HEAD_DIM
IN_W = A_IN + B_IN + C_IN + D_IN

LANES = 128
SUBLANES = 8
VMEM_LIMIT = 48 * 1024 * 1024

ROW_TILE = 256
Q_TILE = 512
SCAN_STEPS = 32
KV_CHUNK = 512
MOE_TILE = 256
ROUTER_W = 128
MASK_VALUE = -1e30
LOG2E = 1.4426950408889634


def _cparams(sem):
    return pltpu.CompilerParams(dimension_semantics=sem, vmem_limit_bytes=VMEM_LIMIT)


def _mod_kernel(c_ref, w_ref, b_ref, o_ref):
    c = c_ref[...]
    a = c * jax.nn.sigmoid(c)
    o_ref[...] = jnp.dot(a, w_ref[...], preferred_element_type=F32,
                         precision=lax.Precision.HIGHEST) + b_ref[...]


def _modulation(cond, w, b):
    m, d = cond.shape
    n = w.shape[1]
    mp = -(-m // SUBLANES) * SUBLANES
    cp = jnp.pad(cond, ((0, mp - m), (0, 0)))
    tn = 768
    out = pl.pallas_call(
        _mod_kernel,
        out_shape=jax.ShapeDtypeStruct((mp, n), F32),
        grid=(n // tn,),
        in_specs=[pl.BlockSpec((mp, d), lambda j: (0, 0)),
                  pl.BlockSpec((d, tn), lambda j: (0, j)),
                  pl.BlockSpec((1, tn), lambda j: (0, j))],
        out_specs=pl.BlockSpec((mp, tn), lambda j: (0, j)),
        compiler_params=_cparams(("arbitrary",)),
        name="modulation",
    )(cp, w, b.reshape(1, n))
    return out[:m]


def _inproj_kernel(x_ref, g_ref, sc_ref, sh_ref, w_ref, oa_ref, ob_ref, oc_ref, od_ref):
    x = x_ref[...]
    ms = jnp.mean(x * x, axis=-1, keepdims=True)
    h = x * lax.rsqrt(ms + NORM_EPS) * g_ref[...]
    h = h * sc_ref[0] + sh_ref[0]
    u = jnp.dot(h.astype(BF16), w_ref[...].astype(BF16), preferred_element_type=F32)
    oa_ref[...] = u[:, :A_IN]
    ob_ref[...] = u[:, A_IN:A_IN + B_IN]
    oc_ref[...] = u[:, A_IN + B_IN:A_IN + B_IN + C_IN]
    od_ref[...] = u[:, A_IN + B_IN + C_IN:]


def _in_projection(x2, gain, scale1p, shift, w_in, t_len):
    n, d = x2.shape
    bm = scale1p.shape[0]
    tm = ROW_TILE
    per_seq = t_len // tm
    b = n // t_len

    def mod_idx(i):
        return ((i // per_seq) if bm > 1 else 0, 0, 0)

    widths = (B_IN, C_IN, D_IN)
    ua, ub, uc, ud = pl.pallas_call(
        _inproj_kernel,
        out_shape=(jax.ShapeDtypeStruct((t_len, b * A_IN), F32),)
        + tuple(jax.ShapeDtypeStruct((n, w), F32) for w in widths),
        grid=(n // tm,),
        in_specs=[pl.BlockSpec((tm, d), lambda i: (i, 0)),
                  pl.BlockSpec((1, d), lambda i: (0, 0)),
                  pl.BlockSpec((1, 1, d), mod_idx),
                  pl.BlockSpec((1, 1, d), mod_idx),
                  pl.BlockSpec((d, IN_W), lambda i: (0, 0))],
        out_specs=(pl.BlockSpec((tm, A_IN), lambda i: (i % per_seq, i // per_seq)),)
        + tuple(pl.BlockSpec((tm, w), lambda i: (i, 0)) for w in widths),
        compiler_params=_cparams(("arbitrary",)),
        name="in_projection",
    )(x2, gain.reshape(1, d), scale1p.reshape(bm, 1, d), shift.reshape(bm, 1, d), w_in)
    return ua.reshape(t_len, b, A_IN), ub, uc, ud


def _head_sums(x, seg):
    hi = x.astype(BF16)
    lo = (x - hi.astype(F32)).astype(BF16)
    return (jnp.dot(hi, seg, preferred_element_type=F32) + jnp.dot(lo, seg, preferred_element_type=F32))


def _head_segments():
    idx = np.arange(GROUP_W) // HEAD_DIM
    return jnp.asarray(idx[:, None] == idx[None, :], BF16)


def _outproj_kernel(yf_ref, yb_ref, gb_ref, ob_ref, oc_ref, od_ref, gnw_ref, gnb_ref, seg_ref, w_ref, x_ref, g1_ref,
                    gn_ref, sc_ref, sh_ref, wr_ref, br_ref, xo_ref, h_ref, lg_ref):
    gw = GROUP_W
    o = yf_ref[...] + yb_ref[...]
    seg = seg_ref[...]
    mean = _head_sums(o, seg) * (1.0 / HEAD_DIM)
    cen = o - mean
    var = _head_sums(cen * cen, seg) * (1.0 / HEAD_DIM)
    oa = cen * lax.rsqrt(var + RWKV_GN_EPS) * gnw_ref[...] + gnb_ref[...]
    oa = (oa + gb_ref[:, gw:2 * gw]) * gb_ref[:, 0:gw]
    y = jnp.dot(oa.astype(BF16), w_ref[0:gw, :].astype(BF16), preferred_element_type=F32)
    for j, m_ref in enumerate((ob_ref, oc_ref, od_ref)):
        y = y + jnp.dot(m_ref[...].astype(BF16), w_ref[(j + 1) * gw:(j + 2) * gw, :].astype(BF16),
                        preferred_element_type=F32)
    xn = x_ref[...] + g1_ref[0] * y
    xo_ref[...] = xn
    ms = jnp.mean(xn * xn, axis=-1, keepdims=True)
    h = xn * lax.rsqrt(ms + NORM_EPS) * gn_ref[...]
    h = (h * sc_ref[0] + sh_ref[0]).astype(BF16)
    h_ref[...] = h
    lg_ref[...] = jnp.dot(h, wr_ref[...].astype(BF16), preferred_element_type=F32) + br_ref[...]


def _out_projection(yf, yb, gb, ob, oc, od, gn_w, gn_b, w_out, x2, gate1, gain, scale1p, shift, w_router, b_router,
                    t_len):
    n, d = x2.shape
    bm = gate1.shape[0]
    tm = ROW_TILE
    per_seq = t_len // tm
    gw = GROUP_W

    def mod_idx(i):
        return ((i // per_seq) if bm > 1 else 0, 0, 0)

    def tm_idx(i):
        return (i % per_seq, i // per_seq)

    def rows(w):
        return pl.BlockSpec((tm, w), lambda i: (i, 0))

    def full(shape):
        return pl.BlockSpec(shape, lambda i: (0,) * len(shape))

    mod_spec = pl.BlockSpec((1, 1, d), mod_idx)
    return pl.pallas_call(
        _outproj_kernel,
        out_shape=(jax.ShapeDtypeStruct((n, d), F32),
                   jax.ShapeDtypeStruct((n, d), BF16),
                   jax.ShapeDtypeStruct((n, ROUTER_W), F32)),
        grid=(n // tm,),
        in_specs=[pl.BlockSpec((tm, gw), tm_idx), pl.BlockSpec((tm, gw), tm_idx), pl.BlockSpec((tm, 2 * gw), tm_idx),
                  rows(gw), rows(gw), rows(gw), full((1, gw)), full((1, gw)), full((gw, gw)), full((d, d)), rows(d),
                  mod_spec, full((1, d)), mod_spec, mod_spec, full((d, ROUTER_W)), full((1, ROUTER_W))],
        out_specs=(rows(d), rows(d), rows(ROUTER_W)),
        compiler_params=_cparams(("arbitrary",)),
        name="out_projection",
    )(yf.reshape(t_len, -1), yb.reshape(t_len, -1), gb.reshape(t_len, -1), ob, oc, od,
      gn_w.reshape(1, gw), gn_b.reshape(1, gw), _head_segments(), w_out, x2, gate1.reshape(bm, 1, d),
      gain.reshape(1, d), scale1p.reshape(bm, 1, d), shift.reshape(bm, 1, d), w_router, b_router)


ACC_ROWS = HEAD_DIM + 16


def _flash_kernel(sc_ref, q_ref, *refs, heads, diff, chunk, has_ctx):
    if has_ctx:
        kc_ref, vtc_ref, k_ref, vt_ref, g_ref, o_ref, m_ref, acc_ref, sa_ref, sb_ref = refs
        n_ctx = kc_ref.shape[1] // chunk
    else:
        k_ref, vt_ref, g_ref, o_ref, m_ref, acc_ref, sa_ref, sb_ref = refs
        kc_ref = vtc_ref = None
        n_ctx = 0
    tq = q_ref.shape[1]
    kdim = k_ref.shape[2]
    n_lat = k_ref.shape[1] // chunk
    qt = q_ref[0].astype(F32).T.astype(BF16)

    def stationary(q_rows, k_row0):
        pieces = []
        if k_row0 > 0:
            pieces.append(jnp.zeros((k_row0, tq), BF16))
        pieces.append(q_rows)
        rest = kdim - k_row0 - q_rows.shape[0]
        if rest > 0:
            pieces.append(jnp.zeros((rest, tq), BF16))
        return jnp.concatenate(pieces, axis=0) if len(pieces) > 1 else pieces[0]

    streams = []
    for (maps, v_row0) in heads:
        for (qs, qn, ks) in maps:
            streams.append((stationary(qt[qs:qs + qn], ks), v_row0))
    m_ref[...] = jnp.full(m_ref.shape, MASK_VALUE, F32)
    acc_ref[...] = jnp.zeros(acc_ref.shape, F32)
    ones_rows = (lax.broadcasted_iota(jnp.int32, (ACC_ROWS - HEAD_DIM, chunk), 0) == 0).astype(BF16)

    def key_rows(src):
        kind, c = src
        if kind == "ctx":
            return kc_ref[0, c * chunk:(c + 1) * chunk, :]
        return k_ref[0, pl.ds(pl.multiple_of(c * chunk, chunk), chunk), :]

    def value_cols(src, v_row0):
        kind, c = src
        if kind == "ctx":
            return vtc_ref[0, v_row0:v_row0 + HEAD_DIM, c * chunk:(c + 1) * chunk]
        return vt_ref[0, pl.ds(v_row0, HEAD_DIM), pl.ds(pl.multiple_of(c * chunk, chunk), chunk)]

    def score(src, s_ref, j):
        s_ref[j] = jnp.dot(key_rows(src), streams[j][0], preferred_element_type=F32)

    def accumulate(src, s_ref, j):
        s = s_ref[j]
        m = m_ref[j]
        m_new = jnp.maximum(m, jnp.max(s, axis=0, keepdims=True))
        p = jnp.exp2(s - m_new).astype(BF16)
        alpha = jnp.exp2(m - m_new)
        m_ref[j] = m_new
        vc = jnp.concatenate([value_cols(src, streams[j][1]), ones_rows], axis=0)
        acc_ref[j] = alpha * acc_ref[j] + jnp.dot(vc, p, preferred_element_type=F32)

    ns = len(streams)

    def stage(nxt, next_ref, cur, cur_ref):
        for j in range(ns):
            if nxt is not None:
                score(nxt, next_ref, j)
            if cur is not None:
                accumulate(cur, cur_ref, j)

    bufs = (sa_ref, sb_ref)
    first = ("ctx", 0) if n_ctx else ("lat", 0)
    stage(first, bufs[0], None, None)
    for g in range(n_ctx):
        nxt = ("ctx", g + 1) if g + 1 < n_ctx else ("lat", 0)
        stage(nxt, bufs[(g + 1) % 2], ("ctx", g), bufs[g % 2])
    ra, rb = bufs[n_ctx % 2], bufs[(n_ctx + 1) % 2]

    def body(jj, carry):
        c = 2 * jj
        stage(("lat", c + 1), rb, ("lat", c), ra)
        stage(("lat", c + 2), ra, ("lat", c + 1), rb)
        return carry

    lax.fori_loop(0, (n_lat - 1) // 2, body, 0)
    if n_lat % 2 == 0:
        stage(("lat", n_lat - 1), rb, ("lat", n_lat - 2), ra)
        stage(None, None, ("lat", n_lat - 1), rb)
    else:
        stage(None, None, ("lat", n_lat - 1), ra)

    outs = []
    j = 0
    for (maps, v_row0) in heads:
        a1 = acc_ref[j]
        o = a1[:HEAD_DIM] / a1[HEAD_DIM:HEAD_DIM + 1]
        if diff:
            a2 = acc_ref[j + 1]
            o = o - (sc_ref[0] / a2[HEAD_DIM:HEAD_DIM + 1]) * a2[:HEAD_DIM]
            ms = jnp.mean(o * o, axis=0, keepdims=True)
            o = o * lax.rsqrt(ms + NORM_EPS) * g_ref[...] * sc_ref[1]
        j += len(maps)
        outs.append(o)
    o_ref[0] = jnp.concatenate(outs, axis=0).T


def _flash_attention(q, k, vt, heads, scalars=None, gain=None, k_ctx=None, vt_ctx=None):
    b, s, c = q.shape
    l, ck = k.shape[1:]
    cv = vt.shape[1]
    tq = min(Q_TILE, s)
    chunk = min(KV_CHUNK, l)
    assert l % chunk == 0 and s % tq == 0
    diff = scalars is not None
    has_ctx = k_ctx is not None
    ns = sum(len(maps) for maps, _ in heads)
    if not diff:
        scalars = jnp.zeros((2,), F32)
        gain = jnp.ones((HEAD_DIM,), F32)
    in_specs = [pl.BlockSpec((1, tq, c), lambda bi, i, sc: (bi, i, 0))]
    args = [q]
    if has_ctx:
        lc = k_ctx.shape[1]
        assert lc % chunk == 0
        in_specs += [pl.BlockSpec((1, lc, ck), lambda bi, i, sc: (bi, 0, 0)),
                     pl.BlockSpec((1, cv, lc), lambda bi, i, sc: (bi, 0, 0))]
        args += [k_ctx, vt_ctx]
    in_specs += [pl.BlockSpec((1, l, ck), lambda bi, i, sc: (bi, 0, 0)),
                 pl.BlockSpec((1, cv, l), lambda bi, i, sc: (bi, 0, 0)),
                 pl.BlockSpec((HEAD_DIM, 1), lambda bi, i, sc: (0, 0))]
    args += [k, vt, gain.reshape(HEAD_DIM, 1)]
    return pl.pallas_call(
        functools.partial(_flash_kernel, heads=heads, diff=diff, chunk=chunk, has_ctx=has_ctx),
        out_shape=jax.ShapeDtypeStruct((b, s, c), F32),
        grid_spec=pltpu.PrefetchScalarGridSpec(
            num_scalar_prefetch=1,
            grid=(b, s // tq),
            in_specs=in_specs,
            out_specs=pl.BlockSpec((1, tq, c), lambda bi, i, sc: (bi, i, 0)),
            scratch_shapes=[pltpu.VMEM((ns, 1, tq), F32), pltpu.VMEM((ns, ACC_ROWS, tq), F32),
                            pltpu.VMEM((ns, chunk, tq), F32), pltpu.VMEM((ns, chunk, tq), F32)]),
        compiler_params=_cparams(("arbitrary", "arbitrary")),
        name="flash_attention",
    )(scalars, *args)


DIFF_HEADS = tuple((((h * HEAD_DIM, HEAD_DIM // 2, h * HEAD_DIM),
                     (h * HEAD_DIM + HEAD_DIM // 2, HEAD_DIM // 2, h * HEAD_DIM + HEAD_DIM // 2)), h * HEAD_DIM)
                   for h in range(GROUP_HEADS))
DENSE_HEADS = tuple((((h * HEAD_DIM, HEAD_DIM, h * HEAD_DIM),), h * HEAD_DIM) for h in range(GROUP_HEADS))
GQA_HEADS = tuple((((h * HEAD_DIM, HEAD_DIM, (h // 2) * HEAD_DIM),), (h // 2) * HEAD_DIM) for h in range(GROUP_HEADS))


def _group_segments(width, group):
    idx = np.arange(width) // group
    return jnp.asarray(idx[:, None] == idx[None, :], BF16)


def _rope_tables(cs, dim, width):
    cos, sin = cs
    t = cos.shape[0]
    zero = jnp.zeros_like(sin)
    cos_l = jnp.stack([cos, cos], axis=2).reshape(t, dim)
    ta = jnp.stack([-sin, zero], axis=2).reshape(t, dim)
    tb = jnp.stack([zero, sin], axis=2).reshape(t, dim)
    rep = width // dim
    return tuple(jnp.tile(z, (1, rep)) for z in (cos_l, ta, tb))


def _aprep_kernel(ub_ref, uc_ref, ud_ref, cb_ref, ab_ref, bb_ref, cd_ref, ad_ref, bd_ref, gb_ref, gc_ref, gd_ref,
                  segb_ref, segd_ref, qb_ref, kb_ref, vbt_ref, qc_ref, kc_ref, vct_ref, qd_ref, kd_ref, vdt_ref):
    gw = GROUP_W
    kvw = GQA_KV_HEADS * HEAD_DIM

    def norm(x, seg, group, gain):
        ms = _head_sums(x * x, seg) * (1.0 / group)
        return x * lax.rsqrt(ms + NORM_EPS) * gain

    def norm_rope(x, seg, group, gain, cos, ta, tb, scale):
        x = norm(x, seg, group, gain)
        q4 = group // 4
        outs = []
        for c0 in range(0, x.shape[1], LANES):
            xc = x[:, c0:c0 + LANES]
            up = pltpu.roll(xc, LANES - q4, axis=1)
            dn = pltpu.roll(xc, q4, axis=1)
            outs.append(xc * cos[:, c0:c0 + LANES] + up * ta[:, c0:c0 + LANES] + dn * tb[:, c0:c0 + LANES])
        y = jnp.concatenate(outs, axis=1) if len(outs) > 1 else outs[0]
        if scale != 1.0:
            y = y * scale
        return y.astype(BF16)

    ub = ub_ref[...]
    segb = segb_ref[...]
    cb, ab, bb = cb_ref[...], ab_ref[...], bb_ref[...]
    qb_ref[...] = norm_rope(ub[:, 0:gw], segb, DIFF_DQK, gb_ref[0:1, :], cb, ab, bb, LOG2E / math.sqrt(DIFF_DQK))
    kb_ref[...] = norm_rope(ub[:, gw:2 * gw], segb, DIFF_DQK, gb_ref[1:2, :], cb, ab, bb, 1.0)
    vbt_ref[0] = ub[:, 2 * gw:3 * gw].T.astype(BF16)
    segd = segd_ref[...]
    uc = uc_ref[...]
    qc_ref[...] = (norm(uc[:, 0:gw], segd, HEAD_DIM, gc_ref[0:1, :]) * (LOG2E / math.sqrt(HEAD_DIM))).astype(BF16)
    kc_ref[...] = norm(uc[:, gw:2 * gw], segd, HEAD_DIM, gc_ref[1:2, :]).astype(BF16)
    vct_ref[0] = uc[:, 2 * gw:3 * gw].T.astype(BF16)
    ud = ud_ref[...]
    cd, ad, bd = cd_ref[...], ad_ref[...], bd_ref[...]
    qd_ref[...] = norm_rope(ud[:, 0:gw], segd, HEAD_DIM, gd_ref[0:1, :], cd, ad, bd, LOG2E / math.sqrt(HEAD_DIM))
    kd_ref[...] = norm_rope(ud[:, gw:gw + kvw], segd[0:kvw, 0:kvw], HEAD_DIM, gd_ref[1:2, 0:kvw],
                            cd[:, 0:kvw], ad[:, 0:kvw], bd[:, 0:kvw], 1.0)
    vdt_ref[0] = ud[:, gw + kvw:gw + 2 * kvw].T.astype(BF16)


def _attention_prepare(ub, uc, ud, rope, diff_norm, na_norm, gqa_norm, t_len):
    n = ub.shape[0]
    b = n // t_len
    tm = ROW_TILE
    per_seq = t_len // tm
    gw = GROUP_W
    kvw = GQA_KV_HEADS * HEAD_DIM
    tabs_b, tabs_d = rope
    gains_b = jnp.tile(diff_norm, (1, gw // DIFF_DQK))
    gains_c = jnp.tile(na_norm, (1, gw // HEAD_DIM))
    gains_d = jnp.tile(gqa_norm, (1, gw // HEAD_DIM))

    def rows(w):
        return pl.BlockSpec((tm, w), lambda i: (i, 0))

    def tab():
        return pl.BlockSpec((tm, gw), lambda i: (i % per_seq, 0))

    def full(shape):
        return pl.BlockSpec(shape, lambda i: (0,) * len(shape))

    def chan(w):
        return pl.BlockSpec((1, w, tm), lambda i: (i // per_seq, 0, i % per_seq))

    return pl.pallas_call(
        _aprep_kernel,
        out_shape=(jax.ShapeDtypeStruct((n, gw), BF16), jax.ShapeDtypeStruct((n, gw), BF16),
                   jax.ShapeDtypeStruct((b, gw, t_len), BF16),
                   jax.ShapeDtypeStruct((n, gw), BF16), jax.ShapeDtypeStruct((n, gw), BF16),
                   jax.ShapeDtypeStruct((b, gw, t_len), BF16),
                   jax.ShapeDtypeStruct((n, gw), BF16), jax.ShapeDtypeStruct((n, kvw), BF16),
                   jax.ShapeDtypeStruct((b, kvw, t_len), BF16)),
        grid=(n // tm,),
        in_specs=[rows(B_IN), rows(C_IN), rows(D_IN), tab(), tab(), tab(), tab(), tab(), tab(), full((2, gw)),
                  full((2, gw)), full((2, gw)), full((gw, gw)), full((gw, gw))],
        out_specs=(rows(gw), rows(gw), chan(gw), rows(gw), rows(gw), chan(gw), rows(gw), rows(kvw), chan(kvw)),
        compiler_params=_cparams(("arbitrary",)),
        name="attention_prepare",
    )(ub, uc, ud, *tabs_b, *tabs_d, gains_b, gains_c, gains_d, _group_segments(gw, DIFF_DQK),
      _group_segments(gw, HEAD_DIM))


NA_QROWS = 4
NA_SPAN = 12


def _na_kernel(q_ref, k_ref, vt_ref, kc_ref, vtc_ref, bias_ref, o_ref, *, grid_rows):
    i = pl.program_id(1)
    start = jnp.clip(i * NA_QROWS - NA_ROWS // 2, 0, grid_rows - NA_SPAN)
    off = pl.multiple_of(start * GRID_W, NA_QROWS * GRID_W)
    span = NA_SPAN * GRID_W
    tq = q_ref.shape[1]
    qt = q_ref[0].astype(F32).T.astype(BF16)
    kw = k_ref[0, pl.ds(off, span), :]
    kc = kc_ref[0]
    ones_w = (lax.broadcasted_iota(jnp.int32, (ACC_ROWS - HEAD_DIM, span), 0) == 0).astype(BF16)
    ones_c = (lax.broadcasted_iota(jnp.int32, (ACC_ROWS - HEAD_DIM, kc.shape[0]), 0) == 0).astype(BF16)
    outs = []
    for h in range(GROUP_HEADS):
        r0 = h * HEAD_DIM
        pieces = []
        if r0 > 0:
            pieces.append(jnp.zeros((r0, tq), BF16))
        pieces.append(qt[r0:r0 + HEAD_DIM])
        if r0 + HEAD_DIM < GROUP_W:
            pieces.append(jnp.zeros((GROUP_W - r0 - HEAD_DIM, tq), BF16))
        w = jnp.concatenate(pieces, axis=0)
        s_w = jnp.dot(kw, w, preferred_element_type=F32) + bias_ref[0, h]
        s_c = jnp.dot(kc, w, preferred_element_type=F32)
        m = jnp.maximum(jnp.max(s_w, axis=0, keepdims=True), jnp.max(s_c, axis=0, keepdims=True))
        p_w = jnp.exp2(s_w - m).astype(BF16)
        p_c = jnp.exp2(s_c - m).astype(BF16)
        vw = jnp.concatenate([vt_ref[0, pl.ds(r0, HEAD_DIM), pl.ds(off, span)], ones_w], axis=0)
        vc = jnp.concatenate([vtc_ref[0, r0:r0 + HEAD_DIM, :], ones_c], axis=0)
        acc = jnp.dot(vw, p_w, preferred_element_type=F32) + jnp.dot(vc, p_c, preferred_element_type=F32)
        outs.append(acc[:HEAD_DIM] / acc[HEAD_DIM:HEAD_DIM + 1])
    o_ref[0] = jnp.concatenate(outs, axis=0).T


def _na_bias_tables(rel_bias, grid_rows):
    half_r, half_c = NA_ROWS // 2, NA_COLS // 2
    kcol = np.arange(GRID_W)[:, None]
    qcol = np.arange(GRID_W)[None, :]
    cs = np.clip(qcol - half_c, 0, GRID_W - NA_COLS)
    valid_c = (kcol >= cs) & (kcol < cs + NA_COLS)
    cidx = np.clip(kcol - qcol + NA_COLS - 1, 0, 2 * NA_COLS - 2)
    col_sel = jnp.asarray(cidx[:, :, None] == np.arange(2 * NA_COLS - 1), F32)
    kr_l = np.arange(NA_SPAN)[:, None]
    qr_l = np.arange(NA_QROWS)[None, :]
    tabs = []
    for r0 in (0, NA_QROWS * (grid_rows // NA_QROWS // 2), grid_rows - NA_QROWS):
        start = int(np.clip(r0 - half_r, 0, grid_rows - NA_SPAN))
        kr = start + kr_l
        qr = r0 + qr_l
        rs = np.clip(qr - half_r, 0, grid_rows - NA_ROWS)
        valid_r = (kr >= rs) & (kr < rs + NA_ROWS)
        ridx = np.clip(kr - qr + NA_ROWS - 1, 0, 2 * NA_ROWS - 2)
        row_sel = jnp.asarray(ridx[:, :, None] == np.arange(2 * NA_ROWS - 1), F32)
        t = jnp.einsum('kqr,hrc,xyc->hkxqy', row_sel, rel_bias.astype(F32), col_sel,
                       precision=lax.Precision.HIGHEST) * LOG2E
        valid = valid_r[:, None, :, None] & valid_c[None, :, None, :]
        t = jnp.where(valid[None], t, MASK_VALUE)
        tabs.append(t.reshape(rel_bias.shape[0], NA_SPAN * GRID_W, NA_QROWS * GRID_W))
    return jnp.stack(tabs, axis=0)


def _neighborhood_attention(q, k, vt, k_ctx, vt_ctx, bias_tabs):
    b, s, c = q.shape
    lc = k_ctx.shape[1]
    grid_rows = s // GRID_W
    assert grid_rows >= NA_SPAN and grid_rows % NA_QROWS == 0
    nblk = grid_rows // NA_QROWS
    tq = NA_QROWS * GRID_W

    def bias_idx(bi, i):
        return (jnp.where(i == 0, 0, jnp.where(i == nblk - 1, 2, 1)), 0, 0, 0)

    return pl.pallas_call(
        functools.partial(_na_kernel, grid_rows=grid_rows),
        out_shape=jax.ShapeDtypeStruct((b, s, c), F32),
        grid=(b, nblk),
        in_specs=[pl.BlockSpec((1, tq, c), lambda bi, i: (bi, i, 0)),
                  pl.BlockSpec((1, s, c), lambda bi, i: (bi, 0, 0)),
                  pl.BlockSpec((1, c, s), lambda bi, i: (bi, 0, 0)),
                  pl.BlockSpec((1, lc, c), lambda bi, i: (bi, 0, 0)),
                  pl.BlockSpec((1, c, lc), lambda bi, i: (bi, 0, 0)),
                  pl.BlockSpec((1, GROUP_HEADS, NA_SPAN * GRID_W, tq), bias_idx)],
        out_specs=pl.BlockSpec((1, tq, c), lambda bi, i: (bi, i, 0)),
        compiler_params=_cparams(("arbitrary", "arbitrary")),
        name="neighborhood_attention",
    )(q, k, vt, k_ctx, vt_ctx, bias_tabs)


SCAN_BATCH = SUBLANES
N_KQ = 5
FWD_W = (N_KQ + 1) * GROUP_W
BWD_W = (N_KQ - 1) * GROUP_W
V_SPLIT = 4
V_TILES = HEAD_DIM // (V_SPLIT * SUBLANES)
KT_BUFFERS = 4
SCAN_ACCUMULATORS = 4


def _scan_kernel(zf_ref, zb_ref, zrv_ref, s0_ref, yf_ref, yb_ref, sfin_ref, st_ref, sa_ref, *kt_refs, steps):
    i = pl.program_id(1)

    @pl.when(i == 0)
    def _():
        st_ref[...] = s0_ref[0]

    lane = lax.broadcasted_iota(jnp.int32, (SUBLANES, LANES), 1)
    lane_vq = lane // (LANES // V_SPLIT)
    chan_vq = (lane // (V_TILES * SUBLANES)) % V_SPLIT

    def operand_tile(q, s):
        sb = steps - 1 - s
        pieces = []
        for hp in range(2):
            pieces.append(zf_ref[s, :, pl.ds(q * GROUP_W + hp * LANES, LANES)])
        for hp in range(2):
            if q < N_KQ - 1:
                pieces.append(zb_ref[sb, :, pl.ds(q * GROUP_W + hp * LANES, LANES)])
            else:
                pieces.append(zrv_ref[sb, :, pl.ds((q - (N_KQ - 1)) * GROUP_W + hp * LANES, LANES)])
        tile = jnp.concatenate(pieces * V_SPLIT, axis=0)
        return tile.T

    def prepare(kt_ref, s):
        for q in range(N_KQ + 1):
            kt_ref[q] = operand_tile(q, s)

    nacc = SCAN_ACCUMULATORS

    def tree(parts):
        while len(parts) > 1:
            parts = [parts[i] + parts[i + 1] for i in range(0, len(parts), 2)]
        return parts[0]

    prepare(kt_refs[0], 0)
    prepare(kt_refs[1], 1)
    for hpar in range(2):
        for vb in range(V_TILES):
            parts = [None] * nacc
            for k in range(HEAD_DIM):
                term = st_ref[hpar, k, vb] * kt_refs[0][0, pl.ds(hpar * HEAD_DIM + k, 1), :]
                parts[k % nacc] = term if parts[k % nacc] is None else parts[k % nacc] + term
            sa_ref[hpar, vb] = -tree(parts)

    def step(s, kt_ref, nxt_ref, far_ref):
        prepare(far_ref, jnp.minimum(s + 2, steps - 1))
        vt = kt_ref[N_KQ]
        ys = []
        for hpar in range(2):
            base = hpar * HEAD_DIM
            vops = []
            for vb in range(V_TILES):
                acc = None
                for vq in range(V_SPLIT):
                    r0 = base + vq * V_TILES * SUBLANES + vb * SUBLANES
                    piece = vt[r0:r0 + SUBLANES, :]
                    acc = piece if acc is None else jnp.where(lane_vq == vq, piece, acc)
                vops.append(acc)
            sas = [sa_ref[hpar, vb] for vb in range(V_TILES)]
            yp = [[None] * nacc for _ in range(V_TILES)]
            sp = [[None] * nacc for _ in range(V_TILES)]
            for k in range(HEAD_DIM):
                row = pl.ds(base + k, 1)
                w = kt_ref[1, row, :]
                ka = kt_ref[2, row, :]
                kd = kt_ref[3, row, :]
                rr = kt_ref[4, row, :]
                kn = nxt_ref[0, row, :]
                j = k % nacc
                for vb in range(V_TILES):
                    sn = st_ref[hpar, k, vb] * w + sas[vb] * ka + vops[vb] * kd
                    st_ref[hpar, k, vb] = sn
                    ty = sn * rr
                    yp[vb][j] = ty if yp[vb][j] is None else yp[vb][j] + ty
                    tn = sn * kn
                    sp[vb][j] = tn if sp[vb][j] is None else sp[vb][j] + tn
            for vb in range(V_TILES):
                sa_ref[hpar, vb] = -tree(sp[vb])
            ys.append([tree(yp[vb]) for vb in range(V_TILES)])
        ytile = jnp.concatenate([ys[hpar][vb] for hpar in range(2) for _ in range(V_SPLIT) for vb in range(V_TILES)],
                                axis=0)
        yt = ytile.T
        sb = steps - 1 - s
        for d in range(2):
            for hp in range(2):
                acc = None
                for vq in range(V_SPLIT):
                    r0 = vq * (LANES // V_SPLIT) + d * 2 * SUBLANES + hp * SUBLANES
                    piece = yt[r0:r0 + SUBLANES, :]
                    acc = piece if acc is None else jnp.where(chan_vq == vq, piece, acc)
                if d == 0:
                    yf_ref[s, :, pl.ds(hp * LANES, LANES)] = acc
                else:
                    yb_ref[sb, :, pl.ds(hp * LANES, LANES)] = acc

    nbuf = len(kt_refs)

    def body(j, carry):
        for u in range(nbuf):
            step(nbuf * j + u, kt_refs[u], kt_refs[(u + 1) % nbuf], kt_refs[(u + 2) % nbuf])
        return carry

    lax.fori_loop(0, steps // nbuf, body, 0)

    @pl.when(i == pl.num_programs(1) - 1)
    def _():
        sfin_ref[0] = st_ref[...]


def _rwkv_scan(zf, zb, s0):
    t_len, b, _ = zf.shape
    groups = b // SCAN_BATCH
    ts = SCAN_STEPS
    nblk = t_len // ts
    st_shape = (2, HEAD_DIM, V_TILES, SUBLANES, LANES)
    return pl.pallas_call(
        functools.partial(_scan_kernel, steps=ts),
        out_shape=(jax.ShapeDtypeStruct((t_len, b, GROUP_W), F32),
                   jax.ShapeDtypeStruct((t_len, b, GROUP_W), F32),
                   jax.ShapeDtypeStruct((groups,) + st_shape, F32)),
        grid=(groups, nblk),
        in_specs=[pl.BlockSpec((ts, SCAN_BATCH, FWD_W), lambda g, i: (i, g, 0)),
                  pl.BlockSpec((ts, SCAN_BATCH, BWD_W), lambda g, i: (nblk - 1 - i, g, 0)),
                  pl.BlockSpec((ts, SCAN_BATCH, 2 * GROUP_W), lambda g, i: (nblk - 1 - i, g, (N_KQ - 1) // 2)),
                  pl.BlockSpec((1,) + st_shape, lambda g, i: (g, 0, 0, 0, 0, 0))],
        out_specs=(pl.BlockSpec((ts, SCAN_BATCH, GROUP_W), lambda g, i: (i, g, 0)),
                   pl.BlockSpec((ts, SCAN_BATCH, GROUP_W), lambda g, i: (nblk - 1 - i, g, 0)),
                   pl.BlockSpec((1,) + st_shape, lambda g, i: (g, 0, 0, 0, 0, 0))),
        scratch_shapes=[pltpu.VMEM(st_shape, F32), pltpu.VMEM((2, V_TILES, SUBLANES, LANES), F32)]
        + [pltpu.VMEM((N_KQ + 1, LANES, LANES), F32)] * KT_BUFFERS,
        compiler_params=_cparams(("arbitrary", "arbitrary")),
        name="rwkv_scan",
    )(zf, zb, zf, s0)


PREP_STEPS = 32
DECAY_RATE = math.exp(-0.5)


def _prep_kernel(u_ref, up_ref, un_ref, mu_ref, w0_ref, wup_ref, a0_ref, aup_ref, gup_ref, kk_ref, ka_ref, rk_ref,
                 seg_ref, zf_ref, zb_ref, gb_ref):
    i = pl.program_id(1)
    last = pl.num_programs(1) - 1
    tb = u_ref.shape[0]
    u = u_ref[...]
    before = jnp.where(i > 0, up_ref[...], 0.0)
    after = jnp.where(i < last, un_ref[...], 0.0)
    prev = jnp.concatenate([before, u[:-1]], axis=0)
    nxt = jnp.concatenate([u[1:], after], axis=0)
    mu = mu_ref[...]
    x = (u + mu[0:1] * (prev - u) + mu[1:2] * (nxt - u)).reshape(tb * SCAN_BATCH, A_IN)
    gw = GROUP_W
    r = x[:, 0:gw]
    k = x[:, gw:2 * gw]
    v = x[:, 2 * gw:3 * gw]
    wd = x[:, 3 * gw:3 * gw + 2 * LORA_W]
    ad = x[:, 3 * gw + 2 * LORA_W:3 * gw + 2 * LORA_W + 2 * LORA_A]
    gd = x[:, 3 * gw + 2 * LORA_W + 2 * LORA_A:]
    seg = seg_ref[...]
    w_raw = w0_ref[...] + jnp.dot(jnp.tanh(wd).astype(BF16), wup_ref[...].astype(BF16), preferred_element_type=F32)
    decay = jnp.exp(-DECAY_RATE * jax.nn.sigmoid(w_raw))
    a = jax.nn.sigmoid(a0_ref[...] + jnp.dot(ad.astype(BF16), aup_ref[...].astype(BF16),
                                             preferred_element_type=F32))
    g = jnp.dot(jax.nn.sigmoid(gd).astype(BF16), gup_ref[...].astype(BF16), preferred_element_type=F32)
    bonus = jnp.zeros_like(r)
    for d, z_ref in enumerate((zf_ref, zb_ref)):
        sl = slice(d * gw, (d + 1) * gw)
        kk = k * kk_ref[:, sl]
        a_d = a[:, sl]
        kd = k * (1.0 + (a_d - 1.0) * ka_ref[:, sl])
        kk = kk * lax.rsqrt(_head_sums(kk * kk, seg) + 1e-12)
        bonus = bonus + _head_sums(r * kd * rk_ref[:, sl], seg) * v
        for q, val in enumerate((kk, decay[:, sl], kk * a_d, kd)):
            z_ref[:, :, q * gw:(q + 1) * gw] = val.reshape(tb, SCAN_BATCH, gw)
    zf_ref[:, :, 4 * gw:5 * gw] = r.reshape(tb, SCAN_BATCH, gw)
    zf_ref[:, :, 5 * gw:6 * gw] = v.reshape(tb, SCAN_BATCH, gw)
    gb_ref[:, :, 0:gw] = g.reshape(tb, SCAN_BATCH, gw)
    gb_ref[:, :, gw:2 * gw] = bonus.reshape(tb, SCAN_BATCH, gw)


def _block_diag(w):
    z = jnp.zeros_like(w[0])
    return jnp.concatenate([jnp.concatenate([w[0], z], axis=1), jnp.concatenate([z, w[1]], axis=1)], axis=0)


def _rwkv_prepare(ua_tm, mu, w0, w_up, a0, a_up, g_up, k_k, k_a, r_k):
    t, b, _ = ua_tm.shape
    tb = min(PREP_STEPS, t)
    nblk = t // tb
    gw2 = 2 * GROUP_W

    def full(shape):
        return pl.BlockSpec(shape, lambda g, i: (0,) * len(shape))

    return pl.pallas_call(
        _prep_kernel,
        out_shape=(jax.ShapeDtypeStruct((t, b, FWD_W), F32), jax.ShapeDtypeStruct((t, b, BWD_W), F32),
                   jax.ShapeDtypeStruct((t, b, gw2), F32)),
        grid=(b // SCAN_BATCH, nblk),
        in_specs=[pl.BlockSpec((tb, SCAN_BATCH, A_IN), lambda g, i: (i, g, 0)),
                  pl.BlockSpec((1, SCAN_BATCH, A_IN), lambda g, i: (jnp.maximum(i * tb - 1, 0), g, 0)),
                  pl.BlockSpec((1, SCAN_BATCH, A_IN), lambda g, i: (jnp.minimum((i + 1) * tb, t - 1), g, 0)),
                  full((2, A_IN)), full((1, gw2)), full((2 * LORA_W, gw2)), full((1, gw2)), full((2 * LORA_A, gw2)),
                  full((LORA_G, GROUP_W)), full((1, gw2)), full((1, gw2)), full((1, gw2)),
                  full((GROUP_W, GROUP_W))],
        out_specs=(pl.BlockSpec((tb, SCAN_BATCH, FWD_W), lambda g, i: (i, g, 0)),
                   pl.BlockSpec((tb, SCAN_BATCH, BWD_W), lambda g, i: (i, g, 0)),
                   pl.BlockSpec((tb, SCAN_BATCH, gw2), lambda g, i: (i, g, 0))),
        compiler_params=_cparams(("arbitrary", "arbitrary")),
        name="rwkv_prepare",
    )(ua_tm, ua_tm, ua_tm, mu, w0.reshape(1, gw2), _block_diag(w_up), a0.reshape(1, gw2), _block_diag(a_up), g_up,
      k_k.reshape(1, gw2), k_a.reshape(1, gw2), r_k.reshape(1, gw2), _head_segments())


def _rwkv_time_mix(ua_tm, s0, mu, w0, w_up, a0, a_up, g_up, k_k, k_a, r_k):
    t, b, _ = ua_tm.shape
    h, n = GROUP_HEADS, HEAD_DIM
    bp = -(-b // SCAN_BATCH) * SCAN_BATCH
    groups = bp // SCAN_BATCH
    if bp != b:
        ua_tm = jnp.pad(ua_tm, ((0, 0), (0, bp - b), (0, 0)))
    zf, zb, gb = _rwkv_prepare(ua_tm, mu, w0, w_up, a0, a_up, g_up, k_k, k_a, r_k)
    if s0 is None:
        s0l = jnp.zeros((groups, 2, n, V_TILES, SUBLANES, LANES), F32)
    else:
        s0 = jnp.pad(s0.astype(F32), ((0, bp - b),) + ((0, 0),) * 4)
        s0l = s0.reshape(groups, SCAN_BATCH, 2, 2, 2, V_SPLIT, V_TILES, SUBLANES, n)
        s0l = jnp.transpose(s0l, (0, 4, 8, 6, 7, 5, 2, 3, 1)).reshape(groups, 2, n, V_TILES, SUBLANES, LANES)
    yf, yb, s_fin = _rwkv_scan(zf, zb, s0l)
    s_fin = s_fin.reshape(groups, 2, n, V_TILES, SUBLANES, V_SPLIT, 2, 2, SCAN_BATCH)
    s_fin = jnp.transpose(s_fin, (0, 8, 6, 7, 1, 5, 3, 4, 2)).reshape(bp, 2, h, n, n)[:b]
    return yf, yb, gb, s_fin


def _expert_kernel(be_ref, nu_ref, x_ref, w1_ref, w3_ref, w2_ref, o_ref, w1s, w3s, w2s):
    i = pl.program_id(0)
    e = be_ref[i]
    prev = be_ref[jnp.maximum(i - 1, 0)]

    @pl.when((i == 0) | (e != prev))
    def _():
        w1s[...] = w1_ref[0].astype(BF16)
        w3s[...] = w3_ref[0].astype(BF16)
        w2s[...] = w2_ref[0].astype(BF16)

    @pl.when(i < nu_ref[0])
    def _():
        x = x_ref[...]
        a = jnp.dot(x, w1s[...], preferred_element_type=F32)
        g = jnp.dot(x, w3s[...], preferred_element_type=F32)
        hmid = (a * jax.nn.sigmoid(a)) * g
        o_ref[...] = jnp.dot(hmid.astype(BF16), w2s[...], preferred_element_type=F32).astype(o_ref.dtype)

    @pl.when(i >= nu_ref[0])
    def _():
        o_ref[...] = jnp.zeros_like(o_ref)


def _expert_mlp(xb, block_e, n_used, w1, w3, w2):
    cap, d = xb.shape
    bm = MOE_TILE
    de = w1.shape[-1]
    return pl.pallas_call(
        _expert_kernel,
        out_shape=jax.ShapeDtypeStruct((cap, d), F32),
        grid_spec=pltpu.PrefetchScalarGridSpec(
            num_scalar_prefetch=2,
            grid=(cap // bm,),
            in_specs=[pl.BlockSpec((bm, d), lambda i, be, nu: (i, 0)),
                      pl.BlockSpec((1, d, de), lambda i, be, nu: (be[i], 0, 0)),
                      pl.BlockSpec((1, d, de), lambda i, be, nu: (be[i], 0, 0)),
                      pl.BlockSpec((1, de, d), lambda i, be, nu: (be[i], 0, 0))],
            out_specs=pl.BlockSpec((bm, d), lambda i, be, nu: (i, 0)),
            scratch_shapes=[pltpu.VMEM((d, de), BF16), pltpu.VMEM((d, de), BF16), pltpu.VMEM((de, d), BF16)]),
        compiler_params=_cparams(("arbitrary",)),
        name="expert_mlp",
    )(block_e, n_used, xb, w1, w3, w2)


def _hier_moe(h_bf, logits, w1, w3, w2):
    n, d = h_bf.shape
    bm = MOE_TILE
    g_logits = logits[:, :N_EXPERT_GROUPS]
    g_idx = jnp.argmax(g_logits, axis=-1)
    g_top = jnp.max(jax.nn.softmax(g_logits, axis=-1), axis=-1)
    e_logits = logits[:, N_EXPERT_GROUPS:N_EXPERT_GROUPS + N_EXPERTS].reshape(n, N_EXPERT_GROUPS, EXPERTS_PER_GROUP)
    e_logits = jnp.take_along_axis(e_logits, g_idx[:, None, None], axis=1)[:, 0]
    top_v, top_i = lax.top_k(e_logits, TOP_K)
    gate = jax.nn.softmax(top_v, axis=-1) * g_top[:, None]
    eid = (g_idx[:, None] * EXPERTS_PER_GROUP + top_i).reshape(-1).astype(jnp.int32)
    tok = jnp.repeat(jnp.arange(n, dtype=jnp.int32), TOP_K)
    n_assign = n * TOP_K
    onehot = (eid[:, None] == jnp.arange(N_EXPERTS, dtype=jnp.int32)[None, :]).astype(jnp.int32)
    csum = jnp.cumsum(onehot, axis=0)
    counts = csum[-1]
    rank = jnp.take_along_axis(csum, eid[:, None], axis=1)[:, 0] - 1
    padded = (counts + bm - 1) // bm * bm
    pad_end = jnp.cumsum(padded)
    pad_start = pad_end - padded
    dest = pad_start[eid] + rank
    n_blocks = -(-(n_assign + N_EXPERTS * (bm - 1)) // bm)
    cap = n_blocks * bm
    buf_tok = jnp.zeros((cap,), jnp.int32).at[dest].set(tok)
    block_start = jnp.arange(n_blocks, dtype=jnp.int32) * bm
    block_e = jnp.minimum(jnp.sum((pad_end[None, :] <= block_start[:, None]).astype(jnp.int32), axis=1),
                          N_EXPERTS - 1).astype(jnp.int32)
    n_used = (pad_end[-1:] // bm).astype(jnp.int32)
    xb = h_bf[buf_tok]
    yb = _expert_mlp(xb, block_e, n_used, w1, w3, w2)
    d2 = dest.reshape(n, TOP_K)
    return yb[d2[:, 0]] * gate[:, 0:1] + yb[d2[:, 1]] * gate[:, 1:2]


def _rms(x, g):
    return x * lax.rsqrt(jnp.mean(x * x, axis=-1, keepdims=True) + NORM_EPS) * g


def _axial_rope(t_len, dim):
    q4 = dim // 4
    inv = ROPE_THETA ** (-jnp.arange(q4, dtype=F32) / q4)
    t = jnp.arange(t_len)
    row = (t // GRID_W).astype(F32)
    col = (t % GRID_W).astype(F32)
    ang = jnp.stack([row[:, None] * inv, col[:, None] * inv], axis=1)
    return jnp.cos(ang), jnp.sin(ang)


def _token_mixers(ua_tm, ub, uc, ud, rope, ctx, l, W):
    b, t, _ = ub.shape
    nh, n = GROUP_HEADS, HEAD_DIM
    latent = ctx is not None

    yf, yb, gb, s_fin = _rwkv_time_mix(ua_tm, ctx[0] if latent else None, W['rw_shift'][l], W['rw_w0'][l],
                                       W['rw_w_up'][l], W['rw_a0'][l], W['rw_a_up'][l], W['rw_g_up'][l],
                                       W['rw_k_k'][l], W['rw_k_a'][l], W['rw_r_k'][l])

    kvw = GQA_KV_HEADS * n
    if latent:
        qb, kb, vbt, qc, kc, vct, qd, kd, vdt = _attention_prepare(
            ub.reshape(b * t, -1), uc.reshape(b * t, -1), ud.reshape(b * t, -1), rope,
            W['diff_qk_norm'][l], W['na_qk_norm'][l], W['gqa_qk_norm'][l], t)

    lam_init = 0.8 - 0.6 * math.exp(-0.3 * l)
    lv = W['diff_lambda'][l].astype(F32)
    lam = jnp.exp(jnp.sum(lv[0] * lv[1])) - jnp.exp(jnp.sum(lv[2] * lv[3])) + lam_init
    scalars = jnp.stack([lam, jnp.asarray(1.0 - lam_init, F32)]).astype(F32)
    if latent:
        k_ctx = jnp.transpose(ctx[1], (0, 3, 1, 2, 4)).reshape(b, -1, GROUP_W).astype(BF16)
        vt_ctx = jnp.transpose(ctx[2], (0, 1, 3, 2)).reshape(b, GROUP_W, -1).astype(BF16)
        out_b = _flash_attention(qb.reshape(b, t, GROUP_W), kb.reshape(b, t, GROUP_W), vbt, DIFF_HEADS, scalars,
                                 W['diff_subln'][l], k_ctx=k_ctx, vt_ctx=vt_ctx)
    else:
        qb, kb, vb = jnp.split(ub, 3, axis=-1)
        qb = _rms(qb.reshape(b, t, nh, 2, DIFF_DQK), W['diff_qk_norm'][l, 0])
        kb = _rms(kb.reshape(b, t, nh, 2, DIFF_DQK), W['diff_qk_norm'][l, 1])
        out_b = _flash_attention((qb * (LOG2E / math.sqrt(DIFF_DQK))).reshape(b, t, GROUP_W).astype(BF16),
                                 kb.reshape(b, t, GROUP_W).astype(BF16), jnp.swapaxes(vb, 1, 2).astype(BF16),
                                 DIFF_HEADS, scalars, W['diff_subln'][l])

    if latent:
        k_ctx = jnp.transpose(ctx[3], (0, 2, 1, 3)).reshape(b, -1, GROUP_W).astype(BF16)
        vt_ctx = jnp.transpose(ctx[4], (0, 1, 3, 2)).reshape(b, GROUP_W, -1).astype(BF16)
        out_c = _neighborhood_attention(qc.reshape(b, t, GROUP_W), kc.reshape(b, t, GROUP_W), vct, k_ctx, vt_ctx,
                                        _na_bias_tables(W['na_rel_bias'][l], t // GRID_W))
    else:
        qc, kc, vc = jnp.split(uc, 3, axis=-1)
        qc = _rms(qc.reshape(b, t, nh, n), W['na_qk_norm'][l, 0])
        kc = _rms(kc.reshape(b, t, nh, n), W['na_qk_norm'][l, 1])
        out_c = _flash_attention((qc * (LOG2E / math.sqrt(n))).reshape(b, t, GROUP_W).astype(BF16),
                                 kc.reshape(b, t, GROUP_W).astype(BF16), jnp.swapaxes(vc, 1, 2).astype(BF16),
                                 DENSE_HEADS)

    if latent:
        k_ctx = jnp.transpose(ctx[5], (0, 2, 1, 3)).reshape(b, -1, kvw).astype(BF16)
        vt_ctx = jnp.transpose(ctx[6], (0, 1, 3, 2)).reshape(b, kvw, -1).astype(BF16)
        out_d = _flash_attention(qd.reshape(b, t, GROUP_W), kd.reshape(b, t, kvw), vdt, GQA_HEADS,
                                 k_ctx=k_ctx, vt_ctx=vt_ctx)
    else:
        qd, kd, vd = jnp.split(ud, [GROUP_W, GROUP_W + kvw], axis=-1)
        qd = _rms(qd.reshape(b, t, nh, n), W['gqa_qk_norm'][l, 0])
        kd = _rms(kd.reshape(b, t, GQA_KV_HEADS, n), W['gqa_qk_norm'][l, 1])
        out_d = _flash_attention((qd * (LOG2E / math.sqrt(n))).reshape(b, t, GROUP_W).astype(BF16),
                                 kd.reshape(b, t, kvw).astype(BF16), jnp.swapaxes(vd, 1, 2).astype(BF16), GQA_HEADS)

    mix = (yf, yb, gb) + tuple(z.reshape(b * t, GROUP_W) for z in (out_b, out_c, out_d))
    if latent:
        return mix, None
    new_ctx = (s_fin, jnp.transpose(kb, (0, 2, 3, 1, 4)), vb.reshape(b, t, nh, n).transpose(0, 2, 1, 3),
               kc.transpose(0, 2, 1, 3), vc.reshape(b, t, nh, n).transpose(0, 2, 1, 3),
               kd.transpose(0, 2, 1, 3), vd.reshape(b, t, GQA_KV_HEADS, n).transpose(0, 2, 1, 3))
    return mix, new_ctx


def _mixing_sublayer(x, cond, rope, ctx, l, W):
    b, t, d = x.shape
    n = b * t
    mod = _modulation(cond, W['w_mod'][l], W['b_mod'][l])
    sh1, sc1, g1, sh2, sc2, g2 = jnp.split(mod, 6, axis=-1)
    x2 = x.reshape(n, d)
    ua, ub, uc, ud = _in_projection(x2, W['norm_mix'][l], 1.0 + sc1, sh1, W['w_in'][l].astype(BF16), t)
    mix, new_ctx = _token_mixers(ua, ub.reshape(b, t, -1), uc.reshape(b, t, -1), ud.reshape(b, t, -1), rope, ctx, l, W)
    w_router = jnp.concatenate([W['moe_w_group'][l], W['moe_w_router'][l]], axis=1)
    w_router = jnp.pad(w_router, ((0, 0), (0, ROUTER_W - w_router.shape[1])))
    b_router = jnp.concatenate([W['moe_b_group'][l], W['moe_b_router'][l]])
    b_router = jnp.pad(b_router, (0, ROUTER_W - b_router.shape[0])).reshape(1, ROUTER_W)
    x2, h2, logits = _out_projection(*mix, W['rw_gn_w'][l], W['rw_gn_b'][l], W['w_out'][l].astype(BF16), x2, g1,
                                     W['norm_ffn'][l], 1.0 + sc2, sh2, w_router, b_router, t)
    g2r = jnp.broadcast_to(g2[:, None, :], (g2.shape[0], n // g2.shape[0], d)).reshape(n, d)
    return x2, h2, logits, g2r, new_ctx


def kernel(x_prompt, x_sample, c, state_rwkv, cache_diff_k, cache_diff_v, cache_na_k, cache_na_v,
           cache_gqa_k, cache_gqa_v, c_ctx, norm_mix, norm_ffn, w_mod, b_mod, w_in, w_out,
           rw_shift, rw_w0, rw_w_up, rw_a0, rw_a_up, rw_g_up, rw_k_k, rw_k_a, rw_r_k, rw_gn_w, rw_gn_b,
           diff_qk_norm, diff_lambda, diff_subln, na_qk_norm, na_rel_bias, gqa_qk_norm,
           moe_w_group, moe_b_group, moe_w_router, moe_b_router, moe_w1, moe_w3, moe_w2):
    W = {
        'norm_mix': norm_mix, 'norm_ffn': norm_ffn, 'w_mod': w_mod, 'b_mod': b_mod,
        'w_in': w_in, 'w_out': w_out, 'rw_shift': rw_shift, 'rw_w0': rw_w0, 'rw_w_up': rw_w_up,
        'rw_a0': rw_a0, 'rw_a_up': rw_a_up, 'rw_g_up': rw_g_up, 'rw_k_k': rw_k_k, 'rw_k_a': rw_k_a,
        'rw_r_k': rw_r_k, 'rw_gn_w': rw_gn_w, 'rw_gn_b': rw_gn_b, 'diff_qk_norm': diff_qk_norm,
        'diff_lambda': diff_lambda, 'diff_subln': diff_subln, 'na_qk_norm': na_qk_norm,
        'na_rel_bias': na_rel_bias, 'gqa_qk_norm': gqa_qk_norm, 'moe_w_group': moe_w_group,
        'moe_b_group': moe_b_group, 'moe_w_router': moe_w_router, 'moe_b_router': moe_b_router,
        'moe_w1': moe_w1, 'moe_w3': moe_w3, 'moe_w2': moe_w2,
    }
    depth = w_in.shape[0]

    def run_pass(x, cond, rope, caches):
        new_ctx = []
        for l in range(depth):
            ctx = None if caches is None else tuple(z[:, l] for z in caches)
            r2, h2, logits, g2r, nc = _mixing_sublayer(x, cond, rope, ctx, l, W)
            y = _hier_moe(h2, logits, W['moe_w1'][l], W['moe_w3'][l], W['moe_w2'][l])
            x = (r2 + g2r * y).reshape(x.shape)
            new_ctx.append(nc)
        return x, new_ctx

    xp, ctx_layers = run_pass(x_prompt, c_ctx[None, :], None, None)
    new_caches = tuple(jnp.stack([z[i] for z in ctx_layers], axis=1) for i in range(7))

    t_lat = x_sample.shape[1]
    rope = (_rope_tables(_axial_rope(t_lat, DIFF_DQK), DIFF_DQK, GROUP_W),
            _rope_tables(_axial_rope(t_lat, HEAD_DIM), HEAD_DIM, GROUP_W))
    xs, _ = run_pass(x_sample, c, rope, (state_rwkv, cache_diff_k, cache_diff_v, cache_na_k, cache_na_v,
                                        cache_gqa_k, cache_gqa_v))
    return (xp, xs) + new_caches
```
